```python
import math
import jax
import jax.numpy as jnp
from jax import lax
import numpy as np

D_MODEL = 1024
BATCH = 8
SEQ = 2048
DEPTH = 4

GRID_W = 64
CTX_LEN = 256
N_MIXERS = 4
D_FF = 4 * D_MODEL
Q_BLOCK = 128
ROPE_THETA = 10000.0
DEEPNORM_ALPHA = (2 * DEPTH) ** 0.25
DEEPNORM_BETA = (8 * DEPTH) ** -0.25
DN_HEADS = 8
DN_HEAD_DIM = 128
DN_WIDTH = DN_HEADS * DN_HEAD_DIM
DN_CONV = 5
DN_CHUNK = 64
DA_HEADS = 8
DA_HEAD_DIM = D_MODEL // DA_HEADS // 2
GA_HEADS = 8
GA_KV_HEADS = 2
GA_HEAD_DIM = D_MODEL // GA_HEADS
SS_GROUP = 16
SS_GROUPS = D_MODEL // SS_GROUP
SS_STATE = 64

kernel_name = 'hybrid_interleaved_diffusion_trunk'

F32 = jnp.float32


def n_layers_of(m):
    return (DEPTH - m + N_MIXERS - 1) // N_MIXERS


def layer_norm(x, g, b, eps=1e-5):
    xf = x.astype(F32)
    mu = jnp.mean(xf, -1, keepdims=True)
    var = jnp.mean(jnp.square(xf - mu), -1, keepdims=True)
    return ((xf - mu) * lax.rsqrt(var + eps) * g.astype(F32) + b.astype(F32)).astype(x.dtype)


def rms_norm(x, g, eps=1e-6):
    xf = x.astype(F32)
    return (xf * lax.rsqrt(jnp.mean(xf * xf, -1, keepdims=True) + eps) * g.astype(F32)).astype(x.dtype)


def l2_norm(x, eps=1e-6):
    xf = x.astype(F32)
    return xf * lax.rsqrt(jnp.sum(xf * xf, -1, keepdims=True) + eps)


def axial_rope_tables(n_tokens, head_dim):
    rows = n_tokens // GRID_W
    row = jnp.repeat(jnp.arange(rows), GRID_W).astype(F32)
    col = jnp.tile(jnp.arange(GRID_W), rows).astype(F32)
    n_freq = head_dim // 4
    inv_freq = ROPE_THETA ** (-jnp.arange(n_freq, dtype=F32) / n_freq)
    ang = jnp.concatenate([row[:, None] * inv_freq, col[:, None] * inv_freq], -1)
    return jnp.cos(ang), jnp.sin(ang)


def apply_rope(x, cos, sin):
    shp = (x.shape[1],) + (1,) * (x.ndim - 3) + (x.shape[-1] // 2,)
    cos = cos.reshape(shp)
    sin = sin.reshape(shp)
    xf = x.astype(F32).reshape(*x.shape[:-1], -1, 2)
    x1, x2 = xf[..., 0], xf[..., 1]
    return jnp.stack([x1 * cos - x2 * sin, x1 * sin + x2 * cos], -1).reshape(x.shape).astype(x.dtype)


def map_query_blocks(fn, q):
    b, n = q.shape[:2]
    qb = jnp.swapaxes(q.reshape(b, n // Q_BLOCK, Q_BLOCK, *q.shape[2:]), 0, 1)
    out = lax.map(fn, qb)
    return jnp.swapaxes(out, 0, 1).reshape(b, n, *out.shape[3:])


def centred_depthwise_conv(x, w):
    k, ch = w.shape
    return lax.conv_general_dilated(x, w[:, None, :].astype(x.dtype), window_strides=(1,),
                                    padding=[(k // 2, k // 2)],
                                    dimension_numbers=('NWC', 'WIO', 'NWC'),
                                    feature_group_count=ch)


def squared_relu_mlp(h, w1, w2):
    return jnp.square(jax.nn.relu(h @ w1)) @ w2


def gated_delta_chunked(q, k, v, log_decay, beta, s0):
    b, n, h, dk = q.shape
    dv = v.shape[-1]
    nc = n // DN_CHUNK

    def to_chunks(t):
        t = t.astype(F32).reshape(b, nc, DN_CHUNK, h, *t.shape[3:])
        return jnp.moveaxis(jnp.moveaxis(t, 1, 0), 3, 2)

    q, k, v, g, bt = (to_chunks(t) for t in (q, k, v, log_decay, beta))
    gc = jnp.cumsum(g, -1)
    idx = jnp.arange(DN_CHUNK)
    causal = idx[:, None] >= idx[None, :]
    strict = idx[:, None] > idx[None, :]
    decay = jnp.exp(jnp.where(causal, gc[..., :, None] - gc[..., None, :], -jnp.inf))
    a = jnp.where(strict, bt[..., :, None] * jnp.einsum('nbhid,nbhjd->nbhij', k, k) * decay, 0.0)
    eye = jnp.eye(DN_CHUNK, dtype=F32)
    t_inv = lax.linalg.triangular_solve(eye + a, jnp.broadcast_to(eye, a.shape), left_side=True,
                                        lower=True, unit_diagonal=True)
    u = t_inv @ (bt[..., None] * v)
    w = t_inv @ (bt[..., None] * jnp.exp(gc)[..., None] * k)
    qk = jnp.einsum('nbhid,nbhjd->nbhij', q, k) * decay
    q_dec = q * jnp.exp(gc)[..., None]
    k_dec = k * jnp.exp(gc[..., -1:] - gc)[..., None]
    g_last = jnp.exp(gc[..., -1])[..., None, None]

    def step(s, inp):
        u_c, w_c, qk_c, qd_c, kd_c, gl_c = inp
        v_new = u_c - w_c @ s
        o = qd_c @ s + qk_c @ v_new
        s = gl_c * s + jnp.swapaxes(kd_c, -1, -2) @ v_new
        return s, o

    s_fin, o = lax.scan(step, s0.astype(F32), (u, w, qk, q_dec, k_dec, g_last))
    o = jnp.moveaxis(jnp.moveaxis(o, 2, 3), 0, 1).reshape(b, n, h, dv)
    return o, s_fin


def deltanet_mixer(h_c, h_x, w_in, conv_w, a_log, dt_bias, norm_g, w_out, ctx_out):
    def project(h):
        b, n, _ = h.shape
        p = h @ w_in
        qkv = jax.nn.silu(centred_depthwise_conv(p[..., :3 * DN_WIDTH], conv_w))
        q, k, v = jnp.split(qkv, 3, -1)
        q = l2_norm(q.reshape(b, n, DN_HEADS, DN_HEAD_DIM)) * DN_HEAD_DIM ** -0.5
        k = l2_norm(k.reshape(b, n, DN_HEADS, DN_HEAD_DIM))
        v = v.reshape(b, n, DN_HEADS, DN_HEAD_DIM)
        z = p[..., 3 * DN_WIDTH:4 * DN_WIDTH]
        gates = p[..., 4 * DN_WIDTH:].astype(F32).reshape(b, n, 2, 2, DN_HEADS)
        beta = jax.nn.sigmoid(gates[:, :, 0])
        log_decay = -jnp.exp(a_log.astype(F32)) * jax.nn.softplus(gates[:, :, 1] + dt_bias.astype(F32))
        return q, k, v, z, beta, log_decay

    def rev(t):
        return jnp.flip(t, 1)

    qc, kc, vc, zc, bc, gc = project(h_c)
    qx, kx, vx, zx, bx, gx = project(h_x)
    s0 = jnp.zeros((h_x.shape[0], DN_HEADS, DN_HEAD_DIM, DN_HEAD_DIM), F32)
    oc_f, sc_f = gated_delta_chunked(qc, kc, vc, gc[:, :, 0], bc[:, :, 0], s0)
    ox_f, _ = gated_delta_chunked(qx, kx, vx, gx[:, :, 0], bx[:, :, 0], sc_f)
    oc_b, sc_b = gated_delta_chunked(rev(qc), rev(kc), rev(vc), rev(gc[:, :, 1]), rev(bc[:, :, 1]), s0)
    ox_b, _ = gated_delta_chunked(rev(qx), rev(kx), rev(vx), rev(gx[:, :, 1]), rev(bx[:, :, 1]), sc_b)

    def out(o, z):
        b, n = z.shape[:2]
        o = rms_norm(o, norm_g).reshape(b, n, DN_WIDTH) * jax.nn.silu(z.astype(F32))
        return o.astype(z.dtype) @ w_out

    y_x = out(ox_f + rev(ox_b), zx)
    y_c = out(oc_f + rev(oc_b), zc) if ctx_out else None
    return y_c, y_x


def diff_attention_mixer(h_c, h_x, w_qkv, lam_p, norm_g, w_out, lambda_init, ctx_out):
    n = h_x.shape[1]
    cos, sin = axial_rope_tables(n, DA_HEAD_DIM)

    def project(h):
        sh = h.shape[:2]
        q, k, v = jnp.split(h @ w_qkv, 3, -1)
        return (q.reshape(*sh, DA_HEADS, 2, DA_HEAD_DIM), k.reshape(*sh, DA_HEADS, 2, DA_HEAD_DIM),
                v.reshape(*sh, DA_HEADS, 2 * DA_HEAD_DIM))

    qc, kc, vc = project(h_c)
    qx, kx, vx = project(h_x)
    qx = apply_rope(qx, cos, sin)
    kx = apply_rope(kx, cos, sin)
    lp = lam_p.astype(F32)
    lam = jnp.exp(jnp.sum(lp[0] * lp[1])) - jnp.exp(jnp.sum(lp[2] * lp[3])) + lambda_init
    scale = DA_HEAD_DIM ** -0.5

    def attend(q, k, v):
        def block(qb):
            s = jnp.einsum('bqhcd,bshcd->bhcqs', qb, k).astype(F32) * scale
            p = jax.nn.softmax(s, -1)
            a = p[:, :, 0] - lam * p[:, :, 1]
            return jnp.einsum('bhqs,bshe->bqhe', a.astype(v.dtype), v)
        o = map_query_blocks(block, q)
        o = rms_norm(o, norm_g) * (1.0 - lambda_init)
        return o.reshape(*q.shape[:2], D_MODEL) @ w_out

    y_x = attend(qx, jnp.concatenate([kc, kx], 1), jnp.concatenate([vc, vx], 1))
    y_c = attend(qc, kc, vc) if ctx_out else None
    return y_c, y_x


def gqa_mixer(h_c, h_x, w_qkv, q_norm, k_norm, w_out, ctx_out):
    n = h_x.shape[1]
    cos, sin = axial_rope_tables(n, GA_HEAD_DIM)
    group = GA_HEADS // GA_KV_HEADS
    hd = GA_HEAD_DIM

    def project(h):
        sh = h.shape[:2]
        p = h @ w_qkv
        q = p[..., :GA_HEADS * hd].reshape(*sh, GA_KV_HEADS, group, hd)
        k = p[..., GA_HEADS * hd:(GA_HEADS + GA_KV_HEADS) * hd].reshape(*sh, GA_KV_HEADS, hd)
        v = p[..., (GA_HEADS + GA_KV_HEADS) * hd:].reshape(*sh, GA_KV_HEADS, hd)
        return rms_norm(q, q_norm), rms_norm(k, k_norm), v

    qc, kc, vc = project(h_c)
    qx, kx, vx = project(h_x)
    qx = apply_rope(qx, cos, sin)
    kx = apply_rope(kx, cos, sin)
    scale = hd ** -0.5

    def attend(q, k, v):
        def block(qb):
            s = jnp.einsum('bqkgd,bskd->bkgqs', qb, k).astype(F32) * scale
            p = jax.nn.softmax(s, -1).astype(v.dtype)
            return jnp.einsum('bkgqs,bskd->bqkgd', p, v)
        o = map_query_blocks(block, q)
        return o.reshape(*q.shape[:2], GA_HEADS * hd) @ w_out

    y_x = attend(qx, jnp.concatenate([kc, kx], 1), jnp.concatenate([vc, vx], 1))
    y_c = attend(qc, kc, vc) if ctx_out else None
    return y_c, y_x


def s5_mixer(h_c, h_x, a_re, a_im, log_dt, b_re, b_im, c_re, c_im, d_skip, w_glu, ctx_out):
    lam = lax.complex(jnp.minimum(a_re.astype(F32), -1e-4), a_im.astype(F32))
    lam_dt = lam * jnp.exp(log_dt.astype(F32))[..., None]
    lam_bar = jnp.exp(lam_dt)
    b_bar = ((lam_bar - 1.0) / lam)[..., None] * lax.complex(b_re.astype(F32), b_im.astype(F32))
    c_mat = lax.complex(c_re.astype(F32), c_im.astype(F32))

    def combine(e1, e2):
        a1, b1 = e1
        a2, b2 = e2
        return a1 * a2, a2 * b1 + b2

    def scan(u, dd, h0, reverse):
        n = u.shape[1]
        bu = jnp.einsum('gpi,blgi->blgp', b_bar[dd], u.astype(jnp.complex64))
        if h0 is not None:
            pos = -1 if reverse else 0
            bu = bu.at[:, pos].add(lam_bar[dd] * h0)
        a = jnp.broadcast_to(lam_bar[dd], (1, n) + lam_bar.shape[1:])
        _, st = lax.associative_scan(combine, (a, bu), axis=1, reverse=reverse)
        return st

    def readout(st, dd):
        return jnp.real(jnp.einsum('gip,blgp->blgi', c_mat[dd], st))

    def glu_out(u, y):
        b, n = u.shape[:2]
        y = (y + d_skip.astype(F32).reshape(SS_GROUPS, SS_GROUP) * u).reshape(b, n, D_MODEL)
        z = jax.nn.gelu(y) @ w_glu
        return z[..., :D_MODEL] * jax.nn.sigmoid(z[..., D_MODEL:])

    uc = h_c.astype(F32).reshape(h_c.shape[0], h_c.shape[1], SS_GROUPS, SS_GROUP)
    ux = h_x.astype(F32).reshape(h_x.shape[0], h_x.shape[1], SS_GROUPS, SS_GROUP)
    st_cf = scan(uc, 0, None, False)
    st_cb = scan(uc, 1, None, True)
    y_x = readout(scan(ux, 0, st_cf[:, -1], False), 0)
    y_x = y_x + readout(scan(ux, 1, st_cb[:, 0], True), 1)
    y_x = glu_out(ux, y_x).astype(h_x.dtype)
    y_c = glu_out(uc, readout(st_cf, 0) + readout(st_cb, 1)).astype(h_c.dtype) if ctx_out else None
    return y_c, y_x


def setup_inputs(seed: int = 0) -> dict:
    key = jax.random.key(seed)
    ks = iter(jax.random.split(key, 48))
    D = D_MODEL
    beta = DEEPNORM_BETA

    def nrm(shape, scale):
        return jax.random.normal(next(ks), shape, F32) * scale

    def unif(shape, lo, hi):
        return jax.random.uniform(next(ks), shape, F32, lo, hi)

    n_a, n_b, n_c, n_d = (n_layers_of(m) for m in range(N_MIXERS))
    x = nrm((BATCH, SEQ, D), 1.0)
    c = nrm((BATCH, D), 1.0)
    ctx = nrm((BATCH, CTX_LEN, D), 1.0)
    c_ctx = nrm((D,), 1.0)
    ada_w = nrm((DEPTH, D, 6 * D), D ** -0.5)
    ada_b = nrm((DEPTH, 6 * D), 0.02)
    ln_g = 1.0 + nrm((DEPTH, 2, D), 0.05)
    ln_b = nrm((DEPTH, 2, D), 0.02)
    mlp_w1 = nrm((DEPTH, D, D_FF), D ** -0.5)
    mlp_w2 = nrm((DEPTH, D_FF, D), D_FF ** -0.5 * beta)
    dn_w_in = nrm((n_a, D, 4 * DN_WIDTH + 4 * DN_HEADS), D ** -0.5)
    dn_conv = nrm((n_a, DN_CONV, 3 * DN_WIDTH), DN_CONV ** -0.5)
    dn_a_log = jnp.log(unif((n_a, 2, DN_HEADS), 1.0, 16.0))
    dt = jnp.exp(unif((n_a, 2, DN_HEADS), math.log(1e-3), math.log(1e-1)))
    dn_dt_bias = dt + jnp.log(-jnp.expm1(-dt))
    dn_norm_g = 1.0 + nrm((n_a, DN_HEAD_DIM), 0.05)
    dn_w_out = nrm((n_a, DN_WIDTH, D), DN_WIDTH ** -0.5 * beta)
    da_w_qkv = nrm((n_b, D, 3 * D), D ** -0.5)
    da_lambda = nrm((n_b, 4, DA_HEAD_DIM), 0.1)
    da_norm_g = 1.0 + nrm((n_b, 2 * DA_HEAD_DIM), 0.05)
    da_w_out = nrm((n_b, D, D), D ** -0.5 * beta)
    ga_w_qkv = nrm((n_c, D, (GA_HEADS + 2 * GA_KV_HEADS) * GA_HEAD_DIM), D ** -0.5)
    ga_q_norm = 1.0 + nrm((n_c, GA_HEAD_DIM), 0.05)
    ga_k_norm = 1.0 + nrm((n_c, GA_HEAD_DIM), 0.05)
    ga_w_out = nrm((n_c, GA_HEADS * GA_HEAD_DIM, D), (GA_HEADS * GA_HEAD_DIM) ** -0.5 * beta)
    state_idx = jnp.arange(SS_STATE, dtype=F32)
    ss_a_re = -0.5 + nrm((n_d, 2, SS_GROUPS, SS_STATE), 0.01)
    ss_a_im = math.pi * state_idx + nrm((n_d, 2, SS_GROUPS, SS_STATE), 0.01)
    ss_log_dt = unif((n_d, 2, SS_GROUPS), math.log(1e-3), math.log(1e-1))
    ss_b_re = nrm((n_d, 2, SS_GROUPS, SS_STATE, SS_GROUP), (2 * SS_GROUP) ** -0.5)
    ss_b_im = nrm((n_d, 2, SS_GROUPS, SS_STATE, SS_GROUP), (2 * SS_GROUP) ** -0.5)
    ss_c_re = nrm((n_d, 2, SS_GROUPS, SS_GROUP, SS_STATE), (2 * SS_STATE) ** -0.5)
    ss_c_im = nrm((n_d, 2, SS_GROUPS, SS_GROUP, SS_STATE), (2 * SS_STATE) ** -0.5)
    ss_d = nrm((n_d, D), 1.0)
    ss_w_glu = jnp.concatenate([nrm((n_d, D, D), D ** -0.5 * beta), nrm((n_d, D, D), D ** -0.5)], -1)
    return {'x': x, 'c': c, 'ctx': ctx, 'c_ctx': c_ctx,
            'ada_w': ada_w, 'ada_b': ada_b, 'ln_g': ln_g, 'ln_b': ln_b,
            'mlp_w1': mlp_w1, 'mlp_w2': mlp_w2,
            'dn_w_in': dn_w_in, 'dn_conv': dn_conv, 'dn_a_log': dn_a_log, 'dn_dt_bias': dn_dt_bias,
            'dn_norm_g': dn_norm_g, 'dn_w_out': dn_w_out,
            'da_w_qkv': da_w_qkv, 'da_lambda': da_lambda, 'da_norm_g': da_norm_g, 'da_w_out': da_w_out,
            'ga_w_qkv': ga_w_qkv, 'ga_q_norm': ga_q_norm, 'ga_k_norm': ga_k_norm, 'ga_w_out': ga_w_out,
            'ss_a_re': ss_a_re, 'ss_a_im': ss_a_im, 'ss_log_dt': ss_log_dt,
            'ss_b_re': ss_b_re, 'ss_b_im': ss_b_im, 'ss_c_re': ss_c_re, 'ss_c_im': ss_c_im,
            'ss_d': ss_d, 'ss_w_glu': ss_w_glu}


def reference(x, c, ctx, c_ctx, ada_w, ada_b, ln_g, ln_b, mlp_w1, mlp_w2,
              dn_w_in, dn_conv, dn_a_log, dn_dt_bias, dn_norm_g, dn_w_out,
              da_w_qkv, da_lambda, da_norm_g, da_w_out,
              ga_w_qkv, ga_q_norm, ga_k_norm, ga_w_out,
              ss_a_re, ss_a_im, ss_log_dt, ss_b_re, ss_b_im, ss_c_re, ss_c_im, ss_d, ss_w_glu):
    act_x = jax.nn.silu(c)
    act_c = jax.nn.silu(c_ctx)
    alpha = DEEPNORM_ALPHA
    for i in range(DEPTH):
        m, j = i % N_MIXERS, i // N_MIXERS
        last = i == DEPTH - 1
        sh1, sc1, g1, sh2, sc2, g2 = jnp.split((act_x @ ada_w[i] + ada_b[i])[:, None, :], 6, -1)
        ch1, cs1, cg1, ch2, cs2, cg2 = jnp.split(act_c @ ada_w[i] + ada_b[i], 6, -1)
        hx = x * (1.0 + sc1) + sh1
        hc = ctx * (1.0 + cs1) + ch1
        if m == 0:
            yc, yx = deltanet_mixer(hc, hx, dn_w_in[j], dn_conv[j], dn_a_log[j], dn_dt_bias[j],
                                    dn_norm_g[j], dn_w_out[j], not last)
        elif m == 1:
            lambda_init = 0.8 - 0.6 * math.exp(-0.3 * i)
            yc, yx = diff_attention_mixer(hc, hx, da_w_qkv[j], da_lambda[j], da_norm_g[j], da_w_out[j],
                                          lambda_init, not last)
        elif m == 2:
            yc, yx = gqa_mixer(hc, hx, ga_w_qkv[j], ga_q_norm[j], ga_k_norm[j], ga_w_out[j], not last)
        else:
            yc, yx = s5_mixer(hc, hx, ss_a_re[j], ss_a_im[j], ss_log_dt[j], ss_b_re[j], ss_b_im[j],
                              ss_c_re[j], ss_c_im[j], ss_d[j], ss_w_glu[j], not last)
        x = layer_norm(alpha * x + g1 * yx, ln_g[i, 0], ln_b[i, 0])
        x = layer_norm(alpha * x + g2 * squared_relu_mlp(x * (1.0 + sc2) + sh2, mlp_w1[i], mlp_w2[i]),
                       ln_g[i, 1], ln_b[i, 1])
        if not last:
            ctx = layer_norm(alpha * ctx + cg1 * yc, ln_g[i, 0], ln_b[i, 0])
            ctx = layer_norm(alpha * ctx + cg2 * squared_relu_mlp(ctx * (1.0 + cs2) + ch2, mlp_w1[i], mlp_w2[i]),
                             ln_g[i, 1], ln_b[i, 1])
    return x
```

```python
import functools
import math

import jax
import jax.numpy as jnp
from jax import lax
from jax.experimental import pallas as pl
from jax.experimental.pallas import tpu as pltpu

F32 = jnp.float32
BF16 = jnp.bfloat16

D_MODEL = 1024
D_FF = 4 * D_MODEL
DEPTH = 4
GRID_W = 64
CTX_LEN = 256
ROPE_THETA = 10000.0
DEEPNORM_ALPHA = (2 * DEPTH) ** 0.25
TM = 256
LANE = 128
MOD_ROWS = 16
VMEM_LIMIT = 56 * 1024 * 1024

DN_HEADS = 8
DN_HEAD_DIM = 128
DN_WIDTH = DN_HEADS * DN_HEAD_DIM
DN_CONV = 5
DN_CHUNK = 64
DA_HEADS = 8
DA_HEAD_DIM = 64
GA_HEADS = 8
GA_KV_HEADS = 2
GA_HEAD_DIM = 128
SS_GROUP = 16
SS_GROUPS = D_MODEL // SS_GROUP
SS_STATE = 64


def _params(n_axes):
    return pltpu.CompilerParams(dimension_semantics=("arbitrary",) * n_axes,
                                vmem_limit_bytes=VMEM_LIMIT)


def _layer_norm(v, g, b):
    mu = jnp.mean(v, -1, keepdims=True)
    d = v - mu
    var = jnp.mean(d * d, -1, keepdims=True)
    return d * lax.rsqrt(var + 1e-5) * g + b


def _mod_spec(layer):
    return pl.BlockSpec((1, 6, D_MODEL),
                        lambda b, r: (layer * MOD_ROWS + jnp.where(r == 0, 8, b), 0, 0))


def _ada_kernel(c_ref, w_ref, b_ref, o_ref):
    c = c_ref[...]
    act = (c * jax.nn.sigmoid(c)).astype(BF16)
    o_ref[0] = jnp.dot(act, w_ref[0].astype(BF16), preferred_element_type=F32) + b_ref[0]


def _ada_all(c_rows, ada_w, ada_b):
    tn = 1536
    n = 6 * D_MODEL
    return pl.pallas_call(
        _ada_kernel,
        grid=(DEPTH, n // tn),
        in_specs=[pl.BlockSpec((MOD_ROWS, D_MODEL), lambda i, j: (0, 0)),
                  pl.BlockSpec((1, D_MODEL, tn), lambda i, j: (i, 0, j)),
                  pl.BlockSpec((1, 1, tn), lambda i, j: (i, 0, j))],
        out_specs=pl.BlockSpec((1, MOD_ROWS, tn), lambda i, j: (i, 0, j)),
        out_shape=jax.ShapeDtypeStruct((DEPTH, MOD_ROWS, n), F32),
        compiler_params=_params(2),
        name="ada_mod",
    )(c_rows, ada_w, ada_b.reshape(DEPTH, 1, n))


def _proj_kernel(x_ref, mod_ref, w_ref, o_ref):
    sh = mod_ref[0, 0:1, :]
    sc = mod_ref[0, 1:2, :]
    h = (x_ref[0] * (1.0 + sc) + sh).astype(BF16)
    o_ref[0] = jnp.dot(h, w_ref[...], preferred_element_type=F32).astype(o_ref.dtype)


def _proj(xs, mods, w, layer, out_dtype=F32):
    bsz, lt, _ = xs.shape
    n = w.shape[1]
    return pl.pallas_call(
        _proj_kernel,
        grid=(bsz, lt // TM),
        in_specs=[pl.BlockSpec((1, TM, D_MODEL), lambda b, r: (b, r, 0)),
                  _mod_spec(layer),
                  pl.BlockSpec((D_MODEL, n), lambda b, r: (0, 0))],
        out_specs=pl.BlockSpec((1, TM, n), lambda b, r: (b, r, 0)),
        out_shape=jax.ShapeDtypeStruct((bsz, lt, n), out_dtype),
        compiler_params=_params(2),
        name="mod_proj",
    )(xs, mods, w)


def _out_kernel(o_ref, x_ref, mod_ref, w_ref, g_ref, b_ref, y_ref):
    gate = mod_ref[0, 2:3, :]
    y = jnp.dot(o_ref[0], w_ref[...], preferred_element_type=F32)
    y_ref[0] = _layer_norm(DEEPNORM_ALPHA * x_ref[0] + gate * y, g_ref[...], b_ref[...])


def _out_proj(o, xs, mods, w, ln_g, ln_b, layer):
    bsz, lt, k = o.shape
    return pl.pallas_call(
        _out_kernel,
        grid=(bsz, lt // TM),
        in_specs=[pl.BlockSpec((1, TM, k), lambda b, r: (b, r, 0)),
                  pl.BlockSpec((1, TM, D_MODEL), lambda b, r: (b, r, 0)),
                  _mod_spec(layer),
                  pl.BlockSpec((k, D_MODEL), lambda b, r: (0, 0)),
                  pl.BlockSpec((1, D_MODEL), lambda b, r: (0, 0)),
                  pl.BlockSpec((1, D_MODEL), lambda b, r: (0, 0))],
        out_specs=pl.BlockSpec((1, TM, D_MODEL), lambda b, r: (b, r, 0)),
        out_shape=jax.ShapeDtypeStruct((bsz, lt, D_MODEL), F32),
        compiler_params=_params(2),
        name="out_proj_ln",
    )(o, xs, mods, w, ln_g.reshape(1, D_MODEL), ln_b.reshape(1, D_MODEL))


FF_CHUNK = 1024


def _mlp_kernel(x_ref, mod_ref, w1_ref, w2_ref, g_ref, b_ref, y_ref):
    sh = mod_ref[0, 3:4, :]
    sc = mod_ref[0, 4:5, :]
    gate = mod_ref[0, 5:6, :]
    x = x_ref[0]
    h = (x * (1.0 + sc) + sh).astype(BF16)
    acc = jnp.zeros((TM, D_MODEL), F32)
    for j in range(D_FF // FF_CHUNK):
        a = jnp.dot(h, w1_ref[:, j * FF_CHUNK:(j + 1) * FF_CHUNK], preferred_element_type=F32)
        a = jnp.square(jnp.maximum(a, 0.0)).astype(BF16)
        acc = acc + jnp.dot(a, w2_ref[j * FF_CHUNK:(j + 1) * FF_CHUNK, :], preferred_element_type=F32)
    y_ref[0] = _layer_norm(DEEPNORM_ALPHA * x + gate * acc, g_ref[...], b_ref[...])


def _mlp(xs, mods, w1, w2, ln_g, ln_b, layer):
    bsz, lt, _ = xs.shape
    return pl.pallas_call(
        _mlp_kernel,
        grid=(bsz, lt // TM),
        in_specs=[pl.BlockSpec((1, TM, D_MODEL), lambda b, r: (b, r, 0)),
                  _mod_spec(layer),
                  pl.BlockSpec((D_MODEL, D_FF), lambda b, r: (0, 0)),
                  pl.BlockSpec((D_FF, D_MODEL), lambda b, r: (0, 0)),
                  pl.BlockSpec((1, D_MODEL), lambda b, r: (0, 0)),
                  pl.BlockSpec((1, D_MODEL), lambda b, r: (0, 0))],
        out_specs=pl.BlockSpec((1, TM, D_MODEL), lambda b, r: (b, r, 0)),
        out_shape=jax.ShapeDtypeStruct((bsz, lt, D_MODEL), F32),
        compiler_params=_params(2),
        name="mlp_ln",
    )(xs, mods, w1, w2, ln_g.reshape(1, D_MODEL), ln_b.reshape(1, D_MODEL))


def _rope_tables(n_latent, head_dim):
    rows = n_latent // GRID_W
    row = jnp.repeat(jnp.arange(rows), GRID_W).astype(F32)
    col = jnp.tile(jnp.arange(GRID_W), rows).astype(F32)
    n_freq = head_dim // 4
    inv_freq = ROPE_THETA ** (-jnp.arange(n_freq, dtype=F32) / n_freq)
    ang = jnp.concatenate([row[:, None] * inv_freq, col[:, None] * inv_freq], -1)
    cos = jnp.repeat(jnp.cos(ang), 2, axis=-1)
    sin = jnp.repeat(jnp.sin(ang), 2, axis=-1)
    sign = jnp.tile(jnp.array([-1.0, 1.0], F32), head_dim // 2)
    sin = sin * sign
    reps = LANE // head_dim
    cos = jnp.tile(cos, (1, reps))
    sin = jnp.tile(sin, (1, reps))
    cos = jnp.concatenate([jnp.ones((CTX_LEN, LANE), F32), cos], 0)
    sin = jnp.concatenate([jnp.zeros((CTX_LEN, LANE), F32), sin], 0)
    return cos, sin


def _rope(x, cos, sin_signed):
    lane = lax.broadcasted_iota(jnp.int32, x.shape, 1)
    nxt = pltpu.roll(x, LANE - 1, 1)
    prv = pltpu.roll(x, 1, 1)
    swapped = jnp.where(lane % 2 == 0, nxt, prv)
    return x * cos + swapped * sin_signed


def _rms(x, g):
    return x * lax.rsqrt(jnp.mean(x * x, -1, keepdims=True) + 1e-6) * g


def _attn_kernel(q_ref, k_ref, v_ref, cos_ref, sin_ref, qg_ref, kg_ref, lam_ref, ng_ref, o_ref,
                 kb_ref, vb_ref, *, n_maps, qk_norm, scale, lambda_init):
    g = pl.program_id(2)
    qt = pl.program_id(3)
    lt = kb_ref.shape[0]

    @pl.when((g == 0) & (qt == 0))
    def _prep_kv():
        k = k_ref[0]
        if qk_norm:
            k = _rms(k, kg_ref[...])
        kb_ref[...] = _rope(k, cos_ref[...], sin_ref[...]).astype(BF16)
        vb_ref[...] = v_ref[0].astype(BF16)

    q = q_ref[0]
    if qk_norm:
        q = _rms(q, qg_ref[...])
    row0 = pl.multiple_of(qt * TM, TM)
    q = _rope(q, cos_ref[pl.ds(row0, TM), :], sin_ref[pl.ds(row0, TM), :]) * scale

    def scores(qm, nk):
        return lax.dot_general(qm.astype(BF16), kb_ref[0:nk, :], (((1,), (1,)), ((), ())),
                               preferred_element_type=F32)

    def softmax_parts(s):
        m = jnp.max(s, -1, keepdims=True)
        e = jnp.exp(s - m)
        return e, jnp.sum(e, -1, keepdims=True)

    def attend(nk):
        vb = vb_ref[0:nk, :]
        if n_maps == 1:
            e, l = softmax_parts(scores(q, nk))
            o = jnp.dot(e.astype(BF16), vb, preferred_element_type=F32) * (1.0 / l)
        else:
            lp = lam_ref[...]
            lam = (jnp.exp(jnp.sum(lp[0:1, :] * lp[1:2, :])) - jnp.exp(jnp.sum(lp[2:3, :] * lp[3:4, :]))
                   + lambda_init)
            lane = lax.broadcasted_iota(jnp.int32, q.shape, 1)
            e0, l0 = softmax_parts(scores(jnp.where(lane < LANE // 2, q, 0.0), nk))
            e1, l1 = softmax_parts(scores(jnp.where(lane >= LANE // 2, q, 0.0), nk))
            a = e0 * (1.0 / l0) - e1 * (lam / l1)
            o = jnp.dot(a.astype(BF16), vb, preferred_element_type=F32)
            o = _rms(o, ng_ref[...]) * (1.0 - lambda_init)
        o_ref[0] = o.astype(o_ref.dtype)

    @pl.when(qt == 0)
    def _ctx():
        attend(CTX_LEN)

    @pl.when(qt != 0)
    def _latent():
        attend(lt)


def _attention(p, cos, sin, q_gain, k_gain, lam_p, norm_g, *, n_kv, group, q_col, k_col, v_col,
               n_maps, qk_norm, scale, lambda_init):
    bsz, lt, _ = p.shape
    kernel = functools.partial(_attn_kernel, n_maps=n_maps, qk_norm=qk_norm, scale=scale,
                               lambda_init=lambda_init)
    const = lambda b, kv, g, qt: (0, 0)
    return pl.pallas_call(
        kernel,
        grid=(bsz, n_kv, group, lt // TM),
        in_specs=[pl.BlockSpec((1, TM, LANE), lambda b, kv, g, qt: (b, qt, q_col + kv * group + g)),
                  pl.BlockSpec((1, lt, LANE), lambda b, kv, g, qt: (b, 0, k_col + kv)),
                  pl.BlockSpec((1, lt, LANE), lambda b, kv, g, qt: (b, 0, v_col + kv)),
                  pl.BlockSpec((lt, LANE), const),
                  pl.BlockSpec((lt, LANE), const),
                  pl.BlockSpec((1, LANE), const),
                  pl.BlockSpec((1, LANE), const),
                  pl.BlockSpec(lam_p.shape, const),
                  pl.BlockSpec((1, LANE), const)],
        out_specs=pl.BlockSpec((1, TM, LANE), lambda b, kv, g, qt: (b, qt, kv * group + g)),
        out_shape=jax.ShapeDtypeStruct((bsz, lt, n_kv * group * LANE), BF16),
        scratch_shapes=[pltpu.VMEM((lt, LANE), BF16), pltpu.VMEM((lt, LANE), BF16)],
        compiler_params=_params(4),
        name="attention",
    )(p, p, p, cos, sin, q_gain.reshape(1, LANE), k_gain.reshape(1, LANE), lam_p, norm_g.reshape(1, LANE))


def _s5_out_kernel(y_ref, x_ref, mod_ref, d_ref, w_ref, g_ref, b_ref, o_ref):
    sh = mod_ref[0, 0:1, :]
    sc = mod_ref[0, 1:2, :]
    gate = mod_ref[0, 2:3, :]
    x = x_ref[0]
    u = x * (1.0 + sc) + sh
    y = jax.nn.gelu(y_ref[0] + d_ref[...] * u).astype(BF16)
    z = jnp.dot(y, w_ref[...], preferred_element_type=F32)
    out = z[:, :D_MODEL] * jax.nn.sigmoid(z[:, D_MODEL:])
    o_ref[0] = _layer_norm(DEEPNORM_ALPHA * x + gate * out, g_ref[...], b_ref[...])


def _s5_out(y, xs, mods, d_skip, w_glu, ln_g, ln_b, layer):
    bsz, lt, _ = xs.shape
    return pl.pallas_call(
        _s5_out_kernel,
        grid=(bsz, lt // TM),
        in_specs=[pl.BlockSpec((1, TM, D_MODEL), lambda b, r: (b, r, 0)),
                  pl.BlockSpec((1, TM, D_MODEL), lambda b, r: (b, r, 0)),
                  _mod_spec(layer),
                  pl.BlockSpec((1, D_MODEL), lambda b, r: (0, 0)),
                  pl.BlockSpec((D_MODEL, 2 * D_MODEL), lambda b, r: (0, 0)),
                  pl.BlockSpec((1, D_MODEL), lambda b, r: (0, 0)),
                  pl.BlockSpec((1, D_MODEL), lambda b, r: (0, 0))],
        out_specs=pl.BlockSpec((1, TM, D_MODEL), lambda b, r: (b, r, 0)),
        out_shape=jax.ShapeDtypeStruct((bsz, lt, D_MODEL), F32),
        compiler_params=_params(2),
        name="s5_out_ln",
    )(y, xs, mods, d_skip.reshape(1, D_MODEL), w_glu, ln_g.reshape(1, D_MODEL), ln_b.reshape(1, D_MODEL))


def _l2_norm(x, eps=1e-6):
    return x * lax.rsqrt(jnp.sum(x * x, -1, keepdims=True) + eps)


def _conv_tmp(x, w):
    k, ch = w.shape
    return lax.conv_general_dilated(x, w[:, None, :], window_strides=(1,), padding=[(k // 2, k // 2)],
                                    dimension_numbers=('NWC', 'WIO', 'NWC'), feature_group_count=ch)


def _gdc_tmp(q, k, v, log_decay, beta, s0):
    b, n, h, dk = q.shape
    dv = v.shape[-1]
    nc = n // DN_CHUNK

    def to_chunks(t):
        t = t.reshape(b, nc, DN_CHUNK, h, *t.shape[3:])
        return jnp.moveaxis(jnp.moveaxis(t, 1, 0), 3, 2)

    q, k, v, g, bt = (to_chunks(t) for t in (q, k, v, log_decay, beta))
    gc = jnp.cumsum(g, -1)
    idx = jnp.arange(DN_CHUNK)
    causal = idx[:, None] >= idx[None, :]
    strict = idx[:, None] > idx[None, :]
    decay = jnp.exp(jnp.where(causal, gc[..., :, None] - gc[..., None, :], -jnp.inf))
    a = jnp.where(strict, bt[..., :, None] * jnp.einsum('nbhid,nbhjd->nbhij', k, k) * decay, 0.0)
    eye = jnp.eye(DN_CHUNK, dtype=F32)
    t_inv = lax.linalg.triangular_solve(eye + a, jnp.broadcast_to(eye, a.shape), left_side=True,
                                        lower=True, unit_diagonal=True)
    u = t_inv @ (bt[..., None] * v)
    w = t_inv @ (bt[..., None] * jnp.exp(gc)[..., None] * k)
    qk = jnp.einsum('nbhid,nbhjd->nbhij', q, k) * decay
    q_dec = q * jnp.exp(gc)[..., None]
    k_dec = k * jnp.exp(gc[..., -1:] - gc)[..., None]
    g_last = jnp.exp(gc[..., -1])[..., None, None]

    def step(s, inp):
        u_c, w_c, qk_c, qd_c, kd_c, gl_c = inp
        v_new = u_c - w_c @ s
        o = qd_c @ s + qk_c @ v_new
        s = gl_c * s + jnp.swapaxes(kd_c, -1, -2) @ v_new
        return s, o

    s_fin, o = lax.scan(step, s0, (u, w, qk, q_dec, k_dec, g_last))
    o = jnp.moveaxis(jnp.moveaxis(o, 2, 3), 0, 1).reshape(b, n, h, dv)
    return o, s_fin


def _deltanet_core_tmp(p, conv_w, a_log, dt_bias, norm_g):
    def project(pp):
        b, n, _ = pp.shape
        qkv = jax.nn.silu(_conv_tmp(pp[..., :3 * DN_WIDTH], conv_w))
        q, k, v = jnp.split(qkv, 3, -1)
        q = _l2_norm(q.reshape(b, n, DN_HEADS, DN_HEAD_DIM)) * DN_HEAD_DIM ** -0.5
        k = _l2_norm(k.reshape(b, n, DN_HEADS, DN_HEAD_DIM))
        v = v.reshape(b, n, DN_HEADS, DN_HEAD_DIM)
        z = pp[..., 3 * DN_WIDTH:4 * DN_WIDTH]
        gates = pp[..., 4 * DN_WIDTH:4 * DN_WIDTH + 4 * DN_HEADS].reshape(b, n, 2, 2, DN_HEADS)
        beta = jax.nn.sigmoid(gates[:, :, 0])
        log_decay = -jnp.exp(a_log) * jax.nn.softplus(gates[:, :, 1] + dt_bias)
        return q, k, v, z, beta, log_decay

    rev = lambda t: jnp.flip(t, 1)
    qc, kc, vc, zc, bc, gc = project(p[:, :CTX_LEN])
    qx, kx, vx, zx, bx, gx = project(p[:, CTX_LEN:])
    s0 = jnp.zeros((p.shape[0], DN_HEADS, DN_HEAD_DIM, DN_HEAD_DIM), F32)
    oc_f, sc_f = _gdc_tmp(qc, kc, vc, gc[:, :, 0], bc[:, :, 0], s0)
    ox_f, _ = _gdc_tmp(qx, kx, vx, gx[:, :, 0], bx[:, :, 0], sc_f)
    oc_b, sc_b = _gdc_tmp(rev(qc), rev(kc), rev(vc), rev(gc[:, :, 1]), rev(bc[:, :, 1]), s0)
    ox_b, _ = _gdc_tmp(rev(qx), rev(kx), rev(vx), rev(gx[:, :, 1]), rev(bx[:, :, 1]), sc_b)

    def out(o, z):
        b, n = z.shape[:2]
        o = o * lax.rsqrt(jnp.mean(o * o, -1, keepdims=True) + 1e-6) * norm_g
        return (o.reshape(b, n, DN_WIDTH) * jax.nn.silu(z)).astype(BF16)

    return jnp.concatenate([out(oc_f + rev(oc_b), zc), out(ox_f + rev(ox_b), zx)], 1)


def _s5_core_tmp(h, a_re, a_im, log_dt, b_re, b_im, c_re, c_im):
    lam = lax.complex(jnp.minimum(a_re, -1e-4), a_im)
    lam_dt = lam * jnp.exp(log_dt)[..., None]
    lam_bar = jnp.exp(lam_dt)
    b_bar = ((lam_bar - 1.0) / lam)[..., None] * lax.complex(b_re, b_im)
    c_mat = lax.complex(c_re, c_im)

    def combine(e1, e2):
        a1, b1 = e1
        a2, b2 = e2
        return a1 * a2, a2 * b1 + b2

    def scan(u, dd, h0, reverse):
        n = u.shape[1]
        bu = jnp.einsum('gpi,blgi->blgp', b_bar[dd], u.astype(jnp.complex64))
        if h0 is not None:
            pos = -1 if reverse else 0
            bu = bu.at[:, pos].add(lam_bar[dd] * h0)
        a = jnp.broadcast_to(lam_bar[dd], (1, n) + lam_bar.shape[1:])
        _, st = lax.associative_scan(combine, (a, bu), axis=1, reverse=reverse)
        return st

    def readout(st, dd):
        return jnp.real(jnp.einsum('gip,blgp->blgi', c_mat[dd], st))

    bsz = h.shape[0]
    uc = h[:, :CTX_LEN].reshape(bsz, CTX_LEN, SS_GROUPS, SS_GROUP)
    ux = h[:, CTX_LEN:].reshape(bsz, -1, SS_GROUPS, SS_GROUP)
    st_cf = scan(uc, 0, None, False)
    st_cb = scan(uc, 1, None, True)
    y_x = readout(scan(ux, 0, st_cf[:, -1], False), 0) + readout(scan(ux, 1, st_cb[:, 0], True), 1)
    y_c = readout(st_cf, 0) + readout(st_cb, 1)
    return jnp.concatenate([y_c, y_x], 1).reshape(bsz, -1, D_MODEL)


def kernel(x, c, ctx, c_ctx, ada_w, ada_b, ln_g, ln_b, mlp_w1, mlp_w2, dn_w_in, dn_conv, dn_a_log, dn_dt_bias, dn_norm_g, dn_w_out, da_w_qkv, da_lambda, da_norm_g, da_w_out, ga_w_qkv, ga_q_norm, ga_k_norm, ga_w_out, ss_a_re, ss_a_im, ss_log_dt, ss_b_re, ss_b_im, ss_c_re, ss_c_im, ss_d, ss_w_glu):
    bsz, n_latent, _ = x.shape
    xs = jnp.concatenate([ctx, x], 1)
    c_rows = jnp.concatenate([c, c_ctx[None, :], jnp.zeros((MOD_ROWS - bsz - 1, D_MODEL), F32)], 0)
    mods = _ada_all(c_rows, ada_w, ada_b).reshape(DEPTH * MOD_ROWS, 6, D_MODEL)
    ones = jnp.ones((LANE,), F32)
    zeros4 = jnp.zeros((4, DA_HEAD_DIM), F32)

    for i in range(DEPTH):
        m, j = i % 4, i // 4
        if m == 0:
            w_in = jnp.pad(dn_w_in[j], ((0, 0), (0, LANE - 4 * DN_HEADS))).astype(BF16)
            p = _proj(xs, mods, w_in, i)
            o = _deltanet_core_tmp(p, dn_conv[j], dn_a_log[j], dn_dt_bias[j], dn_norm_g[j])
            xs = _out_proj(o, xs, mods, dn_w_out[j].astype(BF16), ln_g[i, 0], ln_b[i, 0], i)
        elif m == 1:
            lambda_init = 0.8 - 0.6 * math.exp(-0.3 * i)
            p = _proj(xs, mods, da_w_qkv[j].astype(BF16), i)
            cos, sin = _rope_tables(n_latent, DA_HEAD_DIM)
            o = _attention(p, cos, sin, ones, ones, da_lambda[j], da_norm_g[j],
                           n_kv=DA_HEADS, group=1, q_col=0, k_col=DA_HEADS, v_col=2 * DA_HEADS,
                           n_maps=2, qk_norm=False, scale=DA_HEAD_DIM ** -0.5, lambda_init=lambda_init)
            xs = _out_proj(o, xs, mods, da_w_out[j].astype(BF16), ln_g[i, 0], ln_b[i, 0], i)
        elif m == 2:
            p = _proj(xs, mods, ga_w_qkv[j].astype(BF16), i)
            cos, sin = _rope_tables(n_latent, GA_HEAD_DIM)
            o = _attention(p, cos, sin, ga_q_norm[j], ga_k_norm[j], zeros4, ones,
                           n_kv=GA_KV_HEADS, group=GA_HEADS // GA_KV_HEADS, q_col=0, k_col=GA_HEADS,
                           v_col=GA_HEADS + GA_KV_HEADS, n_maps=1, qk_norm=True,
                           scale=GA_HEAD_DIM ** -0.5, lambda_init=0.0)
            xs = _out_proj(o, xs, mods, ga_w_out[j].astype(BF16), ln_g[i, 0], ln_b[i, 0], i)
        else:
            sh = jnp.concatenate([jnp.broadcast_to(mods[i * MOD_ROWS + 8, 0], (bsz, CTX_LEN, D_MODEL)),
                                  jnp.broadcast_to(mods[i * MOD_ROWS:i * MOD_ROWS + bsz, 0][:, None], (bsz, n_latent, D_MODEL))], 1)
            sc = jnp.concatenate([jnp.broadcast_to(mods[i * MOD_ROWS + 8, 1], (bsz, CTX_LEN, D_MODEL)),
                                  jnp.broadcast_to(mods[i * MOD_ROWS:i * MOD_ROWS + bsz, 1][:, None], (bsz, n_latent, D_MODEL))], 1)
            h = xs * (1.0 + sc) + sh
            y = _s5_core_tmp(h, ss_a_re[j], ss_a_im[j], ss_log_dt[j], ss_b_re[j], ss_b_im[j],
                             ss_c_re[j], ss_c_im[j])
            xs = _s5_out(y, xs, mods, ss_d[j], ss_w_glu[j].astype(BF16), ln_g[i, 0], ln_b[i, 0], i)
        xs = _mlp(xs, mods, mlp_w1[i].astype(BF16), mlp_w2[i].astype(BF16), ln_g[i, 1], ln_b[i, 1], i)
    return xs[:, CTX_LEN:]
```

```python
import functools
import math

import jax
import jax.numpy as jnp
from jax import lax
from jax.experimental import pallas as pl
from jax.experimental.pallas import tpu as pltpu

F32 = jnp.float32
BF16 = jnp.bfloat16

D_MODEL = 1024
D_FF = 4 * D_MODEL
DEPTH = 4
GRID_W = 64
CTX_LEN = 256
ROPE_THETA = 10000.0
DEEPNORM_ALPHA = (2 * DEPTH) ** 0.25
TM = 256
LANE = 128
MOD_ROWS = 16
VMEM_LIMIT = 56 * 1024 * 1024

DN_HEADS = 8
DN_HEAD_DIM = 128
DN_WIDTH = DN_HEADS * DN_HEAD_DIM
DN_CONV = 5
DN_CHUNK = 64
DA_HEADS = 8
DA_HEAD_DIM = 64
GA_HEADS = 8
GA_KV_HEADS = 2
GA_HEAD_DIM = 128
SS_GROUP = 16
SS_GROUPS = D_MODEL // SS_GROUP
SS_STATE = 64


def _params(n_axes):
    return pltpu.CompilerParams(dimension_semantics=("arbitrary",) * n_axes,
                                vmem_limit_bytes=VMEM_LIMIT)


def _layer_norm(v, g, b):
    mu = jnp.mean(v, -1, keepdims=True)
    d = v - mu
    var = jnp.mean(d * d, -1, keepdims=True)
    return d * lax.rsqrt(var + 1e-5) * g + b


def _mod_spec(layer):
    return pl.BlockSpec((1, 6, D_MODEL),
                        lambda b, r: (layer * MOD_ROWS + jnp.where(r == 0, 8, b), 0, 0))


def _ada_kernel(c_ref, w_ref, b_ref, o_ref):
    c = c_ref[...]
    act = (c * jax.nn.sigmoid(c)).astype(BF16)
    o_ref[0] = jnp.dot(act, w_ref[0].astype(BF16), preferred_element_type=F32) + b_ref[0]


def _ada_all(c_rows, ada_w, ada_b):
    tn = 1536
    n = 6 * D_MODEL
    return pl.pallas_call(
        _ada_kernel,
        grid=(DEPTH, n // tn),
        in_specs=[pl.BlockSpec((MOD_ROWS, D_MODEL), lambda i, j: (0, 0)),
                  pl.BlockSpec((1, D_MODEL, tn), lambda i, j: (i, 0, j)),
                  pl.BlockSpec((1, 1, tn), lambda i, j: (i, 0, j))],
        out_specs=pl.BlockSpec((1, MOD_ROWS, tn), lambda i, j: (i, 0, j)),
        out_shape=jax.ShapeDtypeStruct((DEPTH, MOD_ROWS, n), F32),
        compiler_params=_params(2),
        name="ada_mod",
    )(c_rows, ada_w, ada_b.reshape(DEPTH, 1, n))


def _proj_kernel(x_ref, mod_ref, w_ref, o_ref):
    sh = mod_ref[0, 0:1, :]
    sc = mod_ref[0, 1:2, :]
    h = (x_ref[0] * (1.0 + sc) + sh).astype(BF16)
    o_ref[0] = jnp.dot(h, w_ref[...], preferred_element_type=F32).astype(o_ref.dtype)


def _proj(xs, mods, w, layer, out_dtype=F32):
    bsz, lt, _ = xs.shape
    n = w.shape[1]
    return pl.pallas_call(
        _proj_kernel,
        grid=(bsz, lt // TM),
        in_specs=[pl.BlockSpec((1, TM, D_MODEL), lambda b, r: (b, r, 0)),
                  _mod_spec(layer),
                  pl.BlockSpec((D_MODEL, n), lambda b, r: (0, 0))],
        out_specs=pl.BlockSpec((1, TM, n), lambda b, r: (b, r, 0)),
        out_shape=jax.ShapeDtypeStruct((bsz, lt, n), out_dtype),
        compiler_params=_params(2),
        name="mod_proj",
    )(xs, mods, w)


def _out_kernel(o_ref, x_ref, mod_ref, w_ref, g_ref, b_ref, y_ref):
    gate = mod_ref[0, 2:3, :]
    y = jnp.dot(o_ref[0], w_ref[...], preferred_element_type=F32)
    y_ref[0] = _layer_norm(DEEPNORM_ALPHA * x_ref[0] + gate * y, g_ref[...], b_ref[...])


def _out_proj(o, xs, mods, w, ln_g, ln_b, layer):
    bsz, lt, k = o.shape
    return pl.pallas_call(
        _out_kernel,
        grid=(bsz, lt // TM),
        in_specs=[pl.BlockSpec((1, TM, k), lambda b, r: (b, r, 0)),
                  pl.BlockSpec((1, TM, D_MODEL), lambda b, r: (b, r, 0)),
                  _mod_spec(layer),
                  pl.BlockSpec((k, D_MODEL), lambda b, r: (0, 0)),
                  pl.BlockSpec((1, D_MODEL), lambda b, r: (0, 0)),
                  pl.BlockSpec((1, D_MODEL), lambda b, r: (0, 0))],
        out_specs=pl.BlockSpec((1, TM, D_MODEL), lambda b, r: (b, r, 0)),
        out_shape=jax.ShapeDtypeStruct((bsz, lt, D_MODEL), F32),
        compiler_params=_params(2),
        name="out_proj_ln",
    )(o, xs, mods, w, ln_g.reshape(1, D_MODEL), ln_b.reshape(1, D_MODEL))


FF_CHUNK = 1024


def _mlp_kernel(x_ref, mod_ref, w1_ref, w2_ref, g_ref, b_ref, y_ref):
    sh = mod_ref[0, 3:4, :]
    sc = mod_ref[0, 4:5, :]
    gate = mod_ref[0, 5:6, :]
    x = x_ref[0]
    h = (x * (1.0 + sc) + sh).astype(BF16)
    acc = jnp.zeros((TM, D_MODEL), F32)
    for j in range(D_FF // FF_CHUNK):
        a = jnp.dot(h, w1_ref[:, j * FF_CHUNK:(j + 1) * FF_CHUNK], preferred_element_type=F32)
        a = jnp.square(jnp.maximum(a, 0.0)).astype(BF16)
        acc = acc + jnp.dot(a, w2_ref[j * FF_CHUNK:(j + 1) * FF_CHUNK, :], preferred_element_type=F32)
    y_ref[0] = _layer_norm(DEEPNORM_ALPHA * x + gate * acc, g_ref[...], b_ref[...])


def _mlp(xs, mods, w1, w2, ln_g, ln_b, layer):
    bsz, lt, _ = xs.shape
    return pl.pallas_call(
        _mlp_kernel,
        grid=(bsz, lt // TM),
        in_specs=[pl.BlockSpec((1, TM, D_MODEL), lambda b, r: (b, r, 0)),
                  _mod_spec(layer),
                  pl.BlockSpec((D_MODEL, D_FF), lambda b, r: (0, 0)),
                  pl.BlockSpec((D_FF, D_MODEL), lambda b, r: (0, 0)),
                  pl.BlockSpec((1, D_MODEL), lambda b, r: (0, 0)),
                  pl.BlockSpec((1, D_MODEL), lambda b, r: (0, 0))],
        out_specs=pl.BlockSpec((1, TM, D_MODEL), lambda b, r: (b, r, 0)),
        out_shape=jax.ShapeDtypeStruct((bsz, lt, D_MODEL), F32),
        compiler_params=_params(2),
        name="mlp_ln",
    )(xs, mods, w1, w2, ln_g.reshape(1, D_MODEL), ln_b.reshape(1, D_MODEL))


def _rope_tables(n_latent, head_dim):
    rows = n_latent // GRID_W
    row = jnp.repeat(jnp.arange(rows), GRID_W).astype(F32)
    col = jnp.tile(jnp.arange(GRID_W), rows).astype(F32)
    n_freq = head_dim // 4
    inv_freq = ROPE_THETA ** (-jnp.arange(n_freq, dtype=F32) / n_freq)
    ang = jnp.concatenate([row[:, None] * inv_freq, col[:, None] * inv_freq], -1)
    cos = jnp.repeat(jnp.cos(ang), 2, axis=-1)
    sin = jnp.repeat(jnp.sin(ang), 2, axis=-1)
    sign = jnp.tile(jnp.array([-1.0, 1.0], F32), head_dim // 2)
    sin = sin * sign
    reps = LANE // head_dim
    cos = jnp.tile(cos, (1, reps))
    sin = jnp.tile(sin, (1, reps))
    cos = jnp.concatenate([jnp.ones((CTX_LEN, LANE), F32), cos], 0)
    sin = jnp.concatenate([jnp.zeros((CTX_LEN, LANE), F32), sin], 0)
    return cos, sin


def _rope(x, cos, sin_signed):
    lane = lax.broadcasted_iota(jnp.int32, x.shape, 1)
    nxt = pltpu.roll(x, LANE - 1, 1)
    prv = pltpu.roll(x, 1, 1)
    swapped = jnp.where(lane % 2 == 0, nxt, prv)
    return x * cos + swapped * sin_signed


def _rms(x, g):
    return x * lax.rsqrt(jnp.mean(x * x, -1, keepdims=True) + 1e-6) * g


def _attn_kernel(q_ref, k_ref, v_ref, cos_ref, sin_ref, qg_ref, kg_ref, lam_ref, ng_ref, o_ref,
                 kb_ref, vb_ref, *, n_maps, qk_norm, scale, lambda_init):
    g = pl.program_id(2)
    qt = pl.program_id(3)
    lt = kb_ref.shape[0]

    @pl.when((g == 0) & (qt == 0))
    def _prep_kv():
        k = k_ref[0]
        if qk_norm:
            k = _rms(k, kg_ref[...])
        kb_ref[...] = _rope(k, cos_ref[...], sin_ref[...]).astype(BF16)
        vb_ref[...] = v_ref[0].astype(BF16)

    q = q_ref[0]
    if qk_norm:
        q = _rms(q, qg_ref[...])
    row0 = pl.multiple_of(qt * TM, TM)
    q = _rope(q, cos_ref[pl.ds(row0, TM), :], sin_ref[pl.ds(row0, TM), :]) * scale

    def scores(qm, nk):
        return lax.dot_general(qm.astype(BF16), kb_ref[0:nk, :], (((1,), (1,)), ((), ())),
                               preferred_element_type=F32)

    def softmax_parts(s):
        m = jnp.max(s, -1, keepdims=True)
        e = jnp.exp(s - m)
        return e, jnp.sum(e, -1, keepdims=True)

    def attend(nk):
        vb = vb_ref[0:nk, :]
        if n_maps == 1:
            e, l = softmax_parts(scores(q, nk))
            o = jnp.dot(e.astype(BF16), vb, preferred_element_type=F32) * (1.0 / l)
        else:
            lp = lam_ref[...]
            lam = (jnp.exp(jnp.sum(lp[0:1, :] * lp[1:2, :])) - jnp.exp(jnp.sum(lp[2:3, :] * lp[3:4, :]))
                   + lambda_init)
            lane = lax.broadcasted_iota(jnp.int32, q.shape, 1)
            e0, l0 = softmax_parts(scores(jnp.where(lane < LANE // 2, q, 0.0), nk))
            e1, l1 = softmax_parts(scores(jnp.where(lane >= LANE // 2, q, 0.0), nk))
            a = e0 * (1.0 / l0) - e1 * (lam / l1)
            o = jnp.dot(a.astype(BF16), vb, preferred_element_type=F32)
            o = _rms(o, ng_ref[...]) * (1.0 - lambda_init)
        o_ref[0] = o.astype(o_ref.dtype)

    @pl.when(qt == 0)
    def _ctx():
        attend(CTX_LEN)

    @pl.when(qt != 0)
    def _latent():
        attend(lt)


def _attention(p, cos, sin, q_gain, k_gain, lam_p, norm_g, *, n_kv, group, q_col, k_col, v_col,
               n_maps, qk_norm, scale, lambda_init):
    bsz, lt, _ = p.shape
    kernel = functools.partial(_attn_kernel, n_maps=n_maps, qk_norm=qk_norm, scale=scale,
                               lambda_init=lambda_init)
    const = lambda b, kv, g, qt: (0, 0)
    return pl.pallas_call(
        kernel,
        grid=(bsz, n_kv, group, lt // TM),
        in_specs=[pl.BlockSpec((1, TM, LANE), lambda b, kv, g, qt: (b, qt, q_col + kv * group + g)),
                  pl.BlockSpec((1, lt, LANE), lambda b, kv, g, qt: (b, 0, k_col + kv)),
                  pl.BlockSpec((1, lt, LANE), lambda b, kv, g, qt: (b, 0, v_col + kv)),
                  pl.BlockSpec((lt, LANE), const),
                  pl.BlockSpec((lt, LANE), const),
                  pl.BlockSpec((1, LANE), const),
                  pl.BlockSpec((1, LANE), const),
                  pl.BlockSpec(lam_p.shape, const),
                  pl.BlockSpec((1, LANE), const)],
        out_specs=pl.BlockSpec((1, TM, LANE), lambda b, kv, g, qt: (b, qt, kv * group + g)),
        out_shape=jax.ShapeDtypeStruct((bsz, lt, n_kv * group * LANE), BF16),
        scratch_shapes=[pltpu.VMEM((lt, LANE), BF16), pltpu.VMEM((lt, LANE), BF16)],
        compiler_params=_params(4),
        name="attention",
    )(p, p, p, cos, sin, q_gain.reshape(1, LANE), k_gain.reshape(1, LANE), lam_p, norm_g.reshape(1, LANE))


def _s5_out_kernel(y_ref, x_ref, mod_ref, d_ref, w_ref, g_ref, b_ref, o_ref):
    sh = mod_ref[0, 0:1, :]
    sc = mod_ref[0, 1:2, :]
    gate = mod_ref[0, 2:3, :]
    x = x_ref[0]
    u = x * (1.0 + sc) + sh
    y = jax.nn.gelu(y_ref[0] + d_ref[...] * u).astype(BF16)
    z = jnp.dot(y, w_ref[...], preferred_element_type=F32)
    out = z[:, :D_MODEL] * jax.nn.sigmoid(z[:, D_MODEL:])
    o_ref[0] = _layer_norm(DEEPNORM_ALPHA * x + gate * out, g_ref[...], b_ref[...])


def _s5_out(y, xs, mods, d_skip, w_glu, ln_g, ln_b, layer):
    bsz, lt, _ = xs.shape
    return pl.pallas_call(
        _s5_out_kernel,
        grid=(bsz, lt // TM),
        in_specs=[pl.BlockSpec((1, TM, D_MODEL), lambda b, r: (b, r, 0)),
                  pl.BlockSpec((1, TM, D_MODEL), lambda b, r: (b, r, 0)),
                  _mod_spec(layer),
                  pl.BlockSpec((1, D_MODEL), lambda b, r: (0, 0)),
                  pl.BlockSpec((D_MODEL, 2 * D_MODEL), lambda b, r: (0, 0)),
                  pl.BlockSpec((1, D_MODEL), lambda b, r: (0, 0)),
                  pl.BlockSpec((1, D_MODEL), lambda b, r: (0, 0))],
        out_specs=pl.BlockSpec((1, TM, D_MODEL), lambda b, r: (b, r, 0)),
        out_shape=jax.ShapeDtypeStruct((bsz, lt, D_MODEL), F32),
        compiler_params=_params(2),
        name="s5_out_ln",
    )(y, xs, mods, d_skip.reshape(1, D_MODEL), w_glu, ln_g.reshape(1, D_MODEL), ln_b.reshape(1, D_MODEL))


DN_BLK = 128
DN_GATE_BETA_F, DN_GATE_BETA_B, DN_GATE_A_F, DN_GATE_A_B = 0, DN_HEADS, 2 * DN_HEADS, 3 * DN_HEADS
_NT = (((1,), (1,)), ((), ()))
_TN = (((0,), (0,)), ((), ()))


def _dn_kernel(q_ref, k_ref, v_ref, z_ref, g_ref, cq_ref, ck_ref, cv_ref, alog_ref, dtb_ref, ng_ref, o_ref,
               beta_ref, gc_ref, tot_ref, gct_ref, qn_ref, kn_ref, vn_ref,
               wq_f, u_f, qk_f, kd_f, gl_f, o_f, wq_b, u_b, qk_b, kd_b, gl_b, o_b, *, n_blk, n_ctx_blk):
    h = pl.program_id(1)
    lt = n_blk * DN_BLK
    ctx_rows = n_ctx_blk * DN_BLK
    row = lax.broadcasted_iota(jnp.int32, (lt, 1), 0)
    lane = lax.broadcasted_iota(jnp.int32, (1, LANE), 1)

    @pl.when(h == 0)
    def _gates():
        gts = g_ref[0]
        beta_ref[...] = jax.nn.sigmoid(gts)
        g = -jnp.exp(alog_ref[...]) * jax.nn.softplus(gts + dtb_ref[...])
        pos = row % DN_BLK
        pre = g
        suf = g
        s = 1
        while s < DN_BLK:
            pre = pre + jnp.where(pos >= s, pltpu.roll(pre, s, 0), 0.0)
            suf = suf + jnp.where(pos < DN_BLK - s, pltpu.roll(suf, lt - s, 0), 0.0)
            s *= 2
        gc = jnp.where(lane >= DN_GATE_A_B, suf, pre)
        gc_ref[...] = gc
        tot_ref[...] = pre + suf - g
        for c in range(n_blk):
            gct_ref[c] = gc[c * DN_BLK:(c + 1) * DN_BLK, :].T

    def conv_silu(x_ref, w_ref):
        x = x_ref[0]
        w = w_ref[...]
        acc = x * w[DN_CONV // 2:DN_CONV // 2 + 1, :]
        for d in range(-(DN_CONV // 2), DN_CONV // 2 + 1):
            if d == 0:
                continue
            src = row + d
            ok = (src >= 0) & (src < lt) & ((src < ctx_rows) == (row < ctx_rows))
            acc = acc + jnp.where(ok, pltpu.roll(x, (-d) % lt, 0), 0.0) * w[d + DN_CONV // 2:d + DN_CONV // 2 + 1, :]
        return acc * jax.nn.sigmoid(acc)

    def l2n(x):
        return x * lax.rsqrt(jnp.sum(x * x, -1, keepdims=True) + 1e-6)

    qn_ref[...] = l2n(conv_silu(q_ref, cq_ref)) * DN_HEAD_DIM ** -0.5
    kn_ref[...] = l2n(conv_silu(k_ref, ck_ref))
    vn_ref[...] = conv_silu(v_ref, cv_ref)

    ii = lax.broadcasted_iota(jnp.int32, (DN_BLK, DN_BLK), 0)
    jj = lax.broadcasted_iota(jnp.int32, (DN_BLK, DN_BLK), 1)
    eye = (ii == jj).astype(F32)
    fwd = (DN_GATE_BETA_F, DN_GATE_A_F, ii >= jj, ii > jj, (wq_f, u_f, qk_f, kd_f, gl_f, o_f))
    bwd = (DN_GATE_BETA_B, DN_GATE_A_B, ii <= jj, ii < jj, (wq_b, u_b, qk_b, kd_b, gl_b, o_b))

    def pair_off(s):
        return ((ii // (2 * s)) == (jj // (2 * s))) & ((ii // s) != (jj // s))

    def column(ref, rows, lane_idx):
        return jnp.sum(jnp.where(lane == lane_idx, ref[rows, :], 0.0), -1, keepdims=True)

    def prepare(c, _):
        rows = pl.ds(pl.multiple_of(c * DN_BLK, DN_BLK), DN_BLK)
        qc, kc, vc = qn_ref[rows, :], kn_ref[rows, :], vn_ref[rows, :]
        kb = kc.astype(BF16)
        kk = lax.dot_general(kb, kb, _NT, preferred_element_type=F32)
        qk = lax.dot_general(qc.astype(BF16), kb, _NT, preferred_element_type=F32)
        for lane_beta, lane_g, incl, strict, (wq_ref, u_ref, qk_ref, kd_ref, gl_ref, _) in (fwd, bwd):
            bcol = column(beta_ref, rows, lane_beta + h)
            gcol = column(gc_ref, rows, lane_g + h)
            tcol = column(tot_ref, rows, lane_g + h)
            grow = gct_ref[c, pl.ds(lane_g + h, 1), :]
            dec = jnp.exp(jnp.where(incl, gcol - grow, -jnp.inf))
            a = jnp.where(strict, bcol * kk * dec, 0.0)
            t_inv = eye - jnp.where(pair_off(1), a, 0.0)
            s = 2
            while s < DN_BLK:
                tb = t_inv.astype(BF16)
                ta = jnp.dot(tb, jnp.where(pair_off(s), a, 0.0).astype(BF16), preferred_element_type=F32)
                t_inv = t_inv - jnp.dot(ta.astype(BF16), tb, preferred_element_type=F32)
                s *= 2
            egc = jnp.exp(gcol)
            rhs = jnp.concatenate([bcol * vc, (bcol * egc) * kc], -1).astype(BF16)
            uw = jnp.dot(t_inv.astype(BF16), rhs, preferred_element_type=F32)
            u_ref[c] = uw[:, :DN_HEAD_DIM]
            wq_ref[c, 0:DN_BLK, :] = uw[:, DN_HEAD_DIM:].astype(BF16)
            wq_ref[c, DN_BLK:2 * DN_BLK, :] = (qc * egc).astype(BF16)
            qk_ref[c] = (qk * dec).astype(BF16)
            kd_ref[c] = (kc * jnp.exp(tcol - gcol)).astype(BF16)
            gl_ref[c] = jnp.broadcast_to(jnp.exp(tcol[0:1, :]), (8, LANE))
        return 0

    lax.fori_loop(0, n_blk, prepare, 0)

    def advance(refs, c, s):
        wq_ref, u_ref, qk_ref, kd_ref, gl_ref, out_ref = refs
        r = jnp.dot(wq_ref[c], s.astype(BF16), preferred_element_type=F32)
        v_new = (u_ref[c] - r[:DN_BLK]).astype(BF16)
        out_ref[c] = r[DN_BLK:] + jnp.dot(qk_ref[c], v_new, preferred_element_type=F32)
        return gl_ref[c][0:1, :] * s + lax.dot_general(kd_ref[c], v_new, _TN, preferred_element_type=F32)

    def step(k, carry):
        s_f, s_b = carry
        cb = jnp.where(k < n_ctx_blk, n_ctx_blk - 1 - k, n_blk + n_ctx_blk - 1 - k)
        return advance(fwd[4], k, s_f), advance(bwd[4], cb, s_b)

    zero = jnp.zeros((DN_HEAD_DIM, DN_HEAD_DIM), F32)
    lax.fori_loop(0, n_blk, step, (zero, zero))

    o = (o_f[...] + o_b[...]).reshape(lt, DN_HEAD_DIM)
    o = o * lax.rsqrt(jnp.mean(o * o, -1, keepdims=True) + 1e-6) * ng_ref[...]
    z = z_ref[0]
    o_ref[0] = (o * (z * jax.nn.sigmoid(z))).astype(o_ref.dtype)


def _deltanet_core(p, conv_w, a_log, dt_bias, norm_g):
    bsz, lt, _ = p.shape
    n_blk = lt // DN_BLK
    gate_pad = lambda t: jnp.pad(t.reshape(1, 2 * DN_HEADS), ((0, 0), (DN_GATE_A_F, LANE - DN_GATE_A_F - 2 * DN_HEADS)))
    kernel = functools.partial(_dn_kernel, n_blk=n_blk, n_ctx_blk=CTX_LEN // DN_BLK)
    col = lambda off: pl.BlockSpec((1, lt, LANE), lambda b, h: (b, 0, off + h))
    cw = lambda off: pl.BlockSpec((DN_CONV, LANE), lambda b, h: (0, off + h))
    const = pl.BlockSpec((1, LANE), lambda b, h: (0, 0))
    seq = lambda dt: pltpu.VMEM((lt, LANE), dt)
    blk = lambda rows, dt: pltpu.VMEM((n_blk, rows, LANE), dt)
    per_dir = [blk(2 * DN_BLK, BF16), blk(DN_BLK, F32), blk(DN_BLK, BF16), blk(DN_BLK, BF16), blk(8, F32),
               blk(DN_BLK, F32)]
    return pl.pallas_call(
        kernel,
        grid=(bsz, DN_HEADS),
        in_specs=[col(0), col(DN_HEADS), col(2 * DN_HEADS), col(3 * DN_HEADS),
                  pl.BlockSpec((1, lt, LANE), lambda b, h: (b, 0, 4 * DN_HEADS)),
                  cw(0), cw(DN_HEADS), cw(2 * DN_HEADS), const, const, const],
        out_specs=pl.BlockSpec((1, lt, LANE), lambda b, h: (b, 0, h)),
        out_shape=jax.ShapeDtypeStruct((bsz, lt, DN_WIDTH), BF16),
        scratch_shapes=[seq(F32), seq(F32), seq(F32), blk(LANE, F32), seq(F32), seq(F32), seq(F32)] + per_dir + per_dir,
        compiler_params=_params(2),
        name="deltanet_core",
    )(p, p, p, p, p, conv_w, conv_w, conv_w, gate_pad(a_log), gate_pad(dt_bias), norm_g.reshape(1, LANE))


S5_T = 8
S5_BB = 4
S5_BLK_GROUPS = LANE // SS_GROUP
S5_HALF = S5_BLK_GROUPS * SS_STATE
S5_K = S5_T * LANE


def _s5_weights(a_re, a_im, log_dt, b_re, b_im, c_re, c_im):
    hp = lax.Precision.HIGHEST
    t = S5_T
    nj = SS_GROUPS // S5_BLK_GROUPS
    ar = jnp.minimum(a_re, -1e-4)
    ai = a_im
    dt = jnp.exp(log_dt)[..., None]
    zr, zi = ar * dt, ai * dt
    d = jnp.arange(t + 1, dtype=F32)[:, None, None, None]
    mag = jnp.exp(d * zr)
    er, ei = mag * jnp.cos(d * zi), mag * jnp.sin(d * zi)
    nr, ni = er[1] - 1.0, ei[1]
    den = ar * ar + ai * ai
    fr, fi = (nr * ar + ni * ai) / den, (ni * ar - nr * ai) / den
    bbr = fr[..., None] * b_re - fi[..., None] * b_im
    bbi = fr[..., None] * b_im + fi[..., None] * b_re
    cer = c_re[None] * er[:, :, :, None, :] - c_im[None] * ei[:, :, :, None, :]
    cei = c_re[None] * ei[:, :, :, None, :] + c_im[None] * er[:, :, :, None, :]
    kk = (jnp.einsum('tdgip,dgpj->tdgij', cer, bbr, precision=hp)
          - jnp.einsum('tdgip,dgpj->tdgij', cei, bbi, precision=hp))
    s_idx = jnp.arange(t)[:, None]
    t_idx = jnp.arange(t)[None, :]
    lag_f = jnp.clip(t_idx - s_idx, 0, t)
    lag_b = jnp.clip(s_idx - t_idx, 0, t)
    m_f = (t_idx >= s_idx)[:, :, None, None, None]
    m_b = (s_idx >= t_idx)[:, :, None, None, None]
    kst = jnp.where(m_f, kk[lag_f, 0], 0.0) + jnp.where(m_b, kk[lag_b, 1], 0.0)
    eye = jnp.eye(S5_BLK_GROUPS, dtype=F32)
    kst = kst.reshape(t, t, nj, S5_BLK_GROUPS, SS_GROUP, SS_GROUP)
    w_intra = jnp.einsum('stJgij,gh->Jsgjthi', kst, eye).reshape(nj, S5_K, S5_K)

    pow_f = (t - 1 - jnp.arange(t))
    pow_b = jnp.arange(t)

    def carry_w(pw, dd):
        e_r, e_i = er[pw, dd], ei[pw, dd]
        re = e_r[..., None] * bbr[dd][None] - e_i[..., None] * bbi[dd][None]
        im = e_r[..., None] * bbi[dd][None] + e_i[..., None] * bbr[dd][None]
        return jnp.stack([re, im], 0)

    wb = jnp.stack([carry_w(pow_f, 0), carry_w(pow_b, 1)], 0)
    wb = wb.reshape(2, 2, t, nj, S5_BLK_GROUPS, SS_STATE, SS_GROUP)
    w_carry = jnp.einsum('drsJgpj,gh->Jsgjdrhp', wb, eye).reshape(nj, S5_K, 4 * S5_HALF)

    def read_w(pw, dd):
        wc = jnp.stack([cer[pw, dd], -cei[pw, dd]], 0)
        wc = wc.reshape(2, t, nj, S5_BLK_GROUPS, SS_GROUP, SS_STATE)
        return jnp.einsum('rtJgip,gh->Jrgpthi', wc, eye).reshape(nj, 2 * S5_HALF, S5_K)

    w_read_f = read_w(jnp.arange(t) + 1, 0)
    w_read_b = read_w(t - jnp.arange(t), 1)
    lam_t = jnp.stack([er[t, 0], ei[t, 0], er[t, 1], ei[t, 1]], 0)
    lam_t = lam_t.reshape(4, nj, S5_HALF).transpose(1, 0, 2)
    return (w_carry.astype(BF16), w_intra.astype(BF16), w_read_f.astype(BF16), w_read_b.astype(BF16), lam_t)


def _s5_kernel(x_ref, mod_ref, wb_ref, wi_ref, wcf_ref, wcb_ref, lt_ref, y_ref, xcat_ref, hf_ref, hb_ref,
               *, n_chunk, n_ctx_chunk):
    bh = pl.program_id(1)
    nb = x_ref.shape[0]
    n_lb = hf_ref.shape[0]
    half_lb = n_lb // 2
    is_ctx = lax.broadcasted_iota(jnp.int32, (n_chunk, 1), 0) < n_ctx_chunk
    sh_c = mod_ref[8, 0:1, :]
    sc_c = mod_ref[8, 1:2, :]

    def carry_in(bl, _):
        mb = mod_ref[bh * nb + bl]
        sh = jnp.where(is_ctx, sh_c, mb[0:1, :])
        sc1 = 1.0 + jnp.where(is_ctx, sc_c, mb[1:2, :])
        rows = pl.ds(pl.multiple_of(bl * n_chunk, n_chunk), n_chunk)
        for tau in range(S5_T):
            xt = x_ref[bl, pl.ds(tau, n_chunk, stride=S5_T), :]
            xcat_ref[rows, tau * LANE:(tau + 1) * LANE] = (xt * sc1 + sh).astype(BF16)
        hbv = jnp.dot(xcat_ref[rows, :], wb_ref[0], preferred_element_type=F32)
        for l in range(n_lb):
            hf_ref[l, rows, :] = hbv[:, l * LANE:(l + 1) * LANE]
            hb_ref[l, rows, :] = hbv[:, (n_lb + l) * LANE:(n_lb + l + 1) * LANE]
        return 0

    lax.fori_loop(0, nb, carry_in, 0)

    lt = lt_ref[0]

    def lam_blocks(i):
        return [jnp.broadcast_to(lt[i:i + 1, l * LANE:(l + 1) * LANE], (nb, LANE)) for l in range(half_lb)]

    a_f = (lam_blocks(0), lam_blocks(1))
    a_b = (lam_blocks(2), lam_blocks(3))

    def advance(h_ref, c, a, state):
        a_re, a_im = a
        new = []
        for l in range(half_lb):
            sel = pl.ds(c, nb, stride=n_chunk)
            in_re = h_ref[l, sel, :]
            in_im = h_ref[half_lb + l, sel, :]
            s_re, s_im = state[l], state[half_lb + l]
            h_ref[l, sel, :] = s_re
            h_ref[half_lb + l, sel, :] = s_im
            new.append((a_re[l] * s_re - a_im[l] * s_im + in_re, a_re[l] * s_im + a_im[l] * s_re + in_im))
        return tuple(n[0] for n in new) + tuple(n[1] for n in new)

    def step(k, carry):
        st_f, st_b = carry
        cb = jnp.where(k < n_ctx_chunk, n_ctx_chunk - 1 - k, n_chunk + n_ctx_chunk - 1 - k)
        return advance(hf_ref, k, a_f, st_f), advance(hb_ref, cb, a_b, st_b)

    zero = tuple(jnp.zeros((nb, LANE), F32) for _ in range(n_lb))
    lax.fori_loop(0, n_chunk, step, (zero, zero))

    def read_out(bl, _):
        rows = pl.ds(pl.multiple_of(bl * n_chunk, n_chunk), n_chunk)
        h_f = jnp.concatenate([hf_ref[l, rows, :] for l in range(n_lb)], -1).astype(BF16)
        h_b = jnp.concatenate([hb_ref[l, rows, :] for l in range(n_lb)], -1).astype(BF16)
        y = (jnp.dot(xcat_ref[rows, :], wi_ref[0], preferred_element_type=F32)
             + jnp.dot(h_f, wcf_ref[0], preferred_element_type=F32)
             + jnp.dot(h_b, wcb_ref[0], preferred_element_type=F32))
        for t in range(S5_T):
            y_ref[bl, pl.ds(t, n_chunk, stride=S5_T), :] = y[:, t * LANE:(t + 1) * LANE]
        return 0

    lax.fori_loop(0, nb, read_out, 0)


def _s5_core(xs, mods, weights, layer):
    bsz, lt, _ = xs.shape
    w_carry, w_intra, w_read_f, w_read_b, lam_t = weights
    nj = D_MODEL // LANE
    nb = min(S5_BB, bsz)
    n_chunk = lt // S5_T
    kernel = functools.partial(_s5_kernel, n_chunk=n_chunk, n_ctx_chunk=CTX_LEN // S5_T)
    wspec = lambda shape: pl.BlockSpec((1,) + shape, lambda j, bh: (j, 0, 0))
    return pl.pallas_call(
        kernel,
        grid=(nj, bsz // nb),
        in_specs=[pl.BlockSpec((nb, lt, LANE), lambda j, bh: (bh, 0, j)),
                  pl.BlockSpec((MOD_ROWS, 6, LANE), lambda j, bh: (layer, 0, j)),
                  wspec((S5_K, 4 * S5_HALF)), wspec((S5_K, S5_K)),
                  wspec((2 * S5_HALF, S5_K)), wspec((2 * S5_HALF, S5_K)),
                  wspec((4, S5_HALF))],
        out_specs=pl.BlockSpec((nb, lt, LANE), lambda j, bh: (bh, 0, j)),
        out_shape=jax.ShapeDtypeStruct((bsz, lt, D_MODEL), F32),
        scratch_shapes=[pltpu.VMEM((nb * n_chunk, S5_K), BF16),
                        pltpu.VMEM((2 * S5_HALF // LANE, nb * n_chunk, LANE), F32),
                        pltpu.VMEM((2 * S5_HALF // LANE, nb * n_chunk, LANE), F32)],
        compiler_params=_params(2),
        name="s5_core",
    )(xs, mods, w_carry, w_intra, w_read_f, w_read_b, lam_t)


def kernel(x, c, ctx, c_ctx, ada_w, ada_b, ln_g, ln_b, mlp_w1, mlp_w2, dn_w_in, dn_conv, dn_a_log, dn_dt_bias, dn_norm_g, dn_w_out, da_w_qkv, da_lambda, da_norm_g, da_w_out, ga_w_qkv, ga_q_norm, ga_k_norm, ga_w_out, ss_a_re, ss_a_im, ss_log_dt, ss_b_re, ss_b_im, ss_c_re, ss_c_im, ss_d, ss_w_glu):
    bsz, n_latent, _ = x.shape
    xs = jnp.concatenate([ctx, x], 1)
    c_rows = jnp.concatenate([c, c_ctx[None, :], jnp.zeros((MOD_ROWS - bsz - 1, D_MODEL), F32)], 0)
    mods = _ada_all(c_rows, ada_w, ada_b).reshape(DEPTH * MOD_ROWS, 6, D_MODEL)
    ones = jnp.ones((LANE,), F32)
    zeros4 = jnp.zeros((4, DA_HEAD_DIM), F32)

    for i in range(DEPTH):
        m, j = i % 4, i // 4
        if m == 0:
            w_in = jnp.pad(dn_w_in[j], ((0, 0), (0, LANE - 4 * DN_HEADS))).astype(BF16)
            p = _proj(xs, mods, w_in, i)
            o = _deltanet_core(p, dn_conv[j], dn_a_log[j], dn_dt_bias[j], dn_norm_g[j])
            xs = _out_proj(o, xs, mods, dn_w_out[j].astype(BF16), ln_g[i, 0], ln_b[i, 0], i)
        elif m == 1:
            lambda_init = 0.8 - 0.6 * math.exp(-0.3 * i)
            p = _proj(xs, mods, da_w_qkv[j].astype(BF16), i)
            cos, sin = _rope_tables(n_latent, DA_HEAD_DIM)
            o = _attention(p, cos, sin, ones, ones, da_lambda[j], da_norm_g[j],
                           n_kv=DA_HEADS, group=1, q_col=0, k_col=DA_HEADS, v_col=2 * DA_HEADS,
                           n_maps=2, qk_norm=False, scale=DA_HEAD_DIM ** -0.5, lambda_init=lambda_init)
            xs = _out_proj(o, xs, mods, da_w_out[j].astype(BF16), ln_g[i, 0], ln_b[i, 0], i)
        elif m == 2:
            p = _proj(xs, mods, ga_w_qkv[j].astype(BF16), i)
            cos, sin = _rope_tables(n_latent, GA_HEAD_DIM)
            o = _attention(p, cos, sin, ga_q_norm[j], ga_k_norm[j], zeros4, ones,
                           n_kv=GA_KV_HEADS, group=GA_HEADS // GA_KV_HEADS, q_col=0, k_col=GA_HEADS,
                           v_col=GA_HEADS + GA_KV_HEADS, n_maps=1, qk_norm=True,
                           scale=GA_HEAD_DIM ** -0.5, lambda_init=0.0)
            xs = _out_proj(o, xs, mods, ga_w_out[j].astype(BF16), ln_g[i, 0], ln_b[i, 0], i)
        else:
            weights = _s5_weights(ss_a_re[j], ss_a_im[j], ss_log_dt[j], ss_b_re[j], ss_b_im[j],
                                  ss_c_re[j], ss_c_im[j])
            y = _s5_core(xs, mods, weights, i)
            xs = _s5_out(y, xs, mods, ss_d[j], ss_w_glu[j].astype(BF16), ln_g[i, 0], ln_b[i, 0], i)
        xs = _mlp(xs, mods, mlp_w1[i].astype(BF16), mlp_w2[i].astype(BF16), ln_g[i, 1], ln_b[i, 1], i)
    return xs[:, CTX_LEN:]
```

```python
import functools
import math

import jax
import jax.numpy as jnp
from jax import lax
from jax.experimental import pallas as pl
from jax.experimental.pallas import tpu as pltpu

F32 = jnp.float32
BF16 = jnp.bfloat16

D_MODEL = 1024
D_FF = 4 * D_MODEL
DEPTH = 4
GRID_W = 64
CTX_LEN = 256
ROPE_THETA = 10000.0
DEEPNORM_ALPHA = (2 * DEPTH) ** 0.25
TM = 256
LANE = 128
MOD_ROWS = 16
VMEM_LIMIT = 56 * 1024 * 1024

DN_HEADS = 8
DN_HEAD_DIM = 128
DN_WIDTH = DN_HEADS * DN_HEAD_DIM
DN_CONV = 5
DN_CHUNK = 64
DA_HEADS = 8
DA_HEAD_DIM = 64
GA_HEADS = 8
GA_KV_HEADS = 2
GA_HEAD_DIM = 128
SS_GROUP = 16
SS_GROUPS = D_MODEL // SS_GROUP
SS_STATE = 64


def _params(n_axes):
    return pltpu.CompilerParams(dimension_semantics=("arbitrary",) * n_axes,
                                vmem_limit_bytes=VMEM_LIMIT)


def _layer_norm(v, g, b):
    mu = jnp.mean(v, -1, keepdims=True)
    d = v - mu
    var = jnp.mean(d * d, -1, keepdims=True)
    return d * lax.rsqrt(var + 1e-5) * g + b


def _mod_spec(layer, skip=0):
    return pl.BlockSpec((1, 6, D_MODEL),
                        lambda b, r: (layer * MOD_ROWS + jnp.where(r + skip == 0, 8, b), 0, 0))


def _row_spec(width, skip=0):
    return pl.BlockSpec((1, TM, width), lambda b, r: (b, r + skip, 0))


def _ada_kernel(c_ref, w_ref, b_ref, o_ref):
    c = c_ref[...]
    act = (c * jax.nn.sigmoid(c)).astype(BF16)
    o_ref[0] = jnp.dot(act, w_ref[0].astype(BF16), preferred_element_type=F32) + b_ref[0]


def _ada_all(c_rows, ada_w, ada_b):
    tn = 1536
    n = 6 * D_MODEL
    return pl.pallas_call(
        _ada_kernel,
        grid=(DEPTH, n // tn),
        in_specs=[pl.BlockSpec((MOD_ROWS, D_MODEL), lambda i, j: (0, 0)),
                  pl.BlockSpec((1, D_MODEL, tn), lambda i, j: (i, 0, j)),
                  pl.BlockSpec((1, 1, tn), lambda i, j: (i, 0, j))],
        out_specs=pl.BlockSpec((1, MOD_ROWS, tn), lambda i, j: (i, 0, j)),
        out_shape=jax.ShapeDtypeStruct((DEPTH, MOD_ROWS, n), F32),
        compiler_params=_params(2),
        name="ada_mod",
    )(c_rows, ada_w, ada_b.reshape(DEPTH, 1, n))


def _proj_kernel(x_ref, mod_ref, w_ref, o_ref):
    sh = mod_ref[0, 0:1, :]
    sc = mod_ref[0, 1:2, :]
    h = (x_ref[0] * (1.0 + sc) + sh).astype(BF16)
    o_ref[0] = jnp.dot(h, w_ref[...], preferred_element_type=F32).astype(o_ref.dtype)


def _proj(xs, mods, w, layer, out_dtype=F32):
    bsz, lt, _ = xs.shape
    n = w.shape[1]
    return pl.pallas_call(
        _proj_kernel,
        grid=(bsz, lt // TM),
        in_specs=[pl.BlockSpec((1, TM, D_MODEL), lambda b, r: (b, r, 0)),
                  _mod_spec(layer),
                  pl.BlockSpec((D_MODEL, n), lambda b, r: (0, 0))],
        out_specs=pl.BlockSpec((1, TM, n), lambda b, r: (b, r, 0)),
        out_shape=jax.ShapeDtypeStruct((bsz, lt, n), out_dtype),
        compiler_params=_params(2),
        name="mod_proj",
    )(xs, mods, w)


FF_CHUNK = 1024


def _mlp_body(x, mod_ref, w1_ref, w2_ref, g_ref, b_ref, y_ref):
    sh = mod_ref[0, 3:4, :]
    sc = mod_ref[0, 4:5, :]
    gate = mod_ref[0, 5:6, :]
    h = (x * (1.0 + sc) + sh).astype(BF16)
    acc = jnp.zeros((TM, D_MODEL), F32)
    for j in range(D_FF // FF_CHUNK):
        a = jnp.dot(h, w1_ref[:, j * FF_CHUNK:(j + 1) * FF_CHUNK], preferred_element_type=F32)
        a = jnp.square(jnp.maximum(a, 0.0)).astype(BF16)
        acc = acc + jnp.dot(a, w2_ref[j * FF_CHUNK:(j + 1) * FF_CHUNK, :], preferred_element_type=F32)
    y_ref[0] = _layer_norm(DEEPNORM_ALPHA * x + gate * acc, g_ref[...], b_ref[...])


def _mlp_kernel(x_ref, mod_ref, w1_ref, w2_ref, g_ref, b_ref, y_ref):
    _mlp_body(x_ref[0], mod_ref, w1_ref, w2_ref, g_ref, b_ref, y_ref)


def _out_mlp_kernel(o_ref, x_ref, mod_ref, wo_ref, g1_ref, b1_ref, w1_ref, w2_ref, g2_ref, b2_ref, y_ref):
    y = jnp.dot(o_ref[0], wo_ref[...], preferred_element_type=F32)
    x1 = _layer_norm(DEEPNORM_ALPHA * x_ref[0] + mod_ref[0, 2:3, :] * y, g1_ref[...], b1_ref[...])
    _mlp_body(x1, mod_ref, w1_ref, w2_ref, g2_ref, b2_ref, y_ref)


def _out_mlp(o, xs, mods, w_out, w1, w2, ln_g, ln_b, layer, latent_only=False):
    bsz, lt, k = o.shape
    skip = 1 if latent_only else 0
    const = lambda shape: pl.BlockSpec(shape, lambda b, r: (0, 0))
    vec = lambda t: t.reshape(1, D_MODEL)
    return pl.pallas_call(
        _out_mlp_kernel,
        grid=(bsz, lt // TM - skip),
        in_specs=[_row_spec(k, skip), _row_spec(D_MODEL, skip), _mod_spec(layer, skip),
                  const((k, D_MODEL)), const((1, D_MODEL)), const((1, D_MODEL)),
                  const((D_MODEL, D_FF)), const((D_FF, D_MODEL)), const((1, D_MODEL)), const((1, D_MODEL))],
        out_specs=_row_spec(D_MODEL),
        out_shape=jax.ShapeDtypeStruct((bsz, lt - skip * TM, D_MODEL), F32),
        compiler_params=_params(2),
        name="out_mlp_ln",
    )(o, xs, mods, w_out, vec(ln_g[0]), vec(ln_b[0]), w1, w2, vec(ln_g[1]), vec(ln_b[1]))


def _mlp(xs, mods, w1, w2, ln_g, ln_b, layer, latent_only=False):
    bsz, rows, _ = xs.shape
    skip = 1 if latent_only else 0
    return pl.pallas_call(
        _mlp_kernel,
        grid=(bsz, rows // TM),
        in_specs=[_row_spec(D_MODEL),
                  _mod_spec(layer, skip),
                  pl.BlockSpec((D_MODEL, D_FF), lambda b, r: (0, 0)),
                  pl.BlockSpec((D_FF, D_MODEL), lambda b, r: (0, 0)),
                  pl.BlockSpec((1, D_MODEL), lambda b, r: (0, 0)),
                  pl.BlockSpec((1, D_MODEL), lambda b, r: (0, 0))],
        out_specs=_row_spec(D_MODEL),
        out_shape=jax.ShapeDtypeStruct((bsz, rows, D_MODEL), F32),
        compiler_params=_params(2),
        name="mlp_ln",
    )(xs, mods, w1, w2, ln_g.reshape(1, D_MODEL), ln_b.reshape(1, D_MODEL))


def _rope_tables(n_latent, head_dim):
    rows = n_latent // GRID_W
    row = jnp.repeat(jnp.arange(rows), GRID_W).astype(F32)
    col = jnp.tile(jnp.arange(GRID_W), rows).astype(F32)
    n_freq = head_dim // 4
    inv_freq = ROPE_THETA ** (-jnp.arange(n_freq, dtype=F32) / n_freq)
    ang = jnp.concatenate([row[:, None] * inv_freq, col[:, None] * inv_freq], -1)
    cos = jnp.repeat(jnp.cos(ang), 2, axis=-1)
    sin = jnp.repeat(jnp.sin(ang), 2, axis=-1)
    sign = jnp.tile(jnp.array([-1.0, 1.0], F32), head_dim // 2)
    sin = sin * sign
    reps = LANE // head_dim
    cos = jnp.tile(cos, (1, reps))
    sin = jnp.tile(sin, (1, reps))
    cos = jnp.concatenate([jnp.ones((CTX_LEN, LANE), F32), cos], 0)
    sin = jnp.concatenate([jnp.zeros((CTX_LEN, LANE), F32), sin], 0)
    return cos, sin


ATTN_SPLIT = 2


def _rope(x, cos, sin_signed):
    lane = lax.broadcasted_iota(jnp.int32, x.shape, 1)
    nxt = pltpu.roll(x, LANE - 1, 1)
    prv = pltpu.roll(x, 1, 1)
    swapped = jnp.where(lane % 2 == 0, nxt, prv)
    return x * cos + swapped * sin_signed


def _rms(x, g):
    return x * lax.rsqrt(jnp.mean(x * x, -1, keepdims=True) + 1e-6) * g


def _attn_kernel(q_ref, k_ref, v_ref, cos_ref, sin_ref, qg_ref, kg_ref, lam_ref, ng_ref, o_ref,
                 kb_ref, vb_ref, *, n_maps, qk_norm, scale, lambda_init):
    g = pl.program_id(2)
    qt = pl.program_id(3)
    lt = kb_ref.shape[0]

    @pl.when((g == 0) & (qt == 0))
    def _prep_kv():
        k = k_ref[0]
        if qk_norm:
            k = _rms(k, kg_ref[...])
        kb_ref[...] = _rope(k, cos_ref[...], sin_ref[...]).astype(BF16)
        vb_ref[...] = v_ref[0].astype(BF16)

    q = q_ref[0]
    if qk_norm:
        q = _rms(q, qg_ref[...])
    row0 = pl.multiple_of(qt * TM, TM)
    q = _rope(q, cos_ref[pl.ds(row0, TM), :], sin_ref[pl.ds(row0, TM), :]) * (scale * math.log2(math.e))

    def scores(qm, nk):
        return lax.dot_general(qm.astype(BF16), kb_ref[0:nk, :], (((1,), (1,)), ((), ())),
                               preferred_element_type=F32)

    def softmax_parts(s):
        m = jnp.max(s, -1, keepdims=True)
        e = jnp.exp2(s - m)
        return e, jnp.sum(e, -1, keepdims=True)

    def all_scores(qr, nk):
        if n_maps == 1:
            return (scores(qr, nk),)
        lane = lax.broadcasted_iota(jnp.int32, qr.shape, 1)
        return (scores(jnp.where(lane < LANE // 2, qr, 0.0), nk), scores(jnp.where(lane >= LANE // 2, qr, 0.0), nk))

    def weighted_values(s, nk):
        vb = vb_ref[0:nk, :]
        if n_maps == 1:
            e, l = softmax_parts(s[0])
            return jnp.dot(e.astype(BF16), vb, preferred_element_type=F32) * (1.0 / l)
        lp = lam_ref[...]
        lam = (jnp.exp(jnp.sum(lp[0:1, :] * lp[1:2, :])) - jnp.exp(jnp.sum(lp[2:3, :] * lp[3:4, :]))
               + lambda_init)
        e0, l0 = softmax_parts(s[0])
        e1, l1 = softmax_parts(s[1])
        o = (jnp.dot(e0.astype(BF16), vb, preferred_element_type=F32) * (1.0 / l0)
             - jnp.dot(e1.astype(BF16), vb, preferred_element_type=F32) * (lam / l1))
        return _rms(o, ng_ref[...]) * (1.0 - lambda_init)

    def attend(nk):
        rows = TM // ATTN_SPLIT
        s_all = [all_scores(q[r * rows:(r + 1) * rows], nk) for r in range(ATTN_SPLIT)]
        for r in range(ATTN_SPLIT):
            o_ref[0, r * rows:(r + 1) * rows, :] = weighted_values(s_all[r], nk).astype(o_ref.dtype)

    @pl.when(qt == 0)
    def _ctx():
        attend(CTX_LEN)

    @pl.when(qt != 0)
    def _latent():
        attend(lt)


def _attention(p, cos, sin, q_gain, k_gain, lam_p, norm_g, *, n_kv, group, q_col, k_col, v_col,
               n_maps, qk_norm, scale, lambda_init):
    bsz, lt, _ = p.shape
    kernel = functools.partial(_attn_kernel, n_maps=n_maps, qk_norm=qk_norm, scale=scale,
                               lambda_init=lambda_init)
    const = lambda b, kv, g, qt: (0, 0)
    return pl.pallas_call(
        kernel,
        grid=(bsz, n_kv, group, lt // TM),
        in_specs=[pl.BlockSpec((1, TM, LANE), lambda b, kv, g, qt: (b, qt, q_col + kv * group + g)),
                  pl.BlockSpec((1, lt, LANE), lambda b, kv, g, qt: (b, 0, k_col + kv)),
                  pl.BlockSpec((1, lt, LANE), lambda b, kv, g, qt: (b, 0, v_col + kv)),
                  pl.BlockSpec((lt, LANE), const),
                  pl.BlockSpec((lt, LANE), const),
                  pl.BlockSpec((1, LANE), const),
                  pl.BlockSpec((1, LANE), const),
                  pl.BlockSpec(lam_p.shape, const),
                  pl.BlockSpec((1, LANE), const)],
        out_specs=pl.BlockSpec((1, TM, LANE), lambda b, kv, g, qt: (b, qt, kv * group + g)),
        out_shape=jax.ShapeDtypeStruct((bsz, lt, n_kv * group * LANE), BF16),
        scratch_shapes=[pltpu.VMEM((lt, LANE), BF16), pltpu.VMEM((lt, LANE), BF16)],
        compiler_params=_params(4),
        name="attention",
    )(p, p, p, cos, sin, q_gain.reshape(1, LANE), k_gain.reshape(1, LANE), lam_p, norm_g.reshape(1, LANE))


def _s5_out_kernel(y_ref, x_ref, mod_ref, d_ref, w_ref, g_ref, b_ref, o_ref):
    sh = mod_ref[0, 0:1, :]
    sc = mod_ref[0, 1:2, :]
    gate = mod_ref[0, 2:3, :]
    x = x_ref[0]
    u = x * (1.0 + sc) + sh
    y = jax.nn.gelu(y_ref[0] + d_ref[...] * u).astype(BF16)
    z = jnp.dot(y, w_ref[...], preferred_element_type=F32)
    out = z[:, :D_MODEL] * jax.nn.sigmoid(z[:, D_MODEL:])
    o_ref[0] = _layer_norm(DEEPNORM_ALPHA * x + gate * out, g_ref[...], b_ref[...])


def _s5_out(y, xs, mods, d_skip, w_glu, ln_g, ln_b, layer, latent_only=False):
    bsz, lt, _ = xs.shape
    skip = 1 if latent_only else 0
    return pl.pallas_call(
        _s5_out_kernel,
        grid=(bsz, lt // TM - skip),
        in_specs=[_row_spec(D_MODEL, skip),
                  _row_spec(D_MODEL, skip),
                  _mod_spec(layer, skip),
                  pl.BlockSpec((1, D_MODEL), lambda b, r: (0, 0)),
                  pl.BlockSpec((D_MODEL, 2 * D_MODEL), lambda b, r: (0, 0)),
                  pl.BlockSpec((1, D_MODEL), lambda b, r: (0, 0)),
                  pl.BlockSpec((1, D_MODEL), lambda b, r: (0, 0))],
        out_specs=_row_spec(D_MODEL),
        out_shape=jax.ShapeDtypeStruct((bsz, lt - skip * TM, D_MODEL), F32),
        compiler_params=_params(2),
        name="s5_out_ln",
    )(y, xs, mods, d_skip.reshape(1, D_MODEL), w_glu, ln_g.reshape(1, D_MODEL), ln_b.reshape(1, D_MODEL))


DN_BLK = 128
DN_GATE_BETA_F, DN_GATE_BETA_B, DN_GATE_A_F, DN_GATE_A_B = 0, DN_HEADS, 2 * DN_HEADS, 3 * DN_HEADS
_NT = (((1,), (1,)), ((), ()))
_TN = (((0,), (0,)), ((), ()))


def _dn_kernel(q_ref, k_ref, v_ref, z_ref, g_ref, cq_ref, ck_ref, cv_ref, alog_ref, dtb_ref, ng_ref, o_ref,
               beta_ref, gc_ref, tot_ref, gct_ref, qn_ref, kn_ref, vn_ref,
               wq_f, u_f, qk_f, kd_f, gl_f, o_f, wq_b, u_b, qk_b, kd_b, gl_b, o_b, *, n_blk, n_ctx_blk):
    h = pl.program_id(1)
    lt = n_blk * DN_BLK
    ctx_rows = n_ctx_blk * DN_BLK
    row = lax.broadcasted_iota(jnp.int32, (lt, 1), 0)
    lane = lax.broadcasted_iota(jnp.int32, (1, LANE), 1)

    @pl.when(h == 0)
    def _gates():
        gts = g_ref[0]
        beta_ref[...] = jax.nn.sigmoid(gts)
        g = -jnp.exp(alog_ref[...]) * jax.nn.softplus(gts + dtb_ref[...])
        pos = row % DN_BLK
        pre = g
        suf = g
        s = 1
        while s < DN_BLK:
            pre = pre + jnp.where(pos >= s, pltpu.roll(pre, s, 0), 0.0)
            suf = suf + jnp.where(pos < DN_BLK - s, pltpu.roll(suf, lt - s, 0), 0.0)
            s *= 2
        gc = jnp.where(lane >= DN_GATE_A_B, suf, pre)
        gc_ref[...] = gc
        tot_ref[...] = pre + suf - g
        for c in range(n_blk):
            gct_ref[c] = gc[c * DN_BLK:(c + 1) * DN_BLK, :].T

    def conv_silu(x_ref, w_ref):
        x = x_ref[0]
        w = w_ref[...]
        acc = x * w[DN_CONV // 2:DN_CONV // 2 + 1, :]
        for d in range(-(DN_CONV // 2), DN_CONV // 2 + 1):
            if d == 0:
                continue
            src = row + d
            ok = (src >= 0) & (src < lt) & ((src < ctx_rows) == (row < ctx_rows))
            acc = acc + jnp.where(ok, pltpu.roll(x, (-d) % lt, 0), 0.0) * w[d + DN_CONV // 2:d + DN_CONV // 2 + 1, :]
        return acc * jax.nn.sigmoid(acc)

    def l2n(x):
        return x * lax.rsqrt(jnp.sum(x * x, -1, keepdims=True) + 1e-6)

    qn_ref[...] = l2n(conv_silu(q_ref, cq_ref)) * DN_HEAD_DIM ** -0.5
    kn_ref[...] = l2n(conv_silu(k_ref, ck_ref))
    vn_ref[...] = conv_silu(v_ref, cv_ref)

    ii = lax.broadcasted_iota(jnp.int32, (DN_BLK, DN_BLK), 0)
    jj = lax.broadcasted_iota(jnp.int32, (DN_BLK, DN_BLK), 1)
    fwd =(DN_GATE_BETA_F, DN_GATE_A_F, ii >= jj, ii > jj, (wq_f, u_f, qk_f, kd_f, gl_f, o_f))
    bwd = (DN_GATE_BETA_B, DN_GATE_A_B, ii <= jj, ii < jj, (wq_b, u_b, qk_b, kd_b, gl_b, o_b))

    ii2 = lax.broadcasted_iota(jnp.int32, (2 * DN_BLK, 2 * DN_BLK), 0)
    jj2 = lax.broadcasted_iota(jnp.int32, (2 * DN_BLK, 2 * DN_BLK), 1)
    eye2 = (ii2 == jj2).astype(F32)
    zero_blk = jnp.zeros((DN_BLK, DN_BLK), F32)

    def pair_off(s):
        return ((ii2 // (2 * s)) == (jj2 // (2 * s))) & ((ii2 // s) != (jj2 // s))

    def column(ref, rows, lane_idx):
        return jnp.sum(jnp.where(lane == lane_idx, ref[rows, :], 0.0), -1, keepdims=True)

    def prepare(c, _):
        rows = pl.ds(pl.multiple_of(c * DN_BLK, DN_BLK), DN_BLK)
        qc, kc, vc = qn_ref[rows, :], kn_ref[rows, :], vn_ref[rows, :]
        kb = kc.astype(BF16)
        kk = lax.dot_general(kb, kb, _NT, preferred_element_type=F32)
        qk = lax.dot_general(qc.astype(BF16), kb, _NT, preferred_element_type=F32)
        a_dir, rhs_dir = [], []
        for lane_beta, lane_g, incl, strict, (wq_ref, _, qk_ref, kd_ref, gl_ref, _) in (fwd, bwd):
            bcol = column(beta_ref, rows, lane_beta + h)
            gcol = column(gc_ref, rows, lane_g + h)
            tcol = column(tot_ref, rows, lane_g + h)
            grow = gct_ref[c, pl.ds(lane_g + h, 1), :]
            dec = jnp.exp(jnp.where(incl, gcol - grow, -jnp.inf))
            a_dir.append(jnp.where(strict, bcol * kk * dec, 0.0))
            egc = jnp.exp(gcol)
            rhs_dir.append(jnp.concatenate([bcol * vc, (bcol * egc) * kc], -1).astype(BF16))
            wq_ref[c, DN_BLK:2 * DN_BLK, :] = (qc * egc).astype(BF16)
            qk_ref[c] = (qk * dec).astype(BF16)
            kd_ref[c] = (kc * jnp.exp(tcol - gcol)).T.astype(BF16)
            gl_ref[c] = jnp.broadcast_to(jnp.exp(tcol[0:1, :]), (8, LANE))
        a = jnp.concatenate([jnp.concatenate([a_dir[0], zero_blk], 1),
                             jnp.concatenate([zero_blk, a_dir[1]], 1)], 0)
        t_inv = eye2 - jnp.where(pair_off(1), a, 0.0)
        m = jnp.dot(t_inv.astype(BF16), a.astype(BF16), preferred_element_type=F32)
        s = 2
        while s < DN_BLK:
            m_off = jnp.where(pair_off(s), m, 0.0).astype(BF16)
            if 2 * s < DN_BLK:
                upd = jnp.dot(m_off, jnp.concatenate([t_inv, m], 1).astype(BF16), preferred_element_type=F32)
                t_inv = t_inv - upd[:, :2 * DN_BLK]
                m = m - upd[:, 2 * DN_BLK:]
            else:
                t_inv = t_inv - jnp.dot(m_off, t_inv.astype(BF16), preferred_element_type=F32)
            s *= 2
        uw = jnp.dot(t_inv.astype(BF16), jnp.concatenate(rhs_dir, 0), preferred_element_type=F32)
        for d, (wq_ref, u_ref) in enumerate(((wq_f, u_f), (wq_b, u_b))):
            u_ref[c] = uw[d * DN_BLK:(d + 1) * DN_BLK, :DN_HEAD_DIM]
            wq_ref[c, 0:DN_BLK, :] = uw[d * DN_BLK:(d + 1) * DN_BLK, DN_HEAD_DIM:].astype(BF16)
        return 0

    lax.fori_loop(0, n_blk, prepare, 0, unroll=2)

    zero_bf = jnp.zeros((DN_BLK, DN_BLK), BF16)

    def block_diag(top, bottom):
        return jnp.concatenate([jnp.concatenate([top, zero_bf], 1), jnp.concatenate([zero_bf, bottom], 1)], 0)

    def step(k, carry):
        s_f, s_b = carry
        cb = jnp.where(k < n_ctx_blk, n_ctx_blk - 1 - k, n_blk + n_ctx_blk - 1 - k)
        hd = DN_HEAD_DIM
        r = jnp.dot(jnp.concatenate([wq_f[k], wq_b[cb]], 1), block_diag(s_f.astype(BF16), s_b.astype(BF16)),
                    preferred_element_type=F32)
        v_new = block_diag((u_f[k] - r[:DN_BLK, :hd]).astype(BF16), (u_b[cb] - r[:DN_BLK, hd:]).astype(BF16))
        lhs = jnp.concatenate([jnp.concatenate([qk_f[k], qk_b[cb]], 1),
                               jnp.concatenate([kd_f[k], kd_b[cb]], 1)], 0)
        r2 = jnp.dot(lhs, v_new, preferred_element_type=F32)
        o_f[k] = r[DN_BLK:, :hd] + r2[:DN_BLK, :hd]
        o_b[cb] = r[DN_BLK:, hd:] + r2[:DN_BLK, hd:]
        return (gl_f[k][0:1, :] * s_f + r2[DN_BLK:, :hd], gl_b[cb][0:1, :] * s_b + r2[DN_BLK:, hd:])

    zero = jnp.zeros((DN_HEAD_DIM, DN_HEAD_DIM), F32)
    lax.fori_loop(0, n_blk, step, (zero, zero))

    o = (o_f[...] + o_b[...]).reshape(lt, DN_HEAD_DIM)
    o = o * lax.rsqrt(jnp.mean(o * o, -1, keepdims=True) + 1e-6) * ng_ref[...]
    z = z_ref[0]
    o_ref[0] = (o * (z * jax.nn.sigmoid(z))).astype(o_ref.dtype)


def _deltanet_core(p, conv_w, a_log, dt_bias, norm_g):
    bsz, lt, _ = p.shape
    n_blk = lt // DN_BLK
    gate_pad = lambda t: jnp.pad(t.reshape(1, 2 * DN_HEADS), ((0, 0), (DN_GATE_A_F, LANE - DN_GATE_A_F - 2 * DN_HEADS)))
    kernel = functools.partial(_dn_kernel, n_blk=n_blk, n_ctx_blk=CTX_LEN // DN_BLK)
    col = lambda off: pl.BlockSpec((1, lt, LANE), lambda b, h: (b, 0, off + h))
    cw = lambda off: pl.BlockSpec((DN_CONV, LANE), lambda b, h: (0, off + h))
    const = pl.BlockSpec((1, LANE), lambda b, h: (0, 0))
    seq = lambda dt: pltpu.VMEM((lt, LANE), dt)
    blk = lambda rows, dt: pltpu.VMEM((n_blk, rows, LANE), dt)
    per_dir = [blk(2 * DN_BLK, BF16), blk(DN_BLK, F32), blk(DN_BLK, BF16), blk(DN_BLK, BF16), blk(8, F32),
               blk(DN_BLK, F32)]
    return pl.pallas_call(
        kernel,
        grid=(bsz, DN_HEADS),
        in_specs=[col(0), col(DN_HEADS), col(2 * DN_HEADS), col(3 * DN_HEADS),
                  pl.BlockSpec((1, lt, LANE), lambda b, h: (b, 0, 4 * DN_HEADS)),
                  cw(0), cw(DN_HEADS), cw(2 * DN_HEADS), const, const, const],
        out_specs=pl.BlockSpec((1, lt, LANE), lambda b, h: (b, 0, h)),
        out_shape=jax.ShapeDtypeStruct((bsz, lt, DN_WIDTH), BF16),
        scratch_shapes=[seq(F32), seq(F32), seq(F32), blk(LANE, F32), seq(F32), seq(F32), seq(F32)] + per_dir + per_dir,
        compiler_params=_params(2),
        name="deltanet_core",
    )(p, p, p, p, p, conv_w, conv_w, conv_w, gate_pad(a_log), gate_pad(dt_bias), norm_g.reshape(1, LANE))


S5_T = 8
S5_BB = 4
S5_BLK_GROUPS = LANE // SS_GROUP
S5_HALF = S5_BLK_GROUPS * SS_STATE
S5_K = S5_T * LANE


def _s5_weights(a_re, a_im, log_dt, b_re, b_im, c_re, c_im):
    hp = lax.Precision.HIGHEST
    t = S5_T
    nj = SS_GROUPS // S5_BLK_GROUPS
    ar = jnp.minimum(a_re, -1e-4)
    ai = a_im
    dt = jnp.exp(log_dt)[..., None]
    zr, zi = ar * dt, ai * dt
    d = jnp.arange(t + 1, dtype=F32)[:, None, None, None]
    mag = jnp.exp(d * zr)
    er, ei = mag * jnp.cos(d * zi), mag * jnp.sin(d * zi)
    nr, ni = er[1] - 1.0, ei[1]
    den = ar * ar + ai * ai
    fr, fi = (nr * ar + ni * ai) / den, (ni * ar - nr * ai) / den
    bbr = fr[..., None] * b_re - fi[..., None] * b_im
    bbi = fr[..., None] * b_im + fi[..., None] * b_re
    cer = c_re[None] * er[:, :, :, None, :] - c_im[None] * ei[:, :, :, None, :]
    cei = c_re[None] * ei[:, :, :, None, :] + c_im[None] * er[:, :, :, None, :]
    kk = (jnp.einsum('tdgip,dgpj->tdgij', cer, bbr, precision=hp)
          - jnp.einsum('tdgip,dgpj->tdgij', cei, bbi, precision=hp))
    s_idx = jnp.arange(t)[:, None]
    t_idx = jnp.arange(t)[None, :]
    lag_f = jnp.clip(t_idx - s_idx, 0, t)
    lag_b = jnp.clip(s_idx - t_idx, 0, t)
    m_f = (t_idx >= s_idx)[:, :, None, None, None]
    m_b = (s_idx >= t_idx)[:, :, None, None, None]
    kst = jnp.where(m_f, kk[lag_f, 0], 0.0) + jnp.where(m_b, kk[lag_b, 1], 0.0)
    kst = kst.reshape(t, t, nj, S5_BLK_GROUPS, SS_GROUP, SS_GROUP).transpose(2, 0, 3, 5, 1, 4)
    m_intra = kst.reshape(nj, S5_K, t * SS_GROUP)

    pow_f = (t - 1 - jnp.arange(t))
    pow_b = jnp.arange(t)

    def carry_w(pw, dd):
        e_r, e_i = er[pw, dd], ei[pw, dd]
        re = e_r[..., None] * bbr[dd][None] - e_i[..., None] * bbi[dd][None]
        im = e_r[..., None] * bbi[dd][None] + e_i[..., None] * bbr[dd][None]
        return jnp.stack([re, im], 0)

    wb = jnp.stack([carry_w(pow_f, 0), carry_w(pow_b, 1)], 0)
    wb = wb.reshape(2, 2, t, nj, S5_BLK_GROUPS, SS_STATE, SS_GROUP)
    m_carry = wb.transpose(3, 2, 4, 6, 0, 1, 5).reshape(nj, S5_K, 4 * SS_STATE)

    def read_w(pw, dd):
        wc = jnp.stack([cer[pw, dd], -cei[pw, dd]], 0)
        wc = wc.reshape(2, t, nj, S5_BLK_GROUPS, SS_GROUP, SS_STATE)
        return wc.transpose(2, 0, 3, 5, 1, 4).reshape(nj, 2 * S5_HALF, t * SS_GROUP)

    m_read_f = read_w(jnp.arange(t) + 1, 0)
    m_read_b = read_w(t - jnp.arange(t), 1)
    lam_t = jnp.stack([er[t, 0], ei[t, 0], er[t, 1], ei[t, 1]], 0)
    lam_t = lam_t.reshape(4, nj, S5_HALF).transpose(1, 0, 2)
    return m_carry.astype(BF16), m_intra.astype(BF16), m_read_f.astype(BF16), m_read_b.astype(BF16), lam_t


def _s5_expand(m, row_group_div, col_seg, n_col):
    rows, n_in = m.shape
    q = lax.broadcasted_iota(jnp.int32, (n_in, n_col), 0)
    c = lax.broadcasted_iota(jnp.int32, (n_in, n_col), 1)
    out_seg = col_seg * S5_BLK_GROUPS
    rep = ((q // col_seg == c // out_seg) & (q % col_seg == c % col_seg)).astype(BF16)
    wide = jnp.dot(m, rep, preferred_element_type=F32)
    g_row = (lax.broadcasted_iota(jnp.int32, (rows, n_col), 0) // row_group_div) % S5_BLK_GROUPS
    g_col = (lax.broadcasted_iota(jnp.int32, (rows, n_col), 1) // col_seg) % S5_BLK_GROUPS
    return jnp.where(g_row == g_col, wide, 0.0).astype(BF16)


def _s5_kernel(x_ref, mod_ref, mc_ref, mi_ref, mrf_ref, mrb_ref, lt_ref, y_ref,
               wb_ref, wi_ref, wcf_ref, wcb_ref, xcat_ref, hf_ref, hb_ref, *, n_chunk, n_ctx_chunk):
    bh = pl.program_id(1)

    @pl.when(bh == 0)
    def _expand_weights():
        wb_ref[...] = _s5_expand(mc_ref[0], SS_GROUP, SS_STATE, 4 * S5_HALF)
        wi_ref[...] = _s5_expand(mi_ref[0], SS_GROUP, SS_GROUP, S5_K)
        wcf_ref[...] = _s5_expand(mrf_ref[0], SS_STATE, SS_GROUP, S5_K)
        wcb_ref[...] = _s5_expand(mrb_ref[0], SS_STATE, SS_GROUP, S5_K)

    nb = x_ref.shape[0]
    n_lb = hf_ref.shape[0]
    half_lb = n_lb // 2
    is_ctx = lax.broadcasted_iota(jnp.int32, (n_chunk, 1), 0) < n_ctx_chunk
    sh_c = mod_ref[8, 0:1, :]
    sc_c = mod_ref[8, 1:2, :]

    def carry_in(bl, _):
        mb = mod_ref[bh * nb + bl]
        sh = jnp.where(is_ctx, sh_c, mb[0:1, :])
        sc1 = 1.0 + jnp.where(is_ctx, sc_c, mb[1:2, :])
        rows = pl.ds(pl.multiple_of(bl * n_chunk, n_chunk), n_chunk)
        for tau in range(S5_T):
            xt = x_ref[bl, pl.ds(tau, n_chunk, stride=S5_T), :]
            xcat_ref[rows, tau * LANE:(tau + 1) * LANE] = (xt * sc1 + sh).astype(BF16)
        hbv = jnp.dot(xcat_ref[rows, :], wb_ref[...], preferred_element_type=F32)
        for l in range(n_lb):
            hf_ref[l, rows, :] = hbv[:, l * LANE:(l + 1) * LANE]
            hb_ref[l, rows, :] = hbv[:, (n_lb + l) * LANE:(n_lb + l + 1) * LANE]
        return 0

    lax.fori_loop(0, nb, carry_in, 0)

    lt = lt_ref[0]

    def lam_blocks(i):
        return [jnp.broadcast_to(lt[i:i + 1, l * LANE:(l + 1) * LANE], (nb, LANE)) for l in range(half_lb)]

    a_f = (lam_blocks(0), lam_blocks(1))
    a_b = (lam_blocks(2), lam_blocks(3))

    def advance(h_ref, c, a, state):
        a_re, a_im = a
        new = []
        for l in range(half_lb):
            sel = pl.ds(c, nb, stride=n_chunk)
            in_re = h_ref[l, sel, :]
            in_im = h_ref[half_lb + l, sel, :]
            s_re, s_im = state[l], state[half_lb + l]
            h_ref[l, sel, :] = s_re
            h_ref[half_lb + l, sel, :] = s_im
            new.append((a_re[l] * s_re - a_im[l] * s_im + in_re, a_re[l] * s_im + a_im[l] * s_re + in_im))
        return tuple(n[0] for n in new) + tuple(n[1] for n in new)

    def step(k, carry):
        st_f, st_b = carry
        cb = jnp.where(k < n_ctx_chunk, n_ctx_chunk - 1 - k, n_chunk + n_ctx_chunk - 1 - k)
        return advance(hf_ref, k, a_f, st_f), advance(hb_ref, cb, a_b, st_b)

    zero = tuple(jnp.zeros((nb, LANE), F32) for _ in range(n_lb))
    lax.fori_loop(0, n_chunk, step, (zero, zero))

    def read_out(bl, _):
        rows = pl.ds(pl.multiple_of(bl * n_chunk, n_chunk), n_chunk)
        h_f = jnp.concatenate([hf_ref[l, rows, :] for l in range(n_lb)], -1).astype(BF16)
        h_b = jnp.concatenate([hb_ref[l, rows, :] for l in range(n_lb)], -1).astype(BF16)
        y = (jnp.dot(xcat_ref[rows, :], wi_ref[...], preferred_element_type=F32)
             + jnp.dot(h_f, wcf_ref[...], preferred_element_type=F32)
             + jnp.dot(h_b, wcb_ref[...], preferred_element_type=F32))
        for t in range(S5_T):
            y_ref[bl, pl.ds(t, n_chunk, stride=S5_T), :] = y[:, t * LANE:(t + 1) * LANE]
        return 0

    lax.fori_loop(0, nb, read_out, 0)


def _s5_core(xs, mods, weights, layer):
    bsz, lt, _ = xs.shape
    m_carry, m_intra, m_read_f, m_read_b, lam_t = weights
    nj = D_MODEL // LANE
    nb = min(S5_BB, bsz)
    n_chunk = lt // S5_T
    kernel = functools.partial(_s5_kernel, n_chunk=n_chunk, n_ctx_chunk=CTX_LEN // S5_T)
    wspec = lambda shape: pl.BlockSpec((1,) + shape, lambda j, bh: (j, 0, 0))
    return pl.pallas_call(
        kernel,
        grid=(nj, bsz // nb),
        in_specs=[pl.BlockSpec((nb, lt, LANE), lambda j, bh: (bh, 0, j)),
                  pl.BlockSpec((MOD_ROWS, 6, LANE), lambda j, bh: (layer, 0, j)),
                  wspec(m_carry.shape[1:]), wspec(m_intra.shape[1:]),
                  wspec(m_read_f.shape[1:]), wspec(m_read_b.shape[1:]),
                  wspec((4, S5_HALF))],
        out_specs=pl.BlockSpec((nb, lt, LANE), lambda j, bh: (bh, 0, j)),
        out_shape=jax.ShapeDtypeStruct((bsz, lt, D_MODEL), F32),
        scratch_shapes=[pltpu.VMEM((S5_K, 4 * S5_HALF), BF16), pltpu.VMEM((S5_K, S5_K), BF16),
                        pltpu.VMEM((2 * S5_HALF, S5_K), BF16), pltpu.VMEM((2 * S5_HALF, S5_K), BF16),
                        pltpu.VMEM((nb * n_chunk, S5_K), BF16),
                        pltpu.VMEM((2 * S5_HALF // LANE, nb * n_chunk, LANE), F32),
                        pltpu.VMEM((2 * S5_HALF // LANE, nb * n_chunk, LANE), F32)],
        compiler_params=_params(2),
        name="s5_core",
    )(xs, mods, m_carry, m_intra, m_read_f, m_read_b, lam_t)


def kernel(x, c, ctx, c_ctx, ada_w, ada_b, ln_g, ln_b, mlp_w1, mlp_w2, dn_w_in, dn_conv, dn_a_log, dn_dt_bias, dn_norm_g, dn_w_out, da_w_qkv, da_lambda, da_norm_g, da_w_out, ga_w_qkv, ga_q_norm, ga_k_norm, ga_w_out, ss_a_re, ss_a_im, ss_log_dt, ss_b_re, ss_b_im, ss_c_re, ss_c_im, ss_d, ss_w_glu):
    bsz, n_latent, _ = x.shape
    xs = jnp.concatenate([ctx, x], 1)
    c_rows = jnp.concatenate([c, c_ctx[None, :], jnp.zeros((MOD_ROWS - bsz - 1, D_MODEL), F32)], 0)
    mods = _ada_all(c_rows, ada_w, ada_b).reshape(DEPTH * MOD_ROWS, 6, D_MODEL)
    ones = jnp.ones((LANE,), F32)
    zeros4 = jnp.zeros((4, DA_HEAD_DIM), F32)

    for i in range(DEPTH):
        m, j = i % 4, i // 4
        last = i == DEPTH - 1
        if m == 0:
            w_in = jnp.pad(dn_w_in[j], ((0, 0), (0, LANE - 4 * DN_HEADS))).astype(BF16)
            p = _proj(xs, mods, w_in, i)
            o = _deltanet_core(p, dn_conv[j], dn_a_log[j], dn_dt_bias[j], dn_norm_g[j])
            w_out = dn_w_out[j]
        elif m == 1:
            lambda_init = 0.8 - 0.6 * math.exp(-0.3 * i)
            p = _proj(xs, mods, da_w_qkv[j].astype(BF16), i)
            cos, sin = _rope_tables(n_latent, DA_HEAD_DIM)
            o = _attention(p, cos, sin, ones, ones, da_lambda[j], da_norm_g[j],
                           n_kv=DA_HEADS, group=1, q_col=0, k_col=DA_HEADS, v_col=2 * DA_HEADS,
                           n_maps=2, qk_norm=False, scale=DA_HEAD_DIM ** -0.5, lambda_init=lambda_init)
            w_out = da_w_out[j]
        elif m == 2:
            p = _proj(xs, mods, ga_w_qkv[j].astype(BF16), i)
            cos, sin = _rope_tables(n_latent, GA_HEAD_DIM)
            o = _attention(p, cos, sin, ga_q_norm[j], ga_k_norm[j], zeros4, ones,
                           n_kv=GA_KV_HEADS, group=GA_HEADS // GA_KV_HEADS, q_col=0, k_col=GA_HEADS,
                           v_col=GA_HEADS + GA_KV_HEADS, n_maps=1, qk_norm=True,
                           scale=GA_HEAD_DIM ** -0.5, lambda_init=0.0)
            w_out = ga_w_out[j]
        w1, w2 = mlp_w1[i].astype(BF16), mlp_w2[i].astype(BF16)
        if m == 3:
            weights = _s5_weights(ss_a_re[j], ss_a_im[j], ss_log_dt[j], ss_b_re[j], ss_b_im[j],
                                  ss_c_re[j], ss_c_im[j])
            y = _s5_core(xs, mods, weights, i)
            xs = _s5_out(y, xs, mods, ss_d[j], ss_w_glu[j].astype(BF16), ln_g[i, 0], ln_b[i, 0], i, last)
            xs = _mlp(xs, mods, w1, w2, ln_g[i, 1], ln_b[i, 1], i, last)
        else:
            xs = _out_mlp(o, xs, mods, w_out.astype(BF16), w1, w2, ln_g[i], ln_b[i], i, last)
    return xs
```

```python
import functools
import math

import jax
import jax.numpy as jnp
from jax import lax
from jax.experimental import pallas as pl
from jax.experimental.pallas import tpu as pltpu

F32 = jnp.float32
BF16 = jnp.bfloat16

D_MODEL = 1024
D_FF = 4 * D_MODEL
DEPTH = 4
GRID_W = 64
CTX_LEN = 256
ROPE_THETA = 10000.0
DEEPNORM_ALPHA = (2 * DEPTH) ** 0.25
TM = 256
LANE = 128
MOD_ROWS = 16
VMEM_LIMIT = 56 * 1024 * 1024

DN_HEADS = 8
DN_HEAD_DIM = 128
DN_WIDTH = DN_HEADS * DN_HEAD_DIM
DN_CONV = 5
DN_CHUNK = 64
DA_HEADS = 8
DA_HEAD_DIM = 64
GA_HEADS = 8
GA_KV_HEADS = 2
GA_HEAD_DIM = 128
SS_GROUP = 16
SS_GROUPS = D_MODEL // SS_GROUP
SS_STATE = 64


def _params(n_axes):
    return pltpu.CompilerParams(dimension_semantics=("arbitrary",) * n_axes,
                                vmem_limit_bytes=VMEM_LIMIT)


def _layer_norm(v, g, b):
    mu = jnp.mean(v, -1, keepdims=True)
    d = v - mu
    var = jnp.mean(d * d, -1, keepdims=True)
    return d * lax.rsqrt(var + 1e-5) * g + b


def _mod_spec(layer, skip=0):
    return pl.BlockSpec((1, 6, D_MODEL),
                        lambda b, r: (layer * MOD_ROWS + jnp.where(r + skip == 0, 8, b), 0, 0))


def _row_spec(width, skip=0):
    return pl.BlockSpec((1, TM, width), lambda b, r: (b, r + skip, 0))


def _ada_kernel(c_ref, w_ref, b_ref, o_ref):
    c = c_ref[...]
    act = (c * jax.nn.sigmoid(c)).astype(BF16)
    o_ref[0] = jnp.dot(act, w_ref[0].astype(BF16), preferred_element_type=F32) + b_ref[0]


def _ada_all(c_rows, ada_w, ada_b):
    tn = 1536
    n = 6 * D_MODEL
    return pl.pallas_call(
        _ada_kernel,
        grid=(DEPTH, n // tn),
        in_specs=[pl.BlockSpec((MOD_ROWS, D_MODEL), lambda i, j: (0, 0)),
                  pl.BlockSpec((1, D_MODEL, tn), lambda i, j: (i, 0, j)),
                  pl.BlockSpec((1, 1, tn), lambda i, j: (i, 0, j))],
        out_specs=pl.BlockSpec((1, MOD_ROWS, tn), lambda i, j: (i, 0, j)),
        out_shape=jax.ShapeDtypeStruct((DEPTH, MOD_ROWS, n), F32),
        compiler_params=_params(2),
        name="ada_mod",
    )(c_rows, ada_w, ada_b.reshape(DEPTH, 1, n))


def _proj_kernel(x_ref, mod_ref, w_ref, o_ref):
    sh = mod_ref[0, 0:1, :]
    sc = mod_ref[0, 1:2, :]
    h = (x_ref[0] * (1.0 + sc) + sh).astype(BF16)
    o_ref[0] = jnp.dot(h, w_ref[...], preferred_element_type=F32).astype(o_ref.dtype)


def _proj(xs, mods, w, layer, out_dtype=F32):
    bsz, lt, _ = xs.shape
    n = w.shape[1]
    return pl.pallas_call(
        _proj_kernel,
        grid=(bsz, lt // TM),
        in_specs=[pl.BlockSpec((1, TM, D_MODEL), lambda b, r: (b, r, 0)),
                  _mod_spec(layer),
                  pl.BlockSpec((D_MODEL, n), lambda b, r: (0, 0))],
        out_specs=pl.BlockSpec((1, TM, n), lambda b, r: (b, r, 0)),
        out_shape=jax.ShapeDtypeStruct((bsz, lt, n), out_dtype),
        compiler_params=_params(2),
        name="mod_proj",
    )(xs, mods, w)


FF_CHUNK = 1024


def _mlp_body(x, mod_ref, w1_ref, w2_ref, g_ref, b_ref, y_ref):
    sh = mod_ref[0, 3:4, :]
    sc = mod_ref[0, 4:5, :]
    gate = mod_ref[0, 5:6, :]
    h = (x * (1.0 + sc) + sh).astype(BF16)
    acc = jnp.zeros((TM, D_MODEL), F32)
    for j in range(D_FF // FF_CHUNK):
        a = jnp.dot(h, w1_ref[:, j * FF_CHUNK:(j + 1) * FF_CHUNK], preferred_element_type=F32)
        a = jnp.square(jnp.maximum(a, 0.0)).astype(BF16)
        acc = acc + jnp.dot(a, w2_ref[j * FF_CHUNK:(j + 1) * FF_CHUNK, :], preferred_element_type=F32)
    y_ref[0] = _layer_norm(DEEPNORM_ALPHA * x + gate * acc, g_ref[...], b_ref[...])


def _mlp_kernel(x_ref, mod_ref, w1_ref, w2_ref, g_ref, b_ref, y_ref):
    _mlp_body(x_ref[0], mod_ref, w1_ref, w2_ref, g_ref, b_ref, y_ref)


def _out_mlp_kernel(o_ref, x_ref, mod_ref, wo_ref, g1_ref, b1_ref, w1_ref, w2_ref, g2_ref, b2_ref, y_ref):
    y = jnp.dot(o_ref[0], wo_ref[...], preferred_element_type=F32)
    x1 = _layer_norm(DEEPNORM_ALPHA * x_ref[0] + mod_ref[0, 2:3, :] * y, g1_ref[...], b1_ref[...])
    _mlp_body(x1, mod_ref, w1_ref, w2_ref, g2_ref, b2_ref, y_ref)


def _out_mlp(o, xs, mods, w_out, w1, w2, ln_g, ln_b, layer, latent_only=False):
    bsz, lt, k = o.shape
    skip = 1 if latent_only else 0
    const = lambda shape: pl.BlockSpec(shape, lambda b, r: (0, 0))
    vec = lambda t: t.reshape(1, D_MODEL)
    return pl.pallas_call(
        _out_mlp_kernel,
        grid=(bsz, lt // TM - skip),
        in_specs=[_row_spec(k, skip), _row_spec(D_MODEL, skip), _mod_spec(layer, skip),
                  const((k, D_MODEL)), const((1, D_MODEL)), const((1, D_MODEL)),
                  const((D_MODEL, D_FF)), const((D_FF, D_MODEL)), const((1, D_MODEL)), const((1, D_MODEL))],
        out_specs=_row_spec(D_MODEL),
        out_shape=jax.ShapeDtypeStruct((bsz, lt - skip * TM, D_MODEL), F32),
        compiler_params=_params(2),
        name="out_mlp_ln",
    )(o, xs, mods, w_out, vec(ln_g[0]), vec(ln_b[0]), w1, w2, vec(ln_g[1]), vec(ln_b[1]))


def _mlp(xs, mods, w1, w2, ln_g, ln_b, layer, latent_only=False):
    bsz, rows, _ = xs.shape
    skip = 1 if latent_only else 0
    return pl.pallas_call(
        _mlp_kernel,
        grid=(bsz, rows // TM),
        in_specs=[_row_spec(D_MODEL),
                  _mod_spec(layer, skip),
                  pl.BlockSpec((D_MODEL, D_FF), lambda b, r: (0, 0)),
                  pl.BlockSpec((D_FF, D_MODEL), lambda b, r: (0, 0)),
                  pl.BlockSpec((1, D_MODEL), lambda b, r: (0, 0)),
                  pl.BlockSpec((1, D_MODEL), lambda b, r: (0, 0))],
        out_specs=_row_spec(D_MODEL),
        out_shape=jax.ShapeDtypeStruct((bsz, rows, D_MODEL), F32),
        compiler_params=_params(2),
        name="mlp_ln",
    )(xs, mods, w1, w2, ln_g.reshape(1, D_MODEL), ln_b.reshape(1, D_MODEL))


def _rope_tables(n_latent, head_dim):
    rows = n_latent // GRID_W
    row = jnp.repeat(jnp.arange(rows), GRID_W).astype(F32)
    col = jnp.tile(jnp.arange(GRID_W), rows).astype(F32)
    n_freq = head_dim // 4
    inv_freq = ROPE_THETA ** (-jnp.arange(n_freq, dtype=F32) / n_freq)
    ang = jnp.concatenate([row[:, None] * inv_freq, col[:, None] * inv_freq], -1)
    cos = jnp.repeat(jnp.cos(ang), 2, axis=-1)
    sin = jnp.repeat(jnp.sin(ang), 2, axis=-1)
    sign = jnp.tile(jnp.array([-1.0, 1.0], F32), head_dim // 2)
    sin = sin * sign
    reps = LANE // head_dim
    cos = jnp.tile(cos, (1, reps))
    sin = jnp.tile(sin, (1, reps))
    cos = jnp.concatenate([jnp.ones((CTX_LEN, LANE), F32), cos], 0)
    sin = jnp.concatenate([jnp.zeros((CTX_LEN, LANE), F32), sin], 0)
    return cos, sin


ATTN_KEY_CHUNKS = 3


def _rope(x, cos, sin_signed):
    lane = lax.broadcasted_iota(jnp.int32, x.shape, 1)
    nxt = pltpu.roll(x, LANE - 1, 1)
    prv = pltpu.roll(x, 1, 1)
    swapped = jnp.where(lane % 2 == 0, nxt, prv)
    return x * cos + swapped * sin_signed


def _rms(x, g):
    return x * lax.rsqrt(jnp.mean(x * x, -1, keepdims=True) + 1e-6) * g


def _attn_kernel(q_ref, qn_ref, qnn_ref, k_ref, v_ref, cos_ref, sin_ref, qg_ref, kg_ref, lam_ref, ng_ref, o_ref,
                 kb_ref, vb_ref, s_even_ref, s_odd_ref, qs_ref, *, n_maps, qk_norm, scale, lambda_init):
    g = pl.program_id(2)
    qt = pl.program_id(3)
    lt = kb_ref.shape[0]
    n_qt = lt // TM

    @pl.when((g == 0) & (qt == 0))
    def _prep_kv():
        k = k_ref[0]
        if qk_norm:
            k = _rms(k, kg_ref[...])
        kb_ref[...] = _rope(k, cos_ref[...], sin_ref[...]).astype(BF16)
        vb_ref[...] = v_ref[0].astype(BF16)

    def queries(ref, tile):
        q = ref[0]
        if qk_norm:
            q = _rms(q, qg_ref[...])
        row0 = pl.multiple_of(tile * TM, TM)
        return _rope(q, cos_ref[pl.ds(row0, TM), :], sin_ref[pl.ds(row0, TM), :]) * (scale * math.log2(math.e))

    def scores(qm, nk):
        return lax.dot_general(qm.astype(BF16), kb_ref[0:nk, :], (((1,), (1,)), ((), ())),
                               preferred_element_type=F32)

    def softmax_parts(s):
        m = jnp.max(s, -1, keepdims=True)
        e = jnp.exp2(s - m)
        return e, jnp.sum(e, -1, keepdims=True)

    def all_scores(qr, nk):
        if n_maps == 1:
            return (scores(qr, nk),)
        lane = lax.broadcasted_iota(jnp.int32, qr.shape, 1)
        return (scores(jnp.where(lane < LANE // 2, qr, 0.0), nk), scores(jnp.where(lane >= LANE // 2, qr, 0.0), nk))

    def weighted_values(s, nk):
        vb = vb_ref[0:nk, :]
        if n_maps == 1:
            e, l = softmax_parts(s[0])
            return jnp.dot(e.astype(BF16), vb, preferred_element_type=F32) * (1.0 / l)
        lp = lam_ref[...]
        lam = (jnp.exp(jnp.sum(lp[0:1, :] * lp[1:2, :])) - jnp.exp(jnp.sum(lp[2:3, :] * lp[3:4, :]))
               + lambda_init)
        e0, l0 = softmax_parts(s[0])
        e1, l1 = softmax_parts(s[1])
        o = (jnp.dot(e0.astype(BF16), vb, preferred_element_type=F32) * (1.0 / l0)
             - jnp.dot(e1.astype(BF16), vb, preferred_element_type=F32) * (lam / l1))
        return _rms(o, ng_ref[...]) * (1.0 - lambda_init)

    def store_scores(ref, s):
        for i in range(n_maps):
            ref[i] = s[i]

    @pl.when(qt == 0)
    def _first():
        o_ref[0] = weighted_values(all_scores(queries(q_ref, 0), CTX_LEN), CTX_LEN).astype(o_ref.dtype)
        store_scores(s_odd_ref, all_scores(queries(qn_ref, 1), lt))
        qs_ref[...] = queries(qnn_ref, jnp.minimum(2, n_qt - 1)).astype(BF16)

    def steady(cur_ref, nxt_ref):
        ch = lt // ATTN_KEY_CHUNKS
        m = [jnp.max(cur_ref[i], -1, keepdims=True) for i in range(n_maps)]
        qn = qs_ref[...]
        if n_maps == 1:
            qn_maps = (qn,)
        else:
            lane = lax.broadcasted_iota(jnp.int32, qn.shape, 1)
            zero = jnp.zeros_like(qn)
            qn_maps = (jnp.where(lane < LANE // 2, qn, zero), jnp.where(lane >= LANE // 2, qn, zero))
        l = [jnp.zeros((TM, 1), F32) for _ in range(n_maps)]
        acc = [jnp.zeros((TM, LANE), F32) for _ in range(n_maps)]
        for j in range(ATTN_KEY_CHUNKS):
            cols = slice(j * ch, (j + 1) * ch)
            for i in range(n_maps):
                nxt_ref[i, :, cols] = lax.dot_general(qn_maps[i], kb_ref[cols, :], (((1,), (1,)), ((), ())),
                                                      preferred_element_type=F32)
            for i in range(n_maps):
                e = jnp.exp2(cur_ref[i, :, cols] - m[i])
                l[i] = l[i] + jnp.sum(e, -1, keepdims=True)
                acc[i] = acc[i] + jnp.dot(e.astype(BF16), vb_ref[cols, :], preferred_element_type=F32)
            if j == 0:
                qs_ref[...] = queries(qnn_ref, jnp.minimum(qt + 2, n_qt - 1)).astype(BF16)
        if n_maps == 1:
            o = acc[0] * (1.0 / l[0])
        else:
            lp = lam_ref[...]
            lam = (jnp.exp(jnp.sum(lp[0:1, :] * lp[1:2, :])) - jnp.exp(jnp.sum(lp[2:3, :] * lp[3:4, :]))
                   + lambda_init)
            o = _rms(acc[0] * (1.0 / l[0]) - acc[1] * (lam / l[1]), ng_ref[...]) * (1.0 - lambda_init)
        o_ref[0] = o.astype(o_ref.dtype)

    for parity, (cur_ref, nxt_ref) in enumerate(((s_even_ref, s_odd_ref), (s_odd_ref, s_even_ref))):
        @pl.when((qt > 0) & (qt % 2 == parity))
        def _steady(cur_ref=cur_ref, nxt_ref=nxt_ref):
            steady(cur_ref, nxt_ref)


def _attention(p, cos, sin, q_gain, k_gain, lam_p, norm_g, *, n_kv, group, q_col, k_col, v_col,
               n_maps, qk_norm, scale, lambda_init):
    bsz, lt, _ = p.shape
    n_qt = lt // TM
    kernel = functools.partial(_attn_kernel, n_maps=n_maps, qk_norm=qk_norm, scale=scale,
                               lambda_init=lambda_init)
    const = lambda b, kv, g, qt: (0, 0)
    return pl.pallas_call(
        kernel,
        grid=(bsz, n_kv, group, n_qt),
        in_specs=[pl.BlockSpec((1, TM, LANE), lambda b, kv, g, qt: (b, qt, q_col + kv * group + g)),
                  pl.BlockSpec((1, TM, LANE),
                               lambda b, kv, g, qt: (b, jnp.minimum(qt + 1, n_qt - 1), q_col + kv * group + g)),
                  pl.BlockSpec((1, TM, LANE),
                               lambda b, kv, g, qt: (b, jnp.minimum(qt + 2, n_qt - 1), q_col + kv * group + g)),
                  pl.BlockSpec((1, lt, LANE), lambda b, kv, g, qt: (b, 0, k_col + kv)),
                  pl.BlockSpec((1, lt, LANE), lambda b, kv, g, qt: (b, 0, v_col + kv)),
                  pl.BlockSpec((lt, LANE), const),
                  pl.BlockSpec((lt, LANE), const),
                  pl.BlockSpec((1, LANE), const),
                  pl.BlockSpec((1, LANE), const),
                  pl.BlockSpec(lam_p.shape, const),
                  pl.BlockSpec((1, LANE), const)],
        out_specs=pl.BlockSpec((1, TM, LANE), lambda b, kv, g, qt: (b, qt, kv * group + g)),
        out_shape=jax.ShapeDtypeStruct((bsz, lt, n_kv * group * LANE), BF16),
        scratch_shapes=[pltpu.VMEM((lt, LANE), BF16), pltpu.VMEM((lt, LANE), BF16),
                        pltpu.VMEM((n_maps, TM, lt), F32), pltpu.VMEM((n_maps, TM, lt), F32),
                        pltpu.VMEM((TM, LANE), BF16)],
        compiler_params=_params(4),
        name="attention",
    )(p, p, p, p, p, cos, sin, q_gain.reshape(1, LANE), k_gain.reshape(1, LANE), lam_p, norm_g.reshape(1, LANE))


def _s5_out_kernel(y_ref, x_ref, mod_ref, d_ref, w_ref, g_ref, b_ref, o_ref):
    sh = mod_ref[0, 0:1, :]
    sc = mod_ref[0, 1:2, :]
    gate = mod_ref[0, 2:3, :]
    x = x_ref[0]
    u = x * (1.0 + sc) + sh
    y = jax.nn.gelu(y_ref[0] + d_ref[...] * u).astype(BF16)
    z = jnp.dot(y, w_ref[...], preferred_element_type=F32)
    out = z[:, :D_MODEL] * jax.nn.sigmoid(z[:, D_MODEL:])
    o_ref[0] = _layer_norm(DEEPNORM_ALPHA * x + gate * out, g_ref[...], b_ref[...])


def _s5_out(y, xs, mods, d_skip, w_glu, ln_g, ln_b, layer, latent_only=False):
    bsz, lt, _ = xs.shape
    skip = 1 if latent_only else 0
    return pl.pallas_call(
        _s5_out_kernel,
        grid=(bsz, lt // TM - skip),
        in_specs=[_row_spec(D_MODEL, skip),
                  _row_spec(D_MODEL, skip),
                  _mod_spec(layer, skip),
                  pl.BlockSpec((1, D_MODEL), lambda b, r: (0, 0)),
                  pl.BlockSpec((D_MODEL, 2 * D_MODEL), lambda b, r: (0, 0)),
                  pl.BlockSpec((1, D_MODEL), lambda b, r: (0, 0)),
                  pl.BlockSpec((1, D_MODEL), lambda b, r: (0, 0))],
        out_specs=_row_spec(D_MODEL),
        out_shape=jax.ShapeDtypeStruct((bsz, lt - skip * TM, D_MODEL), F32),
        compiler_params=_params(2),
        name="s5_out_ln",
    )(y, xs, mods, d_skip.reshape(1, D_MODEL), w_glu, ln_g.reshape(1, D_MODEL), ln_b.reshape(1, D_MODEL))


DN_BLK = 128
DN_PREP_BLOCKS = 3
DN_GATE_BETA_F, DN_GATE_BETA_B, DN_GATE_A_F, DN_GATE_A_B = 0, DN_HEADS, 2 * DN_HEADS, 3 * DN_HEADS
_NT = (((1,), (1,)), ((), ()))
_TN = (((0,), (0,)), ((), ()))


def _dn_kernel(q_ref, k_ref, v_ref, z_ref, g_ref, cq_ref, ck_ref, cv_ref, alog_ref, dtb_ref, ng_ref, o_ref,
               beta_ref, gc_ref, tot_ref, gct_ref, qn_ref, kn_ref, vn_ref,
               wq_f, u_f, qk_f, kd_f, gl_f, o_f, wq_b, u_b, qk_b, kd_b, gl_b, o_b, *, n_blk, n_ctx_blk):
    h = pl.program_id(1)
    lt = n_blk * DN_BLK
    ctx_rows = n_ctx_blk * DN_BLK
    row = lax.broadcasted_iota(jnp.int32, (lt, 1), 0)
    lane = lax.broadcasted_iota(jnp.int32, (1, LANE), 1)

    @pl.when(h == 0)
    def _gates():
        gts = g_ref[0]
        beta_ref[...] = jax.nn.sigmoid(gts)
        g = -jnp.exp(alog_ref[...]) * jax.nn.softplus(gts + dtb_ref[...])
        pos = row % DN_BLK
        pre = g
        suf = g
        s = 1
        while s < DN_BLK:
            pre = pre + jnp.where(pos >= s, pltpu.roll(pre, s, 0), 0.0)
            suf = suf + jnp.where(pos < DN_BLK - s, pltpu.roll(suf, lt - s, 0), 0.0)
            s *= 2
        gc = jnp.where(lane >= DN_GATE_A_B, suf, pre)
        gc_ref[...] = gc
        tot_ref[...] = pre + suf - g
        for c in range(n_blk):
            gct_ref[c] = gc[c * DN_BLK:(c + 1) * DN_BLK, :].T

    def conv_silu(x_ref, w_ref):
        x = x_ref[0]
        w = w_ref[...]
        acc = x * w[DN_CONV // 2:DN_CONV // 2 + 1, :]
        for d in range(-(DN_CONV // 2), DN_CONV // 2 + 1):
            if d == 0:
                continue
            src = row + d
            ok = (src >= 0) & (src < lt) & ((src < ctx_rows) == (row < ctx_rows))
            acc = acc + jnp.where(ok, pltpu.roll(x, (-d) % lt, 0), 0.0) * w[d + DN_CONV // 2:d + DN_CONV // 2 + 1, :]
        return acc * jax.nn.sigmoid(acc)

    def l2n(x):
        return x * lax.rsqrt(jnp.sum(x * x, -1, keepdims=True) + 1e-6)

    qn_ref[...] = l2n(conv_silu(q_ref, cq_ref)) * DN_HEAD_DIM ** -0.5
    kn_ref[...] = l2n(conv_silu(k_ref, ck_ref))
    vn_ref[...] = conv_silu(v_ref, cv_ref)

    ii = lax.broadcasted_iota(jnp.int32, (DN_BLK, DN_BLK), 0)
    jj = lax.broadcasted_iota(jnp.int32, (DN_BLK, DN_BLK), 1)
    fwd =(DN_GATE_BETA_F, DN_GATE_A_F, ii >= jj, ii > jj, (wq_f, u_f, qk_f, kd_f, gl_f, o_f))
    bwd = (DN_GATE_BETA_B, DN_GATE_A_B, ii <= jj, ii < jj, (wq_b, u_b, qk_b, kd_b, gl_b, o_b))

    ii2 = lax.broadcasted_iota(jnp.int32, (2 * DN_BLK, 2 * DN_BLK), 0)
    jj2 = lax.broadcasted_iota(jnp.int32, (2 * DN_BLK, 2 * DN_BLK), 1)
    eye2 = (ii2 == jj2).astype(F32)
    zero_blk = jnp.zeros((DN_BLK, DN_BLK), F32)

    def pair_off(s):
        return ((ii2 // (2 * s)) == (jj2 // (2 * s))) & ((ii2 // s) != (jj2 // s))

    def column(ref, rows, lane_idx):
        return jnp.sum(jnp.where(lane == lane_idx, ref[rows, :], 0.0), -1, keepdims=True)

    def block_inputs(c):
        rows = pl.ds(pl.multiple_of(c * DN_BLK, DN_BLK), DN_BLK)
        qc, kc, vc = qn_ref[rows, :], kn_ref[rows, :], vn_ref[rows, :]
        kb = kc.astype(BF16)
        kk = lax.dot_general(kb, kb, _NT, preferred_element_type=F32)
        qk = lax.dot_general(qc.astype(BF16), kb, _NT, preferred_element_type=F32)
        a_dir, rhs_dir = [], []
        for lane_beta, lane_g, incl, strict, (wq_ref, _, qk_ref, kd_ref, gl_ref, _) in (fwd, bwd):
            bcol = column(beta_ref, rows, lane_beta + h)
            gcol = column(gc_ref, rows, lane_g + h)
            tcol = column(tot_ref, rows, lane_g + h)
            grow = gct_ref[c, pl.ds(lane_g + h, 1), :]
            dec = jnp.exp(jnp.where(incl, gcol - grow, -jnp.inf))
            a_dir.append(jnp.where(strict, bcol * kk * dec, 0.0))
            egc = jnp.exp(gcol)
            rhs_dir.append(jnp.concatenate([bcol * vc, (bcol * egc) * kc], -1).astype(BF16))
            wq_ref[c, DN_BLK:2 * DN_BLK, :] = (qc * egc).astype(BF16)
            qk_ref[c] = (qk * dec).astype(BF16)
            kd_ref[c] = (kc * jnp.exp(tcol - gcol)).T.astype(BF16)
            gl_ref[c] = jnp.broadcast_to(jnp.exp(tcol[0:1, :]), (8, LANE))
        a = jnp.concatenate([jnp.concatenate([a_dir[0], zero_blk], 1),
                             jnp.concatenate([zero_blk, a_dir[1]], 1)], 0)
        return a, jnp.concatenate(rhs_dir, 0)

    def prepare(it, _):
        blocks = [it * DN_PREP_BLOCKS + i for i in range(DN_PREP_BLOCKS)]
        a_rhs = [block_inputs(c) for c in blocks]
        t_inv, m = [], []
        for a, _ in a_rhs:
            a_off = jnp.where(pair_off(1), a, 0.0)
            t_inv.append(eye2 - a_off)
            neighbour = jnp.where(ii2 < DN_BLK, pltpu.roll(a, 1, 0), pltpu.roll(a, 2 * DN_BLK - 1, 0))
            m.append(a - jnp.sum(a_off, -1, keepdims=True) * neighbour)
        s = 2
        while s < DN_BLK:
            for i in range(DN_PREP_BLOCKS):
                m_off = jnp.where(pair_off(s), m[i], 0.0).astype(BF16)
                if 2 * s < DN_BLK:
                    upd = jnp.dot(m_off, jnp.concatenate([t_inv[i], m[i]], 1).astype(BF16),
                                  preferred_element_type=F32)
                    t_inv[i] = t_inv[i] - upd[:, :2 * DN_BLK]
                    m[i] = m[i] - upd[:, 2 * DN_BLK:]
                else:
                    t_inv[i] = t_inv[i] - jnp.dot(m_off, t_inv[i].astype(BF16), preferred_element_type=F32)
            s *= 2
        for i, c in enumerate(blocks):
            uw = jnp.dot(t_inv[i].astype(BF16), a_rhs[i][1], preferred_element_type=F32)
            for d, (wq_ref, u_ref) in enumerate(((wq_f, u_f), (wq_b, u_b))):
                u_ref[c] = uw[d * DN_BLK:(d + 1) * DN_BLK, :DN_HEAD_DIM]
                wq_ref[c, 0:DN_BLK, :] = uw[d * DN_BLK:(d + 1) * DN_BLK, DN_HEAD_DIM:].astype(BF16)
        return 0

    lax.fori_loop(0, n_blk // DN_PREP_BLOCKS, prepare, 0)

    zero_bf = jnp.zeros((DN_BLK, DN_BLK), BF16)

    def block_diag(top, bottom):
        return jnp.concatenate([jnp.concatenate([top, zero_bf], 1), jnp.concatenate([zero_bf, bottom], 1)], 0)

    def step(k, carry):
        s_f, s_b = carry
        cb = jnp.where(k < n_ctx_blk, n_ctx_blk - 1 - k, n_blk + n_ctx_blk - 1 - k)
        hd = DN_HEAD_DIM
        r = jnp.dot(jnp.concatenate([wq_f[k], wq_b[cb]], 1), block_diag(s_f.astype(BF16), s_b.astype(BF16)),
                    preferred_element_type=F32)
        v_new = block_diag((u_f[k] - r[:DN_BLK, :hd]).astype(BF16), (u_b[cb] - r[:DN_BLK, hd:]).astype(BF16))
        lhs = jnp.concatenate([jnp.concatenate([qk_f[k], qk_b[cb]], 1),
                               jnp.concatenate([kd_f[k], kd_b[cb]], 1)], 0)
        r2 = jnp.dot(lhs, v_new, preferred_element_type=F32)
        o_f[k] = r[DN_BLK:, :hd] + r2[:DN_BLK, :hd]
        o_b[cb] = r[DN_BLK:, hd:] + r2[:DN_BLK, hd:]
        return (gl_f[k][0:1, :] * s_f + r2[DN_BLK:, :hd], gl_b[cb][0:1, :] * s_b + r2[DN_BLK:, hd:])

    zero = jnp.zeros((DN_HEAD_DIM, DN_HEAD_DIM), F32)
    lax.fori_loop(0, n_blk, step, (zero, zero))

    o = (o_f[...] + o_b[...]).reshape(lt, DN_HEAD_DIM)
    o = o * lax.rsqrt(jnp.mean(o * o, -1, keepdims=True) + 1e-6) * ng_ref[...]
    z = z_ref[0]
    o_ref[0] = (o * (z * jax.nn.sigmoid(z))).astype(o_ref.dtype)


def _deltanet_core(p, conv_w, a_log, dt_bias, norm_g):
    bsz, lt, _ = p.shape
    n_blk = lt // DN_BLK
    gate_pad = lambda t: jnp.pad(t.reshape(1, 2 * DN_HEADS), ((0, 0), (DN_GATE_A_F, LANE - DN_GATE_A_F - 2 * DN_HEADS)))
    kernel = functools.partial(_dn_kernel, n_blk=n_blk, n_ctx_blk=CTX_LEN // DN_BLK)
    col = lambda off: pl.BlockSpec((1, lt, LANE), lambda b, h: (b, 0, off + h))
    cw = lambda off: pl.BlockSpec((DN_CONV, LANE), lambda b, h: (0, off + h))
    const = pl.BlockSpec((1, LANE), lambda b, h: (0, 0))
    seq = lambda dt: pltpu.VMEM((lt, LANE), dt)
    blk = lambda rows, dt: pltpu.VMEM((n_blk, rows, LANE), dt)
    per_dir = [blk(2 * DN_BLK, BF16), blk(DN_BLK, F32), blk(DN_BLK, BF16), blk(DN_BLK, BF16), blk(8, F32),
               blk(DN_BLK, F32)]
    return pl.pallas_call(
        kernel,
        grid=(bsz, DN_HEADS),
        in_specs=[col(0), col(DN_HEADS), col(2 * DN_HEADS), col(3 * DN_HEADS),
                  pl.BlockSpec((1, lt, LANE), lambda b, h: (b, 0, 4 * DN_HEADS)),
                  cw(0), cw(DN_HEADS), cw(2 * DN_HEADS), const, const, const],
        out_specs=pl.BlockSpec((1, lt, LANE), lambda b, h: (b, 0, h)),
        out_shape=jax.ShapeDtypeStruct((bsz, lt, DN_WIDTH), BF16),
        scratch_shapes=[seq(F32), seq(F32), seq(F32), blk(LANE, F32), seq(F32), seq(F32), seq(F32)] + per_dir + per_dir,
        compiler_params=_params(2),
        name="deltanet_core",
    )(p, p, p, p, p, conv_w, conv_w, conv_w, gate_pad(a_log), gate_pad(dt_bias), norm_g.reshape(1, LANE))


S5_T = 8
S5_BB = 4
S5_BLK_GROUPS = LANE // SS_GROUP
S5_HALF = S5_BLK_GROUPS * SS_STATE
S5_K = S5_T * LANE


def _s5_weights(a_re, a_im, log_dt, b_re, b_im, c_re, c_im):
    hp = lax.Precision.HIGHEST
    t = S5_T
    nj = SS_GROUPS // S5_BLK_GROUPS
    ar = jnp.minimum(a_re, -1e-4)
    ai = a_im
    dt = jnp.exp(log_dt)[..., None]
    zr, zi = ar * dt, ai * dt
    d = jnp.arange(t + 1, dtype=F32)[:, None, None, None]
    mag = jnp.exp(d * zr)
    er, ei = mag * jnp.cos(d * zi), mag * jnp.sin(d * zi)
    nr, ni = er[1] - 1.0, ei[1]
    den = ar * ar + ai * ai
    fr, fi = (nr * ar + ni * ai) / den, (ni * ar - nr * ai) / den
    bbr = fr[..., None] * b_re - fi[..., None] * b_im
    bbi = fr[..., None] * b_im + fi[..., None] * b_re
    cer = c_re[None] * er[:, :, :, None, :] - c_im[None] * ei[:, :, :, None, :]
    cei = c_re[None] * ei[:, :, :, None, :] + c_im[None] * er[:, :, :, None, :]
    kk = (jnp.einsum('tdgip,dgpj->tdgij', cer, bbr, precision=hp)
          - jnp.einsum('tdgip,dgpj->tdgij', cei, bbi, precision=hp))
    s_idx = jnp.arange(t)[:, None]
    t_idx = jnp.arange(t)[None, :]
    lag_f = jnp.clip(t_idx - s_idx, 0, t)
    lag_b = jnp.clip(s_idx - t_idx, 0, t)
    m_f = (t_idx >= s_idx)[:, :, None, None, None]
    m_b = (s_idx >= t_idx)[:, :, None, None, None]
    kst = jnp.where(m_f, kk[lag_f, 0], 0.0) + jnp.where(m_b, kk[lag_b, 1], 0.0)
    kst = kst.reshape(t, t, nj, S5_BLK_GROUPS, SS_GROUP, SS_GROUP).transpose(2, 0, 3, 5, 1, 4)
    m_intra = kst.reshape(nj, S5_K, t * SS_GROUP)

    pow_f = (t - 1 - jnp.arange(t))
    pow_b = jnp.arange(t)

    def carry_w(pw, dd):
        e_r, e_i = er[pw, dd], ei[pw, dd]
        re = e_r[..., None] * bbr[dd][None] - e_i[..., None] * bbi[dd][None]
        im = e_r[..., None] * bbi[dd][None] + e_i[..., None] * bbr[dd][None]
        return jnp.stack([re, im], 0)

    wb = jnp.stack([carry_w(pow_f, 0), carry_w(pow_b, 1)], 0)
    wb = wb.reshape(2, 2, t, nj, S5_BLK_GROUPS, SS_STATE, SS_GROUP)
    m_carry = wb.transpose(3, 2, 4, 6, 0, 1, 5).reshape(nj, S5_K, 4 * SS_STATE)

    def read_w(pw, dd):
        wc = jnp.stack([cer[pw, dd], -cei[pw, dd]], 0)
        wc = wc.reshape(2, t, nj, S5_BLK_GROUPS, SS_GROUP, SS_STATE)
        return wc.transpose(2, 0, 3, 5, 1, 4).reshape(nj, 2 * S5_HALF, t * SS_GROUP)

    m_read_f = read_w(jnp.arange(t) + 1, 0)
    m_read_b = read_w(t - jnp.arange(t), 1)
    lam_t = jnp.stack([er[t, 0], ei[t, 0], er[t, 1], ei[t, 1]], 0)
    lam_t = lam_t.reshape(4, nj, S5_HALF).transpose(1, 0, 2)
    return m_carry.astype(BF16), m_intra.astype(BF16), m_read_f.astype(BF16), m_read_b.astype(BF16), lam_t


def _s5_expand(m, row_group_div, col_seg, n_col):
    rows, n_in = m.shape
    q = lax.broadcasted_iota(jnp.int32, (n_in, n_col), 0)
    c = lax.broadcasted_iota(jnp.int32, (n_in, n_col), 1)
    out_seg = col_seg * S5_BLK_GROUPS
    rep = ((q // col_seg == c // out_seg) & (q % col_seg == c % col_seg)).astype(BF16)
    wide = jnp.dot(m, rep, preferred_element_type=F32)
    g_row = (lax.broadcasted_iota(jnp.int32, (rows, n_col), 0) // row_group_div) % S5_BLK_GROUPS
    g_col = (lax.broadcasted_iota(jnp.int32, (rows, n_col), 1) // col_seg) % S5_BLK_GROUPS
    return jnp.where(g_row == g_col, wide, 0.0).astype(BF16)


def _s5_kernel(x_ref, mod_ref, mc_ref, mi_ref, mrf_ref, mrb_ref, lt_ref, y_ref,
               wb_ref, wi_ref, wcf_ref, wcb_ref, xcat_ref, hf_ref, hb_ref, *, n_chunk, n_ctx_chunk):
    bh = pl.program_id(1)

    @pl.when(bh == 0)
    def _expand_weights():
        wb_ref[...] = _s5_expand(mc_ref[0], SS_GROUP, SS_STATE, 4 * S5_HALF)
        wi_ref[...] = _s5_expand(mi_ref[0], SS_GROUP, SS_GROUP, S5_K)
        wcf_ref[...] = _s5_expand(mrf_ref[0], SS_STATE, SS_GROUP, S5_K)
        wcb_ref[...] = _s5_expand(mrb_ref[0], SS_STATE, SS_GROUP, S5_K)

    nb = x_ref.shape[0]
    n_lb = hf_ref.shape[0]
    half_lb = n_lb // 2
    is_ctx = lax.broadcasted_iota(jnp.int32, (n_chunk, 1), 0) < n_ctx_chunk
    sh_c = mod_ref[8, 0:1, :]
    sc_c = mod_ref[8, 1:2, :]

    def carry_in(bl, _):
        mb = mod_ref[bh * nb + bl]
        sh = jnp.where(is_ctx, sh_c, mb[0:1, :])
        sc1 = 1.0 + jnp.where(is_ctx, sc_c, mb[1:2, :])
        rows = pl.ds(pl.multiple_of(bl * n_chunk, n_chunk), n_chunk)
        for tau in range(S5_T):
            xt = x_ref[bl, pl.ds(tau, n_chunk, stride=S5_T), :]
            xcat_ref[rows, tau * LANE:(tau + 1) * LANE] = (xt * sc1 + sh).astype(BF16)
        hbv = jnp.dot(xcat_ref[rows, :], wb_ref[...], preferred_element_type=F32)
        for l in range(n_lb):
            hf_ref[l, rows, :] = hbv[:, l * LANE:(l + 1) * LANE]
            hb_ref[l, rows, :] = hbv[:, (n_lb + l) * LANE:(n_lb + l + 1) * LANE]
        return 0

    lax.fori_loop(0, nb, carry_in, 0)

    lt = lt_ref[0]

    def lam_blocks(i):
        return [jnp.broadcast_to(lt[i:i + 1, l * LANE:(l + 1) * LANE], (nb, LANE)) for l in range(half_lb)]

    a_f = (lam_blocks(0), lam_blocks(1))
    a_b = (lam_blocks(2), lam_blocks(3))

    def advance(h_ref, c, a, state):
        a_re, a_im = a
        new = []
        for l in range(half_lb):
            sel = pl.ds(c, nb, stride=n_chunk)
            in_re = h_ref[l, sel, :]
            in_im = h_ref[half_lb + l, sel, :]
            s_re, s_im = state[l], state[half_lb + l]
            h_ref[l, sel, :] = s_re
            h_ref[half_lb + l, sel, :] = s_im
            new.append((a_re[l] * s_re - a_im[l] * s_im + in_re, a_re[l] * s_im + a_im[l] * s_re + in_im))
        return tuple(n[0] for n in new) + tuple(n[1] for n in new)

    def step(k, carry):
        st_f, st_b = carry
        cb = jnp.where(k < n_ctx_chunk, n_ctx_chunk - 1 - k, n_chunk + n_ctx_chunk - 1 - k)
        return advance(hf_ref, k, a_f, st_f), advance(hb_ref, cb, a_b, st_b)

    zero = tuple(jnp.zeros((nb, LANE), F32) for _ in range(n_lb))
    lax.fori_loop(0, n_chunk, step, (zero, zero))

    def read_out(bl, _):
        rows = pl.ds(pl.multiple_of(bl * n_chunk, n_chunk), n_chunk)
        h_f = jnp.concatenate([hf_ref[l, rows, :] for l in range(n_lb)], -1).astype(BF16)
        h_b = jnp.concatenate([hb_ref[l, rows, :] for l in range(n_lb)], -1).astype(BF16)
        y = (jnp.dot(xcat_ref[rows, :], wi_ref[...], preferred_element_type=F32)
             + jnp.dot(h_f, wcf_ref[...], preferred_element_type=F32)
             + jnp.dot(h_b, wcb_ref[...], preferred_element_type=F32))
        for t in range(S5_T):
            y_ref[bl, pl.ds(t, n_chunk, stride=S5_T), :] = y[:, t * LANE:(t + 1) * LANE]
        return 0

    lax.fori_loop(0, nb, read_out, 0)


def _s5_core(xs, mods, weights, layer):
    bsz, lt, _ = xs.shape
    m_carry, m_intra, m_read_f, m_read_b, lam_t = weights
    nj = D_MODEL // LANE
    nb = min(S5_BB, bsz)
    n_chunk = lt // S5_T
    kernel = functools.partial(_s5_kernel, n_chunk=n_chunk, n_ctx_chunk=CTX_LEN // S5_T)
    wspec = lambda shape: pl.BlockSpec((1,) + shape, lambda j, bh: (j, 0, 0))
    return pl.pallas_call(
        kernel,
        grid=(nj, bsz // nb),
        in_specs=[pl.BlockSpec((nb, lt, LANE), lambda j, bh: (bh, 0, j)),
                  pl.BlockSpec((MOD_ROWS, 6, LANE), lambda j, bh: (layer, 0, j)),
                  wspec(m_carry.shape[1:]), wspec(m_intra.shape[1:]),
                  wspec(m_read_f.shape[1:]), wspec(m_read_b.shape[1:]),
                  wspec((4, S5_HALF))],
        out_specs=pl.BlockSpec((nb, lt, LANE), lambda j, bh: (bh, 0, j)),
        out_shape=jax.ShapeDtypeStruct((bsz, lt, D_MODEL), F32),
        scratch_shapes=[pltpu.VMEM((S5_K, 4 * S5_HALF), BF16), pltpu.VMEM((S5_K, S5_K), BF16),
                        pltpu.VMEM((2 * S5_HALF, S5_K), BF16), pltpu.VMEM((2 * S5_HALF, S5_K), BF16),
                        pltpu.VMEM((nb * n_chunk, S5_K), BF16),
                        pltpu.VMEM((2 * S5_HALF // LANE, nb * n_chunk, LANE), F32),
                        pltpu.VMEM((2 * S5_HALF // LANE, nb * n_chunk, LANE), F32)],
        compiler_params=_params(2),
        name="s5_core",
    )(xs, mods, m_carry, m_intra, m_read_f, m_read_b, lam_t)


def kernel(x, c, ctx, c_ctx, ada_w, ada_b, ln_g, ln_b, mlp_w1, mlp_w2, dn_w_in, dn_conv, dn_a_log, dn_dt_bias, dn_norm_g, dn_w_out, da_w_qkv, da_lambda, da_norm_g, da_w_out, ga_w_qkv, ga_q_norm, ga_k_norm, ga_w_out, ss_a_re, ss_a_im, ss_log_dt, ss_b_re, ss_b_im, ss_c_re, ss_c_im, ss_d, ss_w_glu):
    bsz, n_latent, _ = x.shape
    xs = jnp.concatenate([ctx, x], 1)
    c_rows = jnp.concatenate([c, c_ctx[None, :], jnp.zeros((MOD_ROWS - bsz - 1, D_MODEL), F32)], 0)
    mods = _ada_all(c_rows, ada_w, ada_b).reshape(DEPTH * MOD_ROWS, 6, D_MODEL)
    ones = jnp.ones((LANE,), F32)
    zeros4 = jnp.zeros((4, DA_HEAD_DIM), F32)

    for i in range(DEPTH):
        m, j = i % 4, i // 4
        last = i == DEPTH - 1
        if m == 0:
            w_in = jnp.pad(dn_w_in[j], ((0, 0), (0, LANE - 4 * DN_HEADS))).astype(BF16)
            p = _proj(xs, mods, w_in, i)
            o = _deltanet_core(p, dn_conv[j], dn_a_log[j], dn_dt_bias[j], dn_norm_g[j])
            w_out = dn_w_out[j]
        elif m == 1:
            lambda_init = 0.8 - 0.6 * math.exp(-0.3 * i)
            p = _proj(xs, mods, da_w_qkv[j].astype(BF16), i)
            cos, sin = _rope_tables(n_latent, DA_HEAD_DIM)
            o = _attention(p, cos, sin, ones, ones, da_lambda[j], da_norm_g[j],
                           n_kv=DA_HEADS, group=1, q_col=0, k_col=DA_HEADS, v_col=2 * DA_HEADS,
                           n_maps=2, qk_norm=False, scale=DA_HEAD_DIM ** -0.5, lambda_init=lambda_init)
            w_out = da_w_out[j]
        elif m == 2:
            p = _proj(xs, mods, ga_w_qkv[j].astype(BF16), i)
            cos, sin = _rope_tables(n_latent, GA_HEAD_DIM)
            o = _attention(p, cos, sin, ga_q_norm[j], ga_k_norm[j], zeros4, ones,
                           n_kv=GA_KV_HEADS, group=GA_HEADS // GA_KV_HEADS, q_col=0, k_col=GA_HEADS,
                           v_col=GA_HEADS + GA_KV_HEADS, n_maps=1, qk_norm=True,
                           scale=GA_HEAD_DIM ** -0.5, lambda_init=0.0)
            w_out = ga_w_out[j]
        w1, w2 = mlp_w1[i].astype(BF16), mlp_w2[i].astype(BF16)
        if m == 3:
            weights = _s5_weights(ss_a_re[j], ss_a_im[j], ss_log_dt[j], ss_b_re[j], ss_b_im[j],
                                  ss_c_re[j], ss_c_im[j])
            y = _s5_core(xs, mods, weights, i)
            xs = _s5_out(y, xs, mods, ss_d[j], ss_w_glu[j].astype(BF16), ln_g[i, 0], ln_b[i, 0], i, last)
            xs = _mlp(xs, mods, w1, w2, ln_g[i, 1], ln_b[i, 1], i, last)
        else:
            xs = _out_mlp(o, xs, mods, w_out.astype(BF16), w1, w2, ln_g[i], ln_b[i], i, last)
    return xs
```

```python
import functools
import math

import jax
import jax.numpy as jnp
from jax import lax
from jax.experimental import pallas as pl
from jax.experimental.pallas import tpu as pltpu

F32 = jnp.float32
BF16 = jnp.bfloat16

D_MODEL = 1024
D_FF = 4 * D_MODEL
DEPTH = 4
GRID_W = 64
CTX_LEN = 256
ROPE_THETA = 10000.0
DEEPNORM_ALPHA = (2 * DEPTH) ** 0.25
TM = 256
LANE = 128
MOD_ROWS = 16
VMEM_LIMIT = 56 * 1024 * 1024

DN_HEADS = 8
DN_HEAD_DIM = 128
DN_WIDTH = DN_HEADS * DN_HEAD_DIM
DN_CONV = 5
DN_CHUNK = 64
DA_HEADS = 8
DA_HEAD_DIM = 64
GA_HEADS = 8
GA_KV_HEADS = 2
GA_HEAD_DIM = 128
SS_GROUP = 16
SS_GROUPS = D_MODEL // SS_GROUP
SS_STATE = 64


def _params(n_axes):
    return pltpu.CompilerParams(dimension_semantics=("arbitrary",) * n_axes,
                                vmem_limit_bytes=VMEM_LIMIT)


def _layer_norm(v, g, b):
    mu = jnp.mean(v, -1, keepdims=True)
    d = v - mu
    var = jnp.mean(d * d, -1, keepdims=True)
    return d * lax.rsqrt(var + 1e-5) * g + b


def _mod_spec(layer, skip=0):
    return pl.BlockSpec((1, 6, D_MODEL),
                        lambda b, r: (layer * MOD_ROWS + jnp.where(r + skip == 0, 8, b), 0, 0))


def _row_spec(width, skip=0):
    return pl.BlockSpec((1, TM, width), lambda b, r: (b, r + skip, 0))


def _ada_kernel(c_ref, w_ref, b_ref, o_ref):
    c = c_ref[...]
    act = (c * jax.nn.sigmoid(c)).astype(BF16)
    o_ref[0] = jnp.dot(act, w_ref[0].astype(BF16), preferred_element_type=F32) + b_ref[0]


def _ada_all(c_rows, ada_w, ada_b):
    tn = 1536
    n = 6 * D_MODEL
    return pl.pallas_call(
        _ada_kernel,
        grid=(DEPTH, n // tn),
        in_specs=[pl.BlockSpec((MOD_ROWS, D_MODEL), lambda i, j: (0, 0)),
                  pl.BlockSpec((1, D_MODEL, tn), lambda i, j: (i, 0, j)),
                  pl.BlockSpec((1, 1, tn), lambda i, j: (i, 0, j))],
        out_specs=pl.BlockSpec((1, MOD_ROWS, tn), lambda i, j: (i, 0, j)),
        out_shape=jax.ShapeDtypeStruct((DEPTH, MOD_ROWS, n), F32),
        compiler_params=_params(2),
        name="ada_mod",
    )(c_rows, ada_w, ada_b.reshape(DEPTH, 1, n))


def _proj_kernel(x_ref, mod_ref, w_ref, o_ref):
    sh = mod_ref[0, 0:1, :]
    sc = mod_ref[0, 1:2, :]
    h = (x_ref[0] * (1.0 + sc) + sh).astype(BF16)
    o_ref[0] = jnp.dot(h, w_ref[...], preferred_element_type=F32).astype(o_ref.dtype)


def _proj(xs, mods, w, layer, out_dtype=F32):
    bsz, lt, _ = xs.shape
    n = w.shape[1]
    return pl.pallas_call(
        _proj_kernel,
        grid=(bsz, lt // TM),
        in_specs=[pl.BlockSpec((1, TM, D_MODEL), lambda b, r: (b, r, 0)),
                  _mod_spec(layer),
                  pl.BlockSpec((D_MODEL, n), lambda b, r: (0, 0))],
        out_specs=pl.BlockSpec((1, TM, n), lambda b, r: (b, r, 0)),
        out_shape=jax.ShapeDtypeStruct((bsz, lt, n), out_dtype),
        compiler_params=_params(2),
        name="mod_proj",
    )(xs, mods, w)


FF_CHUNK = 1024


def _mlp_body(x, mod_ref, w1_ref, w2_ref, g_ref, b_ref, y_ref):
    sh = mod_ref[0, 3:4, :]
    sc = mod_ref[0, 4:5, :]
    gate = mod_ref[0, 5:6, :]
    h = (x * (1.0 + sc) + sh).astype(BF16)
    acc = jnp.zeros((TM, D_MODEL), F32)
    for j in range(D_FF // FF_CHUNK):
        a = jnp.dot(h, w1_ref[:, j * FF_CHUNK:(j + 1) * FF_CHUNK], preferred_element_type=F32)
        a = jnp.square(jnp.maximum(a, 0.0)).astype(BF16)
        acc = acc + jnp.dot(a, w2_ref[j * FF_CHUNK:(j + 1) * FF_CHUNK, :], preferred_element_type=F32)
    y_ref[0] = _layer_norm(DEEPNORM_ALPHA * x + gate * acc, g_ref[...], b_ref[...])


def _mlp_kernel(x_ref, mod_ref, w1_ref, w2_ref, g_ref, b_ref, y_ref):
    _mlp_body(x_ref[0], mod_ref, w1_ref, w2_ref, g_ref, b_ref, y_ref)


def _out_mlp_kernel(o_ref, x_ref, mod_ref, wo_ref, g1_ref, b1_ref, w1_ref, w2_ref, g2_ref, b2_ref, y_ref):
    y = jnp.dot(o_ref[0], wo_ref[...], preferred_element_type=F32)
    x1 = _layer_norm(DEEPNORM_ALPHA * x_ref[0] + mod_ref[0, 2:3, :] * y, g1_ref[...], b1_ref[...])
    _mlp_body(x1, mod_ref, w1_ref, w2_ref, g2_ref, b2_ref, y_ref)


def _out_mlp(o, xs, mods, w_out, w1, w2, ln_g, ln_b, layer, latent_only=False):
    bsz, lt, k = o.shape
    skip = 1 if latent_only else 0
    const = lambda shape: pl.BlockSpec(shape, lambda b, r: (0, 0))
    vec = lambda t: t.reshape(1, D_MODEL)
    return pl.pallas_call(
        _out_mlp_kernel,
        grid=(bsz, lt // TM - skip),
        in_specs=[_row_spec(k, skip), _row_spec(D_MODEL, skip), _mod_spec(layer, skip),
                  const((k, D_MODEL)), const((1, D_MODEL)), const((1, D_MODEL)),
                  const((D_MODEL, D_FF)), const((D_FF, D_MODEL)), const((1, D_MODEL)), const((1, D_MODEL))],
        out_specs=_row_spec(D_MODEL),
        out_shape=jax.ShapeDtypeStruct((bsz, lt - skip * TM, D_MODEL), F32),
        compiler_params=_params(2),
        name="out_mlp_ln",
    )(o, xs, mods, w_out, vec(ln_g[0]), vec(ln_b[0]), w1, w2, vec(ln_g[1]), vec(ln_b[1]))


def _mlp(xs, mods, w1, w2, ln_g, ln_b, layer, latent_only=False):
    bsz, rows, _ = xs.shape
    skip = 1 if latent_only else 0
    return pl.pallas_call(
        _mlp_kernel,
        grid=(bsz, rows // TM),
        in_specs=[_row_spec(D_MODEL),
                  _mod_spec(layer, skip),
                  pl.BlockSpec((D_MODEL, D_FF), lambda b, r: (0, 0)),
                  pl.BlockSpec((D_FF, D_MODEL), lambda b, r: (0, 0)),
                  pl.BlockSpec((1, D_MODEL), lambda b, r: (0, 0)),
                  pl.BlockSpec((1, D_MODEL), lambda b, r: (0, 0))],
        out_specs=_row_spec(D_MODEL),
        out_shape=jax.ShapeDtypeStruct((bsz, rows, D_MODEL), F32),
        compiler_params=_params(2),
        name="mlp_ln",
    )(xs, mods, w1, w2, ln_g.reshape(1, D_MODEL), ln_b.reshape(1, D_MODEL))


def _rope_tables(n_latent, head_dim):
    rows = n_latent // GRID_W
    row = jnp.repeat(jnp.arange(rows), GRID_W).astype(F32)
    col = jnp.tile(jnp.arange(GRID_W), rows).astype(F32)
    n_freq = head_dim // 4
    inv_freq = ROPE_THETA ** (-jnp.arange(n_freq, dtype=F32) / n_freq)
    ang = jnp.concatenate([row[:, None] * inv_freq, col[:, None] * inv_freq], -1)
    cos = jnp.repeat(jnp.cos(ang), 2, axis=-1)
    sin = jnp.repeat(jnp.sin(ang), 2, axis=-1)
    sign = jnp.tile(jnp.array([-1.0, 1.0], F32), head_dim // 2)
    sin = sin * sign
    reps = LANE // head_dim
    cos = jnp.tile(cos, (1, reps))
    sin = jnp.tile(sin, (1, reps))
    cos = jnp.concatenate([jnp.ones((CTX_LEN, LANE), F32), cos], 0)
    sin = jnp.concatenate([jnp.zeros((CTX_LEN, LANE), F32), sin], 0)
    return cos, sin


ATTN_KEY_CHUNKS = 3


def _rope(x, cos, sin_signed):
    lane = lax.broadcasted_iota(jnp.int32, x.shape, 1)
    nxt = pltpu.roll(x, LANE - 1, 1)
    prv = pltpu.roll(x, 1, 1)
    swapped = jnp.where(lane % 2 == 0, nxt, prv)
    return x * cos + swapped * sin_signed


def _rms(x, g):
    return x * lax.rsqrt(jnp.mean(x * x, -1, keepdims=True) + 1e-6) * g


def _attn_kernel(q_ref, qn_ref, qnn_ref, k_ref, v_ref, cos_ref, sin_ref, qg_ref, kg_ref, lam_ref, ng_ref, o_ref,
                 kb_ref, vt_ref, s_even_ref, s_odd_ref, qs_ref, *, n_maps, qk_norm, scale, lambda_init):
    g = pl.program_id(2)
    qt = pl.program_id(3)
    lt = kb_ref.shape[0]
    n_qt = lt // TM

    @pl.when((g == 0) & (qt == 0))
    def _prep_kv():
        k = k_ref[0]
        if qk_norm:
            k = _rms(k, kg_ref[...])
        kb_ref[...] = _rope(k, cos_ref[...], sin_ref[...]).astype(BF16)
        vt_ref[...] = v_ref[0].T.astype(BF16)

    def queries(ref, tile):
        q = ref[0]
        if qk_norm:
            q = _rms(q, qg_ref[...])
        row0 = pl.multiple_of(tile * TM, TM)
        q = _rope(q, cos_ref[pl.ds(row0, TM), :], sin_ref[pl.ds(row0, TM), :]) * (scale * math.log2(math.e))
        return q.astype(BF16)

    def query_maps(qb):
        if n_maps == 1:
            return (qb,)
        lane = lax.broadcasted_iota(jnp.int32, qb.shape, 1)
        zero = jnp.zeros_like(qb)
        return (jnp.where(lane < LANE // 2, qb, zero), jnp.where(lane >= LANE // 2, qb, zero))

    def scores_t(qm, rows):
        return lax.dot_general(kb_ref[rows, :], qm, (((1,), (1,)), ((), ())), preferred_element_type=F32)

    def finish(acc, l):
        if n_maps == 1:
            return (acc[0] * (1.0 / l[0])).T
        lp = lam_ref[...]
        lam = (jnp.exp(jnp.sum(lp[0:1, :] * lp[1:2, :])) - jnp.exp(jnp.sum(lp[2:3, :] * lp[3:4, :]))
               + lambda_init)
        o = (acc[0] * (1.0 / l[0]) - acc[1] * (lam / l[1])).T
        return _rms(o, ng_ref[...]) * (1.0 - lambda_init)

    @pl.when(qt == 0)
    def _first():
        ctx_keys = slice(0, CTX_LEN)
        acc, l = [], []
        for qm in query_maps(queries(q_ref, 0)):
            s = scores_t(qm, ctx_keys)
            e = jnp.exp2(s - jnp.max(s, 0, keepdims=True))
            l.append(jnp.sum(e, 0, keepdims=True))
            acc.append(jnp.dot(vt_ref[:, ctx_keys], e.astype(BF16), preferred_element_type=F32))
        o_ref[0] = finish(acc, l).astype(o_ref.dtype)
        for i, qm in enumerate(query_maps(queries(qn_ref, 1))):
            s_odd_ref[i] = scores_t(qm, slice(0, lt))
        qs_ref[...] = queries(qnn_ref, jnp.minimum(2, n_qt - 1))

    def steady(cur_ref, nxt_ref):
        ch = lt // ATTN_KEY_CHUNKS
        m = [jnp.max(cur_ref[i], 0, keepdims=True) for i in range(n_maps)]
        qn_maps = query_maps(qs_ref[...])
        l = [jnp.zeros((1, TM), F32) for _ in range(n_maps)]
        acc = [jnp.zeros((LANE, TM), F32) for _ in range(n_maps)]
        for j in range(ATTN_KEY_CHUNKS):
            rows = slice(j * ch, (j + 1) * ch)
            for i in range(n_maps):
                nxt_ref[i, rows, :] = scores_t(qn_maps[i], rows)
            for i in range(n_maps):
                e = jnp.exp2(cur_ref[i, rows, :] - m[i])
                l[i] = l[i] + jnp.sum(e, 0, keepdims=True)
                acc[i] = acc[i] + jnp.dot(vt_ref[:, rows], e.astype(BF16), preferred_element_type=F32)
            if j == 0:
                qs_ref[...] = queries(qnn_ref, jnp.minimum(qt + 2, n_qt - 1))
        o_ref[0] = finish(acc, l).astype(o_ref.dtype)

    for parity, (cur_ref, nxt_ref) in enumerate(((s_even_ref, s_odd_ref), (s_odd_ref, s_even_ref))):
        @pl.when((qt > 0) & (qt % 2 == parity))
        def _steady(cur_ref=cur_ref, nxt_ref=nxt_ref):
            steady(cur_ref, nxt_ref)


def _attention(p, cos, sin, q_gain, k_gain, lam_p, norm_g, *, n_kv, group, q_col, k_col, v_col,
               n_maps, qk_norm, scale, lambda_init):
    bsz, lt, _ = p.shape
    n_qt = lt // TM
    kernel = functools.partial(_attn_kernel, n_maps=n_maps, qk_norm=qk_norm, scale=scale,
                               lambda_init=lambda_init)
    const = lambda b, kv, g, qt: (0, 0)
    return pl.pallas_call(
        kernel,
        grid=(bsz, n_kv, group, n_qt),
        in_specs=[pl.BlockSpec((1, TM, LANE), lambda b, kv, g, qt: (b, qt, q_col + kv * group + g)),
                  pl.BlockSpec((1, TM, LANE),
                               lambda b, kv, g, qt: (b, jnp.minimum(qt + 1, n_qt - 1), q_col + kv * group + g)),
                  pl.BlockSpec((1, TM, LANE),
                               lambda b, kv, g, qt: (b, jnp.minimum(qt + 2, n_qt - 1), q_col + kv * group + g)),
                  pl.BlockSpec((1, lt, LANE), lambda b, kv, g, qt: (b, 0, k_col + kv)),
                  pl.BlockSpec((1, lt, LANE), lambda b, kv, g, qt: (b, 0, v_col + kv)),
                  pl.BlockSpec((lt, LANE), const),
                  pl.BlockSpec((lt, LANE), const),
                  pl.BlockSpec((1, LANE), const),
                  pl.BlockSpec((1, LANE), const),
                  pl.BlockSpec(lam_p.shape, const),
                  pl.BlockSpec((1, LANE), const)],
        out_specs=pl.BlockSpec((1, TM, LANE), lambda b, kv, g, qt: (b, qt, kv * group + g)),
        out_shape=jax.ShapeDtypeStruct((bsz, lt, n_kv * group * LANE), BF16),
        scratch_shapes=[pltpu.VMEM((lt, LANE), BF16), pltpu.VMEM((LANE, lt), BF16),
                        pltpu.VMEM((n_maps, lt, TM), F32), pltpu.VMEM((n_maps, lt, TM), F32),
                        pltpu.VMEM((TM, LANE), BF16)],
        compiler_params=_params(4),
        name="attention",
    )(p, p, p, p, p, cos, sin, q_gain.reshape(1, LANE), k_gain.reshape(1, LANE), lam_p, norm_g.reshape(1, LANE))


def _s5_out_kernel(y_ref, x_ref, mod_ref, d_ref, w_ref, g_ref, b_ref, o_ref):
    sh = mod_ref[0, 0:1, :]
    sc = mod_ref[0, 1:2, :]
    gate = mod_ref[0, 2:3, :]
    x = x_ref[0]
    u = x * (1.0 + sc) + sh
    y = jax.nn.gelu(y_ref[0] + d_ref[...] * u).astype(BF16)
    z = jnp.dot(y, w_ref[...], preferred_element_type=F32)
    out = z[:, :D_MODEL] * jax.nn.sigmoid(z[:, D_MODEL:])
    o_ref[0] = _layer_norm(DEEPNORM_ALPHA * x + gate * out, g_ref[...], b_ref[...])


def _s5_out(y, xs, mods, d_skip, w_glu, ln_g, ln_b, layer, latent_only=False):
    bsz, lt, _ = xs.shape
    skip = 1 if latent_only else 0
    return pl.pallas_call(
        _s5_out_kernel,
        grid=(bsz, lt // TM - skip),
        in_specs=[_row_spec(D_MODEL, skip),
                  _row_spec(D_MODEL, skip),
                  _mod_spec(layer, skip),
                  pl.BlockSpec((1, D_MODEL), lambda b, r: (0, 0)),
                  pl.BlockSpec((D_MODEL, 2 * D_MODEL), lambda b, r: (0, 0)),
                  pl.BlockSpec((1, D_MODEL), lambda b, r: (0, 0)),
                  pl.BlockSpec((1, D_MODEL), lambda b, r: (0, 0))],
        out_specs=_row_spec(D_MODEL),
        out_shape=jax.ShapeDtypeStruct((bsz, lt - skip * TM, D_MODEL), F32),
        compiler_params=_params(2),
        name="s5_out_ln",
    )(y, xs, mods, d_skip.reshape(1, D_MODEL), w_glu, ln_g.reshape(1, D_MODEL), ln_b.reshape(1, D_MODEL))


DN_BLK = 128
DN_PREP_BLOCKS = 3
DN_GATE_BETA_F, DN_GATE_BETA_B, DN_GATE_A_F, DN_GATE_A_B = 0, DN_HEADS, 2 * DN_HEADS, 3 * DN_HEADS
_NT = (((1,), (1,)), ((), ()))
_TN = (((0,), (0,)), ((), ()))


def _dn_kernel(q_ref, k_ref, v_ref, z_ref, g_ref, cq_ref, ck_ref, cv_ref, alog_ref, dtb_ref, ng_ref, o_ref,
               beta_ref, gc_ref, tot_ref, gct_ref, qn_ref, kn_ref, vn_ref,
               wq_f, u_f, qk_f, kd_f, gl_f, o_f, wq_b, u_b, qk_b, kd_b, gl_b, o_b, *, n_blk, n_ctx_blk):
    h = pl.program_id(1)
    lt = n_blk * DN_BLK
    ctx_rows = n_ctx_blk * DN_BLK
    row = lax.broadcasted_iota(jnp.int32, (lt, 1), 0)
    lane = lax.broadcasted_iota(jnp.int32, (1, LANE), 1)

    @pl.when(h == 0)
    def _gates():
        gts = g_ref[0]
        beta_ref[...] = jax.nn.sigmoid(gts)
        g = -jnp.exp(alog_ref[...]) * jax.nn.softplus(gts + dtb_ref[...])
        pos = row % DN_BLK
        pre = g
        suf = g
        s = 1
        while s < DN_BLK:
            pre = pre + jnp.where(pos >= s, pltpu.roll(pre, s, 0), 0.0)
            suf = suf + jnp.where(pos < DN_BLK - s, pltpu.roll(suf, lt - s, 0), 0.0)
            s *= 2
        gc = jnp.where(lane >= DN_GATE_A_B, suf, pre)
        gc_ref[...] = gc
        tot_ref[...] = pre + suf - g
        for c in range(n_blk):
            gct_ref[c] = gc[c * DN_BLK:(c + 1) * DN_BLK, :].T

    def conv_silu(x_ref, w_ref):
        x = x_ref[0]
        w = w_ref[...]
        acc = x * w[DN_CONV // 2:DN_CONV // 2 + 1, :]
        for d in range(-(DN_CONV // 2), DN_CONV // 2 + 1):
            if d == 0:
                continue
            src = row + d
            ok = (src >= 0) & (src < lt) & ((src < ctx_rows) == (row < ctx_rows))
            acc = acc + jnp.where(ok, pltpu.roll(x, (-d) % lt, 0), 0.0) * w[d + DN_CONV // 2:d + DN_CONV // 2 + 1, :]
        return acc * jax.nn.sigmoid(acc)

    def l2n(x):
        return x * lax.rsqrt(jnp.sum(x * x, -1, keepdims=True) + 1e-6)

    qn_ref[...] = l2n(conv_silu(q_ref, cq_ref)) * DN_HEAD_DIM ** -0.5
    kn_ref[...] = l2n(conv_silu(k_ref, ck_ref))
    vn_ref[...] = conv_silu(v_ref, cv_ref)

    ii = lax.broadcasted_iota(jnp.int32, (DN_BLK, DN_BLK), 0)
    jj = lax.broadcasted_iota(jnp.int32, (DN_BLK, DN_BLK), 1)
    fwd =(DN_GATE_BETA_F, DN_GATE_A_F, ii >= jj, ii > jj, (wq_f, u_f, qk_f, kd_f, gl_f, o_f))
    bwd = (DN_GATE_BETA_B, DN_GATE_A_B, ii <= jj, ii < jj, (wq_b, u_b, qk_b, kd_b, gl_b, o_b))

    ii2 = lax.broadcasted_iota(jnp.int32, (2 * DN_BLK, 2 * DN_BLK), 0)
    jj2 = lax.broadcasted_iota(jnp.int32, (2 * DN_BLK, 2 * DN_BLK), 1)
    eye2 = (ii2 == jj2).astype(F32)
    zero_blk = jnp.zeros((DN_BLK, DN_BLK), F32)

    def pair_off(s):
        return ((ii2 // (2 * s)) == (jj2 // (2 * s))) & ((ii2 // s) != (jj2 // s))

    def column(ref, rows, lane_idx):
        return jnp.sum(jnp.where(lane == lane_idx, ref[rows, :], 0.0), -1, keepdims=True)

    def block_inputs(c):
        rows = pl.ds(pl.multiple_of(c * DN_BLK, DN_BLK), DN_BLK)
        qc, kc, vc = qn_ref[rows, :], kn_ref[rows, :], vn_ref[rows, :]
        kb = kc.astype(BF16)
        kk = lax.dot_general(kb, kb, _NT, preferred_element_type=F32)
        qk = lax.dot_general(qc.astype(BF16), kb, _NT, preferred_element_type=F32)
        a_dir, rhs_dir = [], []
        for lane_beta, lane_g, incl, strict, (wq_ref, _, qk_ref, kd_ref, gl_ref, _) in (fwd, bwd):
            bcol = column(beta_ref, rows, lane_beta + h)
            gcol = column(gc_ref, rows, lane_g + h)
            tcol = column(tot_ref, rows, lane_g + h)
            grow = gct_ref[c, pl.ds(lane_g + h, 1), :]
            dec = jnp.exp(jnp.where(incl, gcol - grow, -jnp.inf))
            a_dir.append(jnp.where(strict, bcol * kk * dec, 0.0))
            egc = jnp.exp(gcol)
            rhs_dir.append(jnp.concatenate([bcol * vc, (bcol * egc) * kc], -1).astype(BF16))
            wq_ref[c, DN_BLK:2 * DN_BLK, :] = (qc * egc).astype(BF16)
            qk_ref[c] = (qk * dec).astype(BF16)
            kd_ref[c] = (kc * jnp.exp(tcol - gcol)).T.astype(BF16)
            gl_ref[c] = jnp.broadcast_to(jnp.exp(tcol[0:1, :]), (8, LANE))
        a = jnp.concatenate([jnp.concatenate([a_dir[0], zero_blk], 1),
                             jnp.concatenate([zero_blk, a_dir[1]], 1)], 0)
        return a, jnp.concatenate(rhs_dir, 0)

    def prepare(it, _):
        blocks = [it * DN_PREP_BLOCKS + i for i in range(DN_PREP_BLOCKS)]
        a_rhs = [block_inputs(c) for c in blocks]
        t_inv, m = [], []
        for a, _ in a_rhs:
            a_off = jnp.where(pair_off(1), a, 0.0)
            t_inv.append(eye2 - a_off)
            neighbour = jnp.where(ii2 < DN_BLK, pltpu.roll(a, 1, 0), pltpu.roll(a, 2 * DN_BLK - 1, 0))
            m.append(a - jnp.sum(a_off, -1, keepdims=True) * neighbour)
        s = 2
        while s < DN_BLK:
            for i in range(DN_PREP_BLOCKS):
                m_off = jnp.where(pair_off(s), m[i], 0.0).astype(BF16)
                if 2 * s < DN_BLK:
                    upd = jnp.dot(m_off, jnp.concatenate([t_inv[i], m[i]], 1).astype(BF16),
                                  preferred_element_type=F32)
                    t_inv[i] = t_inv[i] - upd[:, :2 * DN_BLK]
                    m[i] = m[i] - upd[:, 2 * DN_BLK:]
                else:
                    t_inv[i] = t_inv[i] - jnp.dot(m_off, t_inv[i].astype(BF16), preferred_element_type=F32)
            s *= 2
        for i, c in enumerate(blocks):
            uw = jnp.dot(t_inv[i].astype(BF16), a_rhs[i][1], preferred_element_type=F32)
            for d, (wq_ref, u_ref) in enumerate(((wq_f, u_f), (wq_b, u_b))):
                u_ref[c] = uw[d * DN_BLK:(d + 1) * DN_BLK, :DN_HEAD_DIM]
                wq_ref[c, 0:DN_BLK, :] = uw[d * DN_BLK:(d + 1) * DN_BLK, DN_HEAD_DIM:].astype(BF16)
        return 0

    lax.fori_loop(0, n_blk // DN_PREP_BLOCKS, prepare, 0)

    zero_bf = jnp.zeros((DN_BLK, DN_BLK), BF16)

    def block_diag(top, bottom):
        return jnp.concatenate([jnp.concatenate([top, zero_bf], 1), jnp.concatenate([zero_bf, bottom], 1)], 0)

    def step(k, carry):
        s_f, s_b = carry
        cb = jnp.where(k < n_ctx_blk, n_ctx_blk - 1 - k, n_blk + n_ctx_blk - 1 - k)
        hd = DN_HEAD_DIM
        r = jnp.dot(jnp.concatenate([wq_f[k], wq_b[cb]], 1), block_diag(s_f.astype(BF16), s_b.astype(BF16)),
                    preferred_element_type=F32)
        v_new = block_diag((u_f[k] - r[:DN_BLK, :hd]).astype(BF16), (u_b[cb] - r[:DN_BLK, hd:]).astype(BF16))
        lhs = jnp.concatenate([jnp.concatenate([qk_f[k], qk_b[cb]], 1),
                               jnp.concatenate([kd_f[k], kd_b[cb]], 1)], 0)
        r2 = jnp.dot(lhs, v_new, preferred_element_type=F32)
        o_f[k] = r[DN_BLK:, :hd] + r2[:DN_BLK, :hd]
        o_b[cb] = r[DN_BLK:, hd:] + r2[:DN_BLK, hd:]
        return (gl_f[k][0:1, :] * s_f + r2[DN_BLK:, :hd], gl_b[cb][0:1, :] * s_b + r2[DN_BLK:, hd:])

    zero = jnp.zeros((DN_HEAD_DIM, DN_HEAD_DIM), F32)
    lax.fori_loop(0, n_blk, step, (zero, zero))

    o = (o_f[...] + o_b[...]).reshape(lt, DN_HEAD_DIM)
    o = o * lax.rsqrt(jnp.mean(o * o, -1, keepdims=True) + 1e-6) * ng_ref[...]
    z = z_ref[0]
    o_ref[0] = (o * (z * jax.nn.sigmoid(z))).astype(o_ref.dtype)


def _deltanet_core(p, conv_w, a_log, dt_bias, norm_g):
    bsz, lt, _ = p.shape
    n_blk = lt // DN_BLK
    gate_pad = lambda t: jnp.pad(t.reshape(1, 2 * DN_HEADS), ((0, 0), (DN_GATE_A_F, LANE - DN_GATE_A_F - 2 * DN_HEADS)))
    kernel = functools.partial(_dn_kernel, n_blk=n_blk, n_ctx_blk=CTX_LEN // DN_BLK)
    col = lambda off: pl.BlockSpec((1, lt, LANE), lambda b, h: (b, 0, off + h))
    cw = lambda off: pl.BlockSpec((DN_CONV, LANE), lambda b, h: (0, off + h))
    const = pl.BlockSpec((1, LANE), lambda b, h: (0, 0))
    seq = lambda dt: pltpu.VMEM((lt, LANE), dt)
    blk = lambda rows, dt: pltpu.VMEM((n_blk, rows, LANE), dt)
    per_dir = [blk(2 * DN_BLK, BF16), blk(DN_BLK, F32), blk(DN_BLK, BF16), blk(DN_BLK, BF16), blk(8, F32),
               blk(DN_BLK, F32)]
    return pl.pallas_call(
        kernel,
        grid=(bsz, DN_HEADS),
        in_specs=[col(0), col(DN_HEADS), col(2 * DN_HEADS), col(3 * DN_HEADS),
                  pl.BlockSpec((1, lt, LANE), lambda b, h: (b, 0, 4 * DN_HEADS)),
                  cw(0), cw(DN_HEADS), cw(2 * DN_HEADS), const, const, const],
        out_specs=pl.BlockSpec((1, lt, LANE), lambda b, h: (b, 0, h)),
        out_shape=jax.ShapeDtypeStruct((bsz, lt, DN_WIDTH), BF16),
        scratch_shapes=[seq(F32), seq(F32), seq(F32), blk(LANE, F32), seq(F32), seq(F32), seq(F32)] + per_dir + per_dir,
        compiler_params=_params(2),
        name="deltanet_core",
    )(p, p, p, p, p, conv_w, conv_w, conv_w, gate_pad(a_log), gate_pad(dt_bias), norm_g.reshape(1, LANE))


S5_T = 8
S5_BB = 4
S5_BLK_GROUPS = LANE // SS_GROUP
S5_HALF = S5_BLK_GROUPS * SS_STATE
S5_K = S5_T * LANE


def _s5_weights(a_re, a_im, log_dt, b_re, b_im, c_re, c_im):
    hp = lax.Precision.HIGHEST
    t = S5_T
    nj = SS_GROUPS // S5_BLK_GROUPS
    ar = jnp.minimum(a_re, -1e-4)
    ai = a_im
    dt = jnp.exp(log_dt)[..., None]
    zr, zi = ar * dt, ai * dt
    d = jnp.arange(t + 1, dtype=F32)[:, None, None, None]
    mag = jnp.exp(d * zr)
    er, ei = mag * jnp.cos(d * zi), mag * jnp.sin(d * zi)
    nr, ni = er[1] - 1.0, ei[1]
    den = ar * ar + ai * ai
    fr, fi = (nr * ar + ni * ai) / den, (ni * ar - nr * ai) / den
    bbr = fr[..., None] * b_re - fi[..., None] * b_im
    bbi = fr[..., None] * b_im + fi[..., None] * b_re
    cer = c_re[None] * er[:, :, :, None, :] - c_im[None] * ei[:, :, :, None, :]
    cei = c_re[None] * ei[:, :, :, None, :] + c_im[None] * er[:, :, :, None, :]
    kk = jnp.einsum('tdgiq,dgqj->tdgij', jnp.concatenate([cer, -cei], -1), jnp.concatenate([bbr, bbi], -2),
                    precision=hp)
    s_idx = jnp.arange(t)[:, None]
    t_idx = jnp.arange(t)[None, :]
    lag_f = jnp.clip(t_idx - s_idx, 0, t)
    lag_b = jnp.clip(s_idx - t_idx, 0, t)
    m_f = (t_idx >= s_idx)[:, :, None, None, None]
    m_b = (s_idx >= t_idx)[:, :, None, None, None]
    kst = jnp.where(m_f, kk[lag_f, 0], 0.0) + jnp.where(m_b, kk[lag_b, 1], 0.0)
    kst = kst.reshape(t, t, nj, S5_BLK_GROUPS, SS_GROUP, SS_GROUP).transpose(2, 0, 3, 5, 1, 4)
    m_intra = kst.reshape(nj, S5_K, t * SS_GROUP)

    pow_f = (t - 1 - jnp.arange(t))
    pow_b = jnp.arange(t)

    def carry_w(pw, dd):
        e_r, e_i = er[pw, dd], ei[pw, dd]
        re = e_r[..., None] * bbr[dd][None] - e_i[..., None] * bbi[dd][None]
        im = e_r[..., None] * bbi[dd][None] + e_i[..., None] * bbr[dd][None]
        return jnp.stack([re, im], 0)

    wb = jnp.stack([carry_w(pow_f, 0), carry_w(pow_b, 1)], 0)
    wb = wb.reshape(2, 2, t, nj, S5_BLK_GROUPS, SS_STATE, SS_GROUP)
    m_carry = wb.transpose(3, 2, 4, 6, 0, 1, 5).reshape(nj, S5_K, 4 * SS_STATE)

    def read_w(pw, dd):
        wc = jnp.stack([cer[pw, dd], -cei[pw, dd]], 0)
        wc = wc.reshape(2, t, nj, S5_BLK_GROUPS, SS_GROUP, SS_STATE)
        return wc.transpose(2, 0, 3, 5, 1, 4).reshape(nj, 2 * S5_HALF, t * SS_GROUP)

    m_read_f = read_w(jnp.arange(t) + 1, 0)
    m_read_b = read_w(t - jnp.arange(t), 1)
    lam_t = jnp.stack([er[t, 0], ei[t, 0], er[t, 1], ei[t, 1]], 0)
    lam_t = lam_t.reshape(4, nj, S5_HALF).transpose(1, 0, 2)
    return m_carry.astype(BF16), m_intra.astype(BF16), m_read_f.astype(BF16), m_read_b.astype(BF16), lam_t


def _s5_expand(m, row_group_div, col_seg, n_col):
    rows, n_in = m.shape
    q = lax.broadcasted_iota(jnp.int32, (n_in, n_col), 0)
    c = lax.broadcasted_iota(jnp.int32, (n_in, n_col), 1)
    out_seg = col_seg * S5_BLK_GROUPS
    rep = ((q // col_seg == c // out_seg) & (q % col_seg == c % col_seg)).astype(BF16)
    wide = jnp.dot(m, rep, preferred_element_type=F32)
    g_row = (lax.broadcasted_iota(jnp.int32, (rows, n_col), 0) // row_group_div) % S5_BLK_GROUPS
    g_col = (lax.broadcasted_iota(jnp.int32, (rows, n_col), 1) // col_seg) % S5_BLK_GROUPS
    return jnp.where(g_row == g_col, wide, 0.0).astype(BF16)


def _s5_kernel(x_ref, mod_ref, mc_ref, mi_ref, mrf_ref, mrb_ref, lt_ref, y_ref,
               wb_ref, wi_ref, wcf_ref, wcb_ref, xcat_ref, hf_ref, hb_ref, *, n_chunk, n_ctx_chunk):
    bh = pl.program_id(1)

    @pl.when(bh == 0)
    def _expand_weights():
        wb_ref[...] = _s5_expand(mc_ref[0], SS_GROUP, SS_STATE, 4 * S5_HALF)
        wi_ref[...] = _s5_expand(mi_ref[0], SS_GROUP, SS_GROUP, S5_K)
        wcf_ref[...] = _s5_expand(mrf_ref[0], SS_STATE, SS_GROUP, S5_K)
        wcb_ref[...] = _s5_expand(mrb_ref[0], SS_STATE, SS_GROUP, S5_K)

    nb = x_ref.shape[0]
    n_lb = hf_ref.shape[0]
    half_lb = n_lb // 2
    is_ctx = lax.broadcasted_iota(jnp.int32, (n_chunk, 1), 0) < n_ctx_chunk
    sh_c = mod_ref[8, 0:1, :]
    sc_c = mod_ref[8, 1:2, :]

    def carry_in(bl, _):
        mb = mod_ref[bh * nb + bl]
        sh = jnp.where(is_ctx, sh_c, mb[0:1, :])
        sc1 = 1.0 + jnp.where(is_ctx, sc_c, mb[1:2, :])
        rows = pl.ds(pl.multiple_of(bl * n_chunk, n_chunk), n_chunk)
        for tau in range(S5_T):
            xt = x_ref[bl, pl.ds(tau, n_chunk, stride=S5_T), :]
            xcat_ref[rows, tau * LANE:(tau + 1) * LANE] = (xt * sc1 + sh).astype(BF16)
        hbv = jnp.dot(xcat_ref[rows, :], wb_ref[...], preferred_element_type=F32)
        for l in range(n_lb):
            hf_ref[l, rows, :] = hbv[:, l * LANE:(l + 1) * LANE]
            hb_ref[l, rows, :] = hbv[:, (n_lb + l) * LANE:(n_lb + l + 1) * LANE]
        return 0

    lax.fori_loop(0, nb, carry_in, 0)

    lt = lt_ref[0]

    def lam_blocks(i):
        return [jnp.broadcast_to(lt[i:i + 1, l * LANE:(l + 1) * LANE], (nb, LANE)) for l in range(half_lb)]

    a_f = (lam_blocks(0), lam_blocks(1))
    a_b = (lam_blocks(2), lam_blocks(3))

    def advance(h_ref, c, a, state):
        a_re, a_im = a
        new = []
        for l in range(half_lb):
            sel = pl.ds(c, nb, stride=n_chunk)
            in_re = h_ref[l, sel, :]
            in_im = h_ref[half_lb + l, sel, :]
            s_re, s_im = state[l], state[half_lb + l]
            h_ref[l, sel, :] = s_re
            h_ref[half_lb + l, sel, :] = s_im
            new.append((a_re[l] * s_re - a_im[l] * s_im + in_re, a_re[l] * s_im + a_im[l] * s_re + in_im))
        return tuple(n[0] for n in new) + tuple(n[1] for n in new)

    def step(k, carry):
        st_f, st_b = carry
        cb = jnp.where(k < n_ctx_chunk, n_ctx_chunk - 1 - k, n_chunk + n_ctx_chunk - 1 - k)
        return advance(hf_ref, k, a_f, st_f), advance(hb_ref, cb, a_b, st_b)

    zero = tuple(jnp.zeros((nb, LANE), F32) for _ in range(n_lb))
    lax.fori_loop(0, n_chunk, step, (zero, zero))

    def read_out(bl, _):
        rows = pl.ds(pl.multiple_of(bl * n_chunk, n_chunk), n_chunk)
        h_f = jnp.concatenate([hf_ref[l, rows, :] for l in range(n_lb)], -1).astype(BF16)
        h_b = jnp.concatenate([hb_ref[l, rows, :] for l in range(n_lb)], -1).astype(BF16)
        y = (jnp.dot(xcat_ref[rows, :], wi_ref[...], preferred_element_type=F32)
             + jnp.dot(h_f, wcf_ref[...], preferred_element_type=F32)
             + jnp.dot(h_b, wcb_ref[...], preferred_element_type=F32))
        for t in range(S5_T):
            y_ref[bl, pl.ds(t, n_chunk, stride=S5_T), :] = y[:, t * LANE:(t + 1) * LANE]
        return 0

    lax.fori_loop(0, nb, read_out, 0)


def _s5_core(xs, mods, weights, layer):
    bsz, lt, _ = xs.shape
    m_carry, m_intra, m_read_f, m_read_b, lam_t = weights
    nj = D_MODEL // LANE
    nb = min(S5_BB, bsz)
    n_chunk = lt // S5_T
    kernel = functools.partial(_s5_kernel, n_chunk=n_chunk, n_ctx_chunk=CTX_LEN // S5_T)
    wspec = lambda shape: pl.BlockSpec((1,) + shape, lambda j, bh: (j, 0, 0))
    return pl.pallas_call(
        kernel,
        grid=(nj, bsz // nb),
        in_specs=[pl.BlockSpec((nb, lt, LANE), lambda j, bh: (bh, 0, j)),
                  pl.BlockSpec((MOD_ROWS, 6, LANE), lambda j, bh: (layer, 0, j)),
                  wspec(m_carry.shape[1:]), wspec(m_intra.shape[1:]),
                  wspec(m_read_f.shape[1:]), wspec(m_read_b.shape[1:]),
                  wspec((4, S5_HALF))],
        out_specs=pl.BlockSpec((nb, lt, LANE), lambda j, bh: (bh, 0, j)),
        out_shape=jax.ShapeDtypeStruct((bsz, lt, D_MODEL), F32),
        scratch_shapes=[pltpu.VMEM((S5_K, 4 * S5_HALF), BF16), pltpu.VMEM((S5_K, S5_K), BF16),
                        pltpu.VMEM((2 * S5_HALF, S5_K), BF16), pltpu.VMEM((2 * S5_HALF, S5_K), BF16),
                        pltpu.VMEM((nb * n_chunk, S5_K), BF16),
                        pltpu.VMEM((2 * S5_HALF // LANE, nb * n_chunk, LANE), F32),
                        pltpu.VMEM((2 * S5_HALF // LANE, nb * n_chunk, LANE), F32)],
        compiler_params=_params(2),
        name="s5_core",
    )(xs, mods, m_carry, m_intra, m_read_f, m_read_b, lam_t)


def kernel(x, c, ctx, c_ctx, ada_w, ada_b, ln_g, ln_b, mlp_w1, mlp_w2, dn_w_in, dn_conv, dn_a_log, dn_dt_bias, dn_norm_g, dn_w_out, da_w_qkv, da_lambda, da_norm_g, da_w_out, ga_w_qkv, ga_q_norm, ga_k_norm, ga_w_out, ss_a_re, ss_a_im, ss_log_dt, ss_b_re, ss_b_im, ss_c_re, ss_c_im, ss_d, ss_w_glu):
    bsz, n_latent, _ = x.shape
    xs = jnp.concatenate([ctx, x], 1)
    c_rows = jnp.concatenate([c, c_ctx[None, :], jnp.zeros((MOD_ROWS - bsz - 1, D_MODEL), F32)], 0)
    mods = _ada_all(c_rows, ada_w, ada_b).reshape(DEPTH * MOD_ROWS, 6, D_MODEL)
    ones = jnp.ones((LANE,), F32)
    zeros4 = jnp.zeros((4, DA_HEAD_DIM), F32)

    for i in range(DEPTH):
        m, j = i % 4, i // 4
        last = i == DEPTH - 1
        if m == 0:
            w_in = jnp.pad(dn_w_in[j], ((0, 0), (0, LANE - 4 * DN_HEADS))).astype(BF16)
            p = _proj(xs, mods, w_in, i)
            o = _deltanet_core(p, dn_conv[j], dn_a_log[j], dn_dt_bias[j], dn_norm_g[j])
            w_out = dn_w_out[j]
        elif m == 1:
            lambda_init = 0.8 - 0.6 * math.exp(-0.3 * i)
            p = _proj(xs, mods, da_w_qkv[j].astype(BF16), i)
            cos, sin = _rope_tables(n_latent, DA_HEAD_DIM)
            o = _attention(p, cos, sin, ones, ones, da_lambda[j], da_norm_g[j],
                           n_kv=DA_HEADS, group=1, q_col=0, k_col=DA_HEADS, v_col=2 * DA_HEADS,
                           n_maps=2, qk_norm=False, scale=DA_HEAD_DIM ** -0.5, lambda_init=lambda_init)
            w_out = da_w_out[j]
        elif m == 2:
            p = _proj(xs, mods, ga_w_qkv[j].astype(BF16), i)
            cos, sin = _rope_tables(n_latent, GA_HEAD_DIM)
            o = _attention(p, cos, sin, ga_q_norm[j], ga_k_norm[j], zeros4, ones,
                           n_kv=GA_KV_HEADS, group=GA_HEADS // GA_KV_HEADS, q_col=0, k_col=GA_HEADS,
                           v_col=GA_HEADS + GA_KV_HEADS, n_maps=1, qk_norm=True,
                           scale=GA_HEAD_DIM ** -0.5, lambda_init=0.0)
            w_out = ga_w_out[j]
        w1, w2 = mlp_w1[i].astype(BF16), mlp_w2[i].astype(BF16)
        if m == 3:
            weights = _s5_weights(ss_a_re[j], ss_a_im[j], ss_log_dt[j], ss_b_re[j], ss_b_im[j],
                                  ss_c_re[j], ss_c_im[j])
            y = _s5_core(xs, mods, weights, i)
            xs = _s5_out(y, xs, mods, ss_d[j], ss_w_glu[j].astype(BF16), ln_g[i, 0], ln_b[i, 0], i, last)
            xs = _mlp(xs, mods, w1, w2, ln_g[i, 1], ln_b[i, 1], i, last)
        else:
            xs = _out_mlp(o, xs, mods, w_out.astype(BF16), w1, w2, ln_g[i], ln_b[i], i, last)
    return xs
```

```python
import functools
import math

import jax
import jax.numpy as jnp
from jax import lax
from jax.experimental import pallas as pl
from jax.experimental.pallas import tpu as pltpu

F32 = jnp.float32
BF16 = jnp.bfloat16

D_MODEL = 1024
D_FF = 4 * D_MODEL
DEPTH = 4
GRID_W = 64
CTX_LEN = 256
ROPE_THETA = 10000.0
DEEPNORM_ALPHA = (2 * DEPTH) ** 0.25
TM = 256
LANE = 128
MOD_ROWS = 16
VMEM_LIMIT = 56 * 1024 * 1024

DN_HEADS = 8
DN_HEAD_DIM = 128
DN_WIDTH = DN_HEADS * DN_HEAD_DIM
DN_CONV = 5
DN_CHUNK = 64
DA_HEADS = 8
DA_HEAD_DIM = 64
GA_HEADS = 8
GA_KV_HEADS = 2
GA_HEAD_DIM = 128
SS_GROUP = 16
SS_GROUPS = D_MODEL // SS_GROUP
SS_STATE = 64


def _params(n_axes):
    return pltpu.CompilerParams(dimension_semantics=("arbitrary",) * n_axes,
                                vmem_limit_bytes=VMEM_LIMIT)


def _layer_norm(v, g, b):
    mu = jnp.mean(v, -1, keepdims=True)
    d = v - mu
    var = jnp.mean(d * d, -1, keepdims=True)
    return d * lax.rsqrt(var + 1e-5) * g + b


def _mod_spec(layer, skip=0):
    return pl.BlockSpec((1, 6, D_MODEL),
                        lambda b, r: (layer * MOD_ROWS + jnp.where(r + skip == 0, 8, b), 0, 0))


def _row_spec(width, skip=0):
    return pl.BlockSpec((1, TM, width), lambda b, r: (b, r + skip, 0))


def _ada_kernel(c_ref, w_ref, b_ref, o_ref):
    c = c_ref[...]
    act = (c * jax.nn.sigmoid(c)).astype(BF16)
    o_ref[0] = jnp.dot(act, w_ref[0].astype(BF16), preferred_element_type=F32) + b_ref[0]


def _ada_all(c_rows, ada_w, ada_b):
    tn = 1536
    n = 6 * D_MODEL
    return pl.pallas_call(
        _ada_kernel,
        grid=(DEPTH, n // tn),
        in_specs=[pl.BlockSpec((MOD_ROWS, D_MODEL), lambda i, j: (0, 0)),
                  pl.BlockSpec((1, D_MODEL, tn), lambda i, j: (i, 0, j)),
                  pl.BlockSpec((1, 1, tn), lambda i, j: (i, 0, j))],
        out_specs=pl.BlockSpec((1, MOD_ROWS, tn), lambda i, j: (i, 0, j)),
        out_shape=jax.ShapeDtypeStruct((DEPTH, MOD_ROWS, n), F32),
        compiler_params=_params(2),
        name="ada_mod",
    )(c_rows, ada_w, ada_b.reshape(DEPTH, 1, n))


def _proj_kernel(x_ref, mod_ref, w_ref, o_ref):
    sh = mod_ref[0, 0:1, :]
    sc = mod_ref[0, 1:2, :]
    h = (x_ref[0] * (1.0 + sc) + sh).astype(BF16)
    o_ref[0] = jnp.dot(h, w_ref[...], preferred_element_type=F32).astype(o_ref.dtype)


def _proj(xs, mods, w, layer, out_dtype=F32):
    bsz, lt, _ = xs.shape
    n = w.shape[1]
    return pl.pallas_call(
        _proj_kernel,
        grid=(bsz, lt // TM),
        in_specs=[pl.BlockSpec((1, TM, D_MODEL), lambda b, r: (b, r, 0)),
                  _mod_spec(layer),
                  pl.BlockSpec((D_MODEL, n), lambda b, r: (0, 0))],
        out_specs=pl.BlockSpec((1, TM, n), lambda b, r: (b, r, 0)),
        out_shape=jax.ShapeDtypeStruct((bsz, lt, n), out_dtype),
        compiler_params=_params(2),
        name="mod_proj",
    )(xs, mods, w)


FF_CHUNK = 1024


def _mlp_body(x, mod_ref, w1_ref, w2_ref, g_ref, b_ref, y_ref):
    sh = mod_ref[0, 3:4, :]
    sc = mod_ref[0, 4:5, :]
    gate = mod_ref[0, 5:6, :]
    h = (x * (1.0 + sc) + sh).astype(BF16)
    acc = jnp.zeros((TM, D_MODEL), F32)
    for j in range(D_FF // FF_CHUNK):
        a = jnp.dot(h, w1_ref[:, j * FF_CHUNK:(j + 1) * FF_CHUNK], preferred_element_type=F32)
        a = jnp.square(jnp.maximum(a, 0.0)).astype(BF16)
        acc = acc + jnp.dot(a, w2_ref[j * FF_CHUNK:(j + 1) * FF_CHUNK, :], preferred_element_type=F32)
    y_ref[0] = _layer_norm(DEEPNORM_ALPHA * x + gate * acc, g_ref[...], b_ref[...])


def _mlp_kernel(x_ref, mod_ref, w1_ref, w2_ref, g_ref, b_ref, y_ref):
    _mlp_body(x_ref[0], mod_ref, w1_ref, w2_ref, g_ref, b_ref, y_ref)


def _out_mlp_kernel(o_ref, x_ref, mod_ref, wo_ref, g1_ref, b1_ref, w1_ref, w2_ref, g2_ref, b2_ref, y_ref):
    y = jnp.dot(o_ref[0], wo_ref[...], preferred_element_type=F32)
    x1 = _layer_norm(DEEPNORM_ALPHA * x_ref[0] + mod_ref[0, 2:3, :] * y, g1_ref[...], b1_ref[...])
    _mlp_body(x1, mod_ref, w1_ref, w2_ref, g2_ref, b2_ref, y_ref)


def _out_mlp(o, xs, mods, w_out, w1, w2, ln_g, ln_b, layer, latent_only=False):
    bsz, lt, k = o.shape
    skip = 1 if latent_only else 0
    const = lambda shape: pl.BlockSpec(shape, lambda b, r: (0, 0))
    vec = lambda t: t.reshape(1, D_MODEL)
    return pl.pallas_call(
        _out_mlp_kernel,
        grid=(bsz, lt // TM - skip),
        in_specs=[_row_spec(k, skip), _row_spec(D_MODEL, skip), _mod_spec(layer, skip),
                  const((k, D_MODEL)), const((1, D_MODEL)), const((1, D_MODEL)),
                  const((D_MODEL, D_FF)), const((D_FF, D_MODEL)), const((1, D_MODEL)), const((1, D_MODEL))],
        out_specs=_row_spec(D_MODEL),
        out_shape=jax.ShapeDtypeStruct((bsz, lt - skip * TM, D_MODEL), F32),
        compiler_params=_params(2),
        name="out_mlp_ln",
    )(o, xs, mods, w_out, vec(ln_g[0]), vec(ln_b[0]), w1, w2, vec(ln_g[1]), vec(ln_b[1]))


def _mlp(xs, mods, w1, w2, ln_g, ln_b, layer, latent_only=False):
    bsz, rows, _ = xs.shape
    skip = 1 if latent_only else 0
    return pl.pallas_call(
        _mlp_kernel,
        grid=(bsz, rows // TM),
        in_specs=[_row_spec(D_MODEL),
                  _mod_spec(layer, skip),
                  pl.BlockSpec((D_MODEL, D_FF), lambda b, r: (0, 0)),
                  pl.BlockSpec((D_FF, D_MODEL), lambda b, r: (0, 0)),
                  pl.BlockSpec((1, D_MODEL), lambda b, r: (0, 0)),
                  pl.BlockSpec((1, D_MODEL), lambda b, r: (0, 0))],
        out_specs=_row_spec(D_MODEL),
        out_shape=jax.ShapeDtypeStruct((bsz, rows, D_MODEL), F32),
        compiler_params=_params(2),
        name="mlp_ln",
    )(xs, mods, w1, w2, ln_g.reshape(1, D_MODEL), ln_b.reshape(1, D_MODEL))


def _rope_tables(n_latent, head_dim):
    rows = n_latent // GRID_W
    row = jnp.repeat(jnp.arange(rows), GRID_W).astype(F32)
    col = jnp.tile(jnp.arange(GRID_W), rows).astype(F32)
    n_freq = head_dim // 4
    inv_freq = ROPE_THETA ** (-jnp.arange(n_freq, dtype=F32) / n_freq)
    ang = jnp.concatenate([row[:, None] * inv_freq, col[:, None] * inv_freq], -1)
    cos = jnp.repeat(jnp.cos(ang), 2, axis=-1)
    sin = jnp.repeat(jnp.sin(ang), 2, axis=-1)
    sign = jnp.tile(jnp.array([-1.0, 1.0], F32), head_dim // 2)
    sin = sin * sign
    reps = LANE // head_dim
    cos = jnp.tile(cos, (1, reps))
    sin = jnp.tile(sin, (1, reps))
    cos = jnp.concatenate([jnp.ones((CTX_LEN, LANE), F32), cos], 0)
    sin = jnp.concatenate([jnp.zeros((CTX_LEN, LANE), F32), sin], 0)
    return cos, sin


ATTN_KEY_CHUNKS = 3


def _rope(x, cos, sin_signed):
    lane = lax.broadcasted_iota(jnp.int32, x.shape, 1)
    nxt = pltpu.roll(x, LANE - 1, 1)
    prv = pltpu.roll(x, 1, 1)
    swapped = jnp.where(lane % 2 == 0, nxt, prv)
    return x * cos + swapped * sin_signed


def _rms(x, g):
    return x * lax.rsqrt(jnp.mean(x * x, -1, keepdims=True) + 1e-6) * g


def _attn_kernel(q_ref, qn_ref, qnn_ref, k_ref, v_ref, cos_ref, sin_ref, qg_ref, kg_ref, lam_ref, ng_ref, o_ref,
                 kb_ref, vt_ref, s_even_ref, s_odd_ref, qs_ref, *, n_maps, qk_norm, scale, lambda_init):
    g = pl.program_id(2)
    qt = pl.program_id(3)
    lt = kb_ref.shape[0]
    n_qt = lt // TM

    @pl.when((g == 0) & (qt == 0))
    def _prep_kv():
        k = k_ref[0]
        if qk_norm:
            k = _rms(k, kg_ref[...])
        kb_ref[...] = _rope(k, cos_ref[...], sin_ref[...]).astype(BF16)
        vt_ref[...] = v_ref[0].T.astype(BF16)

    def queries(ref, tile):
        q = ref[0]
        if qk_norm:
            q = _rms(q, qg_ref[...])
        row0 = pl.multiple_of(tile * TM, TM)
        q = _rope(q, cos_ref[pl.ds(row0, TM), :], sin_ref[pl.ds(row0, TM), :]) * (scale * math.log2(math.e))
        return q.astype(BF16)

    def query_maps(qb):
        if n_maps == 1:
            return (qb,)
        lane = lax.broadcasted_iota(jnp.int32, qb.shape, 1)
        zero = jnp.zeros_like(qb)
        return (jnp.where(lane < LANE // 2, qb, zero), jnp.where(lane >= LANE // 2, qb, zero))

    def scores_t(qm, rows):
        return lax.dot_general(kb_ref[rows, :], qm, (((1,), (1,)), ((), ())), preferred_element_type=F32)

    def finish(acc, l):
        if n_maps == 1:
            return (acc[0] * (1.0 / l[0])).T
        lp = lam_ref[...]
        lam = (jnp.exp(jnp.sum(lp[0:1, :] * lp[1:2, :])) - jnp.exp(jnp.sum(lp[2:3, :] * lp[3:4, :]))
               + lambda_init)
        o = (acc[0] * (1.0 / l[0]) - acc[1] * (lam / l[1])).T
        return _rms(o, ng_ref[...]) * (1.0 - lambda_init)

    @pl.when(qt == 0)
    def _first():
        ctx_keys = slice(0, CTX_LEN)
        acc, l = [], []
        for qm in query_maps(queries(q_ref, 0)):
            s = scores_t(qm, ctx_keys)
            e = jnp.exp2(s - jnp.max(s, 0, keepdims=True))
            l.append(jnp.sum(e, 0, keepdims=True))
            acc.append(jnp.dot(vt_ref[:, ctx_keys], e.astype(BF16), preferred_element_type=F32))
        o_ref[0] = finish(acc, l).astype(o_ref.dtype)
        for i, qm in enumerate(query_maps(queries(qn_ref, 1))):
            s_odd_ref[i] = scores_t(qm, slice(0, lt))
        qs_ref[...] = queries(qnn_ref, jnp.minimum(2, n_qt - 1))

    def steady(cur_ref, nxt_ref):
        ch = lt // ATTN_KEY_CHUNKS
        m = [jnp.max(cur_ref[i], 0, keepdims=True) for i in range(n_maps)]
        qn_maps = query_maps(qs_ref[...])
        l = [jnp.zeros((1, TM), F32) for _ in range(n_maps)]
        acc = [jnp.zeros((LANE, TM), F32) for _ in range(n_maps)]
        for j in range(ATTN_KEY_CHUNKS):
            rows = slice(j * ch, (j + 1) * ch)
            for i in range(n_maps):
                nxt_ref[i, rows, :] = scores_t(qn_maps[i], rows)
            for i in range(n_maps):
                e = jnp.exp2(cur_ref[i, rows, :] - m[i])
                l[i] = l[i] + jnp.sum(e, 0, keepdims=True)
                acc[i] = acc[i] + jnp.dot(vt_ref[:, rows], e.astype(BF16), preferred_element_type=F32)
            if j == 0:
                qs_ref[...] = queries(qnn_ref, jnp.minimum(qt + 2, n_qt - 1))
        o_ref[0] = finish(acc, l).astype(o_ref.dtype)

    for parity, (cur_ref, nxt_ref) in enumerate(((s_even_ref, s_odd_ref), (s_odd_ref, s_even_ref))):
        @pl.when((qt > 0) & (qt % 2 == parity))
        def _steady(cur_ref=cur_ref, nxt_ref=nxt_ref):
            steady(cur_ref, nxt_ref)


def _attention(p, cos, sin, q_gain, k_gain, lam_p, norm_g, *, n_kv, group, q_col, k_col, v_col,
               n_maps, qk_norm, scale, lambda_init):
    bsz, lt, _ = p.shape
    n_qt = lt // TM
    kernel = functools.partial(_attn_kernel, n_maps=n_maps, qk_norm=qk_norm, scale=scale,
                               lambda_init=lambda_init)
    const = lambda b, kv, g, qt: (0, 0)
    return pl.pallas_call(
        kernel,
        grid=(bsz, n_kv, group, n_qt),
        in_specs=[pl.BlockSpec((1, TM, LANE), lambda b, kv, g, qt: (b, qt, q_col + kv * group + g)),
                  pl.BlockSpec((1, TM, LANE),
                               lambda b, kv, g, qt: (b, jnp.minimum(qt + 1, n_qt - 1), q_col + kv * group + g)),
                  pl.BlockSpec((1, TM, LANE),
                               lambda b, kv, g, qt: (b, jnp.minimum(qt + 2, n_qt - 1), q_col + kv * group + g)),
                  pl.BlockSpec((1, lt, LANE), lambda b, kv, g, qt: (b, 0, k_col + kv)),
                  pl.BlockSpec((1, lt, LANE), lambda b, kv, g, qt: (b, 0, v_col + kv)),
                  pl.BlockSpec((lt, LANE), const),
                  pl.BlockSpec((lt, LANE), const),
                  pl.BlockSpec((1, LANE), const),
                  pl.BlockSpec((1, LANE), const),
                  pl.BlockSpec(lam_p.shape, const),
                  pl.BlockSpec((1, LANE), const)],
        out_specs=pl.BlockSpec((1, TM, LANE), lambda b, kv, g, qt: (b, qt, kv * group + g)),
        out_shape=jax.ShapeDtypeStruct((bsz, lt, n_kv * group * LANE), BF16),
        scratch_shapes=[pltpu.VMEM((lt, LANE), BF16), pltpu.VMEM((LANE, lt), BF16),
                        pltpu.VMEM((n_maps, lt, TM), F32), pltpu.VMEM((n_maps, lt, TM), F32),
                        pltpu.VMEM((TM, LANE), BF16)],
        compiler_params=_params(4),
        name="attention",
    )(p, p, p, p, p, cos, sin, q_gain.reshape(1, LANE), k_gain.reshape(1, LANE), lam_p, norm_g.reshape(1, LANE))


def _s5_out_kernel(y_ref, x_ref, mod_ref, d_ref, w_ref, g_ref, b_ref, o_ref):
    sh = mod_ref[0, 0:1, :]
    sc = mod_ref[0, 1:2, :]
    gate = mod_ref[0, 2:3, :]
    x = x_ref[0]
    u = x * (1.0 + sc) + sh
    y = jax.nn.gelu(y_ref[0] + d_ref[...] * u).astype(BF16)
    z = jnp.dot(y, w_ref[...], preferred_element_type=F32)
    out = z[:, :D_MODEL] * jax.nn.sigmoid(z[:, D_MODEL:])
    o_ref[0] = _layer_norm(DEEPNORM_ALPHA * x + gate * out, g_ref[...], b_ref[...])


def _s5_out(y, xs, mods, d_skip, w_glu, ln_g, ln_b, layer, latent_only=False):
    bsz, lt, _ = xs.shape
    skip = 1 if latent_only else 0
    return pl.pallas_call(
        _s5_out_kernel,
        grid=(bsz, lt // TM - skip),
        in_specs=[_row_spec(D_MODEL, skip),
                  _row_spec(D_MODEL, skip),
                  _mod_spec(layer, skip),
                  pl.BlockSpec((1, D_MODEL), lambda b, r: (0, 0)),
                  pl.BlockSpec((D_MODEL, 2 * D_MODEL), lambda b, r: (0, 0)),
                  pl.BlockSpec((1, D_MODEL), lambda b, r: (0, 0)),
                  pl.BlockSpec((1, D_MODEL), lambda b, r: (0, 0))],
        out_specs=_row_spec(D_MODEL),
        out_shape=jax.ShapeDtypeStruct((bsz, lt - skip * TM, D_MODEL), F32),
        compiler_params=_params(2),
        name="s5_out_ln",
    )(y, xs, mods, d_skip.reshape(1, D_MODEL), w_glu, ln_g.reshape(1, D_MODEL), ln_b.reshape(1, D_MODEL))


DN_BLK = 128
DN_PREP_BLOCKS = 3
DN_HALO = 8
DN_GATE_BETA_F, DN_GATE_BETA_B, DN_GATE_A_F, DN_GATE_A_B = 0, DN_HEADS, 2 * DN_HEADS, 3 * DN_HEADS
_NT = (((1,), (1,)), ((), ()))
_TN = (((0,), (0,)), ((), ()))


def _dn_kernel(q_ref, k_ref, v_ref, z_ref, g_ref, cq_ref, ck_ref, cv_ref, alog_ref, dtb_ref, ng_ref, o_ref,
               beta_ref, gc_ref, tot_ref, gct_ref, qn_ref, kn_ref, vn_ref, pad_ref,
               wq_f, u_f, qk_f, kd_f, gl_f, o_f, wq_b, u_b, qk_b, kd_b, gl_b, o_b, *, n_blk, n_ctx_blk):
    h = pl.program_id(1)
    lt = n_blk * DN_BLK
    ctx_rows = n_ctx_blk * DN_BLK
    row = lax.broadcasted_iota(jnp.int32, (lt, 1), 0)
    lane = lax.broadcasted_iota(jnp.int32, (1, LANE), 1)

    @pl.when(h == 0)
    def _gates():
        gts = g_ref[0]
        beta_ref[...] = jax.nn.sigmoid(gts)
        g = -jnp.exp(alog_ref[...]) * jax.nn.softplus(gts + dtb_ref[...])
        pos = row % DN_BLK
        pre = g
        suf = g
        s = 1
        while s < DN_BLK:
            pre = pre + jnp.where(pos >= s, pltpu.roll(pre, s, 0), 0.0)
            suf = suf + jnp.where(pos < DN_BLK - s, pltpu.roll(suf, lt - s, 0), 0.0)
            s *= 2
        gc = jnp.where(lane >= DN_GATE_A_B, suf, pre)
        gc_ref[...] = gc
        tot_ref[...] = pre + suf - g
        for c in range(n_blk):
            gct_ref[c] = gc[c * DN_BLK:(c + 1) * DN_BLK, :].T

    def conv_silu(x_ref, w_ref):
        pad_ref[0:DN_HALO, :] = jnp.zeros((DN_HALO, LANE), F32)
        pad_ref[DN_HALO + ctx_rows:2 * DN_HALO + ctx_rows, :] = jnp.zeros((DN_HALO, LANE), F32)
        pad_ref[2 * DN_HALO + lt:3 * DN_HALO + lt, :] = jnp.zeros((DN_HALO, LANE), F32)
        pad_ref[DN_HALO:DN_HALO + ctx_rows, :] = x_ref[0, 0:ctx_rows, :]
        pad_ref[2 * DN_HALO + ctx_rows:2 * DN_HALO + lt, :] = x_ref[0, ctx_rows:lt, :]
        w = w_ref[...]
        parts = []
        for base, n in ((DN_HALO, ctx_rows), (2 * DN_HALO + ctx_rows, lt - ctx_rows)):
            acc = None
            for k in range(DN_CONV):
                tap = pad_ref[base + k - DN_CONV // 2:base + k - DN_CONV // 2 + n, :] * w[k:k + 1, :]
                acc = tap if acc is None else acc + tap
            parts.append(acc * jax.nn.sigmoid(acc))
        return jnp.concatenate(parts, 0)

    def l2n(x):
        return x * lax.rsqrt(jnp.sum(x * x, -1, keepdims=True) + 1e-6)

    qn_ref[...] = l2n(conv_silu(q_ref, cq_ref)) * DN_HEAD_DIM ** -0.5
    kn_ref[...] = l2n(conv_silu(k_ref, ck_ref))
    vn_ref[...] = conv_silu(v_ref, cv_ref)

    ii = lax.broadcasted_iota(jnp.int32, (DN_BLK, DN_BLK), 0)
    jj = lax.broadcasted_iota(jnp.int32, (DN_BLK, DN_BLK), 1)
    fwd =(DN_GATE_BETA_F, DN_GATE_A_F, ii >= jj, ii > jj, (wq_f, u_f, qk_f, kd_f, gl_f, o_f))
    bwd = (DN_GATE_BETA_B, DN_GATE_A_B, ii <= jj, ii < jj, (wq_b, u_b, qk_b, kd_b, gl_b, o_b))

    ii2 = lax.broadcasted_iota(jnp.int32, (2 * DN_BLK, 2 * DN_BLK), 0)
    jj2 = lax.broadcasted_iota(jnp.int32, (2 * DN_BLK, 2 * DN_BLK), 1)
    eye2 = (ii2 == jj2).astype(F32)
    zero_blk = jnp.zeros((DN_BLK, DN_BLK), F32)

    def pair_off(s):
        return ((ii2 // (2 * s)) == (jj2 // (2 * s))) & ((ii2 // s) != (jj2 // s))

    def column(ref, rows, lane_idx):
        return jnp.sum(jnp.where(lane == lane_idx, ref[rows, :], 0.0), -1, keepdims=True)

    def block_inputs(c):
        rows = pl.ds(pl.multiple_of(c * DN_BLK, DN_BLK), DN_BLK)
        qc, kc, vc = qn_ref[rows, :], kn_ref[rows, :], vn_ref[rows, :]
        kb = kc.astype(BF16)
        kk = lax.dot_general(kb, kb, _NT, preferred_element_type=F32)
        qk = lax.dot_general(qc.astype(BF16), kb, _NT, preferred_element_type=F32)
        a_dir, rhs_dir = [], []
        for lane_beta, lane_g, incl, strict, (wq_ref, _, qk_ref, kd_ref, gl_ref, _) in (fwd, bwd):
            bcol = column(beta_ref, rows, lane_beta + h)
            gcol = column(gc_ref, rows, lane_g + h)
            tcol = column(tot_ref, rows, lane_g + h)
            grow = gct_ref[c, pl.ds(lane_g + h, 1), :]
            dec = jnp.exp(jnp.where(incl, gcol - grow, -jnp.inf))
            a_dir.append(jnp.where(strict, bcol * kk * dec, 0.0))
            egc = jnp.exp(gcol)
            rhs_dir.append(jnp.concatenate([bcol * vc, (bcol * egc) * kc], -1).astype(BF16))
            wq_ref[c, DN_BLK:2 * DN_BLK, :] = (qc * egc).astype(BF16)
            qk_ref[c] = (qk * dec).astype(BF16)
            kd_ref[c] = (kc * jnp.exp(tcol - gcol)).T.astype(BF16)
            gl_ref[c] = jnp.broadcast_to(jnp.exp(tcol[0:1, :]), (8, LANE))
        a = jnp.concatenate([jnp.concatenate([a_dir[0], zero_blk], 1),
                             jnp.concatenate([zero_blk, a_dir[1]], 1)], 0)
        return a, jnp.concatenate(rhs_dir, 0)

    def prepare(it, _):
        blocks = [it * DN_PREP_BLOCKS + i for i in range(DN_PREP_BLOCKS)]
        a_rhs = [block_inputs(c) for c in blocks]
        t_inv, m = [], []
        for a, _ in a_rhs:
            a_off = jnp.where(pair_off(1), a, 0.0)
            t_inv.append(eye2 - a_off)
            neighbour = jnp.where(ii2 < DN_BLK, pltpu.roll(a, 1, 0), pltpu.roll(a, 2 * DN_BLK - 1, 0))
            m.append(a - jnp.sum(a_off, -1, keepdims=True) * neighbour)
        s = 2
        while s < DN_BLK:
            for i in range(DN_PREP_BLOCKS):
                m_off = jnp.where(pair_off(s), m[i], 0.0).astype(BF16)
                if 2 * s < DN_BLK:
                    upd = jnp.dot(m_off, jnp.concatenate([t_inv[i], m[i]], 1).astype(BF16),
                                  preferred_element_type=F32)
                    t_inv[i] = t_inv[i] - upd[:, :2 * DN_BLK]
                    m[i] = m[i] - upd[:, 2 * DN_BLK:]
                else:
                    t_inv[i] = t_inv[i] - jnp.dot(m_off, t_inv[i].astype(BF16), preferred_element_type=F32)
            s *= 2
        for i, c in enumerate(blocks):
            uw = jnp.dot(t_inv[i].astype(BF16), a_rhs[i][1], preferred_element_type=F32)
            for d, (wq_ref, u_ref) in enumerate(((wq_f, u_f), (wq_b, u_b))):
                u_ref[c] = uw[d * DN_BLK:(d + 1) * DN_BLK, :DN_HEAD_DIM]
                wq_ref[c, 0:DN_BLK, :] = uw[d * DN_BLK:(d + 1) * DN_BLK, DN_HEAD_DIM:].astype(BF16)
        return 0

    lax.fori_loop(0, n_blk // DN_PREP_BLOCKS, prepare, 0)

    zero_bf = jnp.zeros((DN_BLK, DN_BLK), BF16)

    def block_diag(top, bottom):
        return jnp.concatenate([jnp.concatenate([top, zero_bf], 1), jnp.concatenate([zero_bf, bottom], 1)], 0)

    def step(k, carry):
        s_f, s_b = carry
        cb = jnp.where(k < n_ctx_blk, n_ctx_blk - 1 - k, n_blk + n_ctx_blk - 1 - k)
        hd = DN_HEAD_DIM
        r = jnp.dot(jnp.concatenate([wq_f[k], wq_b[cb]], 1), block_diag(s_f.astype(BF16), s_b.astype(BF16)),
                    preferred_element_type=F32)
        v_new = block_diag((u_f[k] - r[:DN_BLK, :hd]).astype(BF16), (u_b[cb] - r[:DN_BLK, hd:]).astype(BF16))
        lhs = jnp.concatenate([jnp.concatenate([qk_f[k], qk_b[cb]], 1),
                               jnp.concatenate([kd_f[k], kd_b[cb]], 1)], 0)
        r2 = jnp.dot(lhs, v_new, preferred_element_type=F32)
        o_f[k] = r[DN_BLK:, :hd] + r2[:DN_BLK, :hd]
        o_b[cb] = r[DN_BLK:, hd:] + r2[:DN_BLK, hd:]
        return (gl_f[k][0:1, :] * s_f + r2[DN_BLK:, :hd], gl_b[cb][0:1, :] * s_b + r2[DN_BLK:, hd:])

    zero = jnp.zeros((DN_HEAD_DIM, DN_HEAD_DIM), F32)
    lax.fori_loop(0, n_blk, step, (zero, zero))

    o = (o_f[...] + o_b[...]).reshape(lt, DN_HEAD_DIM)
    o = o * lax.rsqrt(jnp.mean(o * o, -1, keepdims=True) + 1e-6) * ng_ref[...]
    z = z_ref[0]
    o_ref[0] = (o * (z * jax.nn.sigmoid(z))).astype(o_ref.dtype)


def _deltanet_core(p, conv_w, a_log, dt_bias, norm_g):
    bsz, lt, _ = p.shape
    n_blk = lt // DN_BLK
    gate_pad = lambda t: jnp.pad(t.reshape(1, 2 * DN_HEADS), ((0, 0), (DN_GATE_A_F, LANE - DN_GATE_A_F - 2 * DN_HEADS)))
    kernel = functools.partial(_dn_kernel, n_blk=n_blk, n_ctx_blk=CTX_LEN // DN_BLK)
    col = lambda off: pl.BlockSpec((1, lt, LANE), lambda b, h: (b, 0, off + h))
    cw = lambda off: pl.BlockSpec((DN_CONV, LANE), lambda b, h: (0, off + h))
    const = pl.BlockSpec((1, LANE), lambda b, h: (0, 0))
    seq = lambda dt: pltpu.VMEM((lt, LANE), dt)
    blk = lambda rows, dt: pltpu.VMEM((n_blk, rows, LANE), dt)
    per_dir = [blk(2 * DN_BLK, BF16), blk(DN_BLK, F32), blk(DN_BLK, BF16), blk(DN_BLK, BF16), blk(8, F32),
               blk(DN_BLK, F32)]
    return pl.pallas_call(
        kernel,
        grid=(bsz, DN_HEADS),
        in_specs=[col(0), col(DN_HEADS), col(2 * DN_HEADS), col(3 * DN_HEADS),
                  pl.BlockSpec((1, lt, LANE), lambda b, h: (b, 0, 4 * DN_HEADS)),
                  cw(0), cw(DN_HEADS), cw(2 * DN_HEADS), const, const, const],
        out_specs=pl.BlockSpec((1, lt, LANE), lambda b, h: (b, 0, h)),
        out_shape=jax.ShapeDtypeStruct((bsz, lt, DN_WIDTH), BF16),
        scratch_shapes=[seq(F32), seq(F32), seq(F32), blk(LANE, F32), seq(F32), seq(F32), seq(F32),
                        pltpu.VMEM((lt + 3 * DN_HALO, LANE), F32)] + per_dir + per_dir,
        compiler_params=_params(2),
        name="deltanet_core",
    )(p, p, p, p, p, conv_w, conv_w, conv_w, gate_pad(a_log), gate_pad(dt_bias), norm_g.reshape(1, LANE))


S5_T = 8
S5_BB = 4
S5_BLK_GROUPS = LANE // SS_GROUP
S5_HALF = S5_BLK_GROUPS * SS_STATE
S5_K = S5_T * LANE


def _s5_weights(a_re, a_im, log_dt, b_re, b_im, c_re, c_im):
    hp = lax.Precision.HIGHEST
    t = S5_T
    nj = SS_GROUPS // S5_BLK_GROUPS
    ar = jnp.minimum(a_re, -1e-4)
    ai = a_im
    dt = jnp.exp(log_dt)[..., None]
    zr, zi = ar * dt, ai * dt
    d = jnp.arange(t + 1, dtype=F32)[:, None, None, None]
    mag = jnp.exp(d * zr)
    er, ei = mag * jnp.cos(d * zi), mag * jnp.sin(d * zi)
    nr, ni = er[1] - 1.0, ei[1]
    den = ar * ar + ai * ai
    fr, fi = (nr * ar + ni * ai) / den, (ni * ar - nr * ai) / den
    bbr = fr[..., None] * b_re - fi[..., None] * b_im
    bbi = fr[..., None] * b_im + fi[..., None] * b_re
    cer = c_re[None] * er[:, :, :, None, :] - c_im[None] * ei[:, :, :, None, :]
    cei = c_re[None] * ei[:, :, :, None, :] + c_im[None] * er[:, :, :, None, :]
    kk = jnp.einsum('tdgiq,dgqj->tdgij', jnp.concatenate([cer, -cei], -1), jnp.concatenate([bbr, bbi], -2),
                    precision=hp)
    s_idx = jnp.arange(t)[:, None]
    t_idx = jnp.arange(t)[None, :]
    lag_f = jnp.clip(t_idx - s_idx, 0, t)
    lag_b = jnp.clip(s_idx - t_idx, 0, t)
    m_f = (t_idx >= s_idx)[:, :, None, None, None]
    m_b = (s_idx >= t_idx)[:, :, None, None, None]
    kst = jnp.where(m_f, kk[lag_f, 0], 0.0) + jnp.where(m_b, kk[lag_b, 1], 0.0)
    kst = kst.reshape(t, t, nj, S5_BLK_GROUPS, SS_GROUP, SS_GROUP).transpose(2, 0, 3, 5, 1, 4)
    m_intra = kst.reshape(nj, S5_K, t * SS_GROUP)

    pow_f = (t - 1 - jnp.arange(t))
    pow_b = jnp.arange(t)

    def carry_w(pw, dd):
        e_r, e_i = er[pw, dd], ei[pw, dd]
        re = e_r[..., None] * bbr[dd][None] - e_i[..., None] * bbi[dd][None]
        im = e_r[..., None] * bbi[dd][None] + e_i[..., None] * bbr[dd][None]
        return jnp.stack([re, im], 0)

    wb = jnp.stack([carry_w(pow_f, 0), carry_w(pow_b, 1)], 0)
    wb = wb.reshape(2, 2, t, nj, S5_BLK_GROUPS, SS_STATE, SS_GROUP)
    m_carry = wb.transpose(3, 2, 4, 6, 0, 1, 5).reshape(nj, S5_K, 4 * SS_STATE)

    def read_w(pw, dd):
        wc = jnp.stack([cer[pw, dd], -cei[pw, dd]], 0)
        wc = wc.reshape(2, t, nj, S5_BLK_GROUPS, SS_GROUP, SS_STATE)
        return wc.transpose(2, 0, 3, 5, 1, 4).reshape(nj, 2 * S5_HALF, t * SS_GROUP)

    m_read_f = read_w(jnp.arange(t) + 1, 0)
    m_read_b = read_w(t - jnp.arange(t), 1)
    lam_t = jnp.stack([er[t, 0], ei[t, 0], er[t, 1], ei[t, 1]], 0)
    lam_t = lam_t.reshape(4, nj, S5_HALF).transpose(1, 0, 2)
    return m_carry.astype(BF16), m_intra.astype(BF16), m_read_f.astype(BF16), m_read_b.astype(BF16), lam_t


def _s5_expand(m, row_group_div, col_seg, n_col):
    rows, n_in = m.shape
    q = lax.broadcasted_iota(jnp.int32, (n_in, n_col), 0)
    c = lax.broadcasted_iota(jnp.int32, (n_in, n_col), 1)
    out_seg = col_seg * S5_BLK_GROUPS
    rep = ((q // col_seg == c // out_seg) & (q % col_seg == c % col_seg)).astype(BF16)
    wide = jnp.dot(m, rep, preferred_element_type=F32)
    g_row = (lax.broadcasted_iota(jnp.int32, (rows, n_col), 0) // row_group_div) % S5_BLK_GROUPS
    g_col = (lax.broadcasted_iota(jnp.int32, (rows, n_col), 1) // col_seg) % S5_BLK_GROUPS
    return jnp.where(g_row == g_col, wide, 0.0).astype(BF16)


def _s5_kernel(x_ref, mod_ref, mc_ref, mi_ref, mrf_ref, mrb_ref, lt_ref, y_ref,
               wb_ref, wi_ref, wcf_ref, wcb_ref, xcat_ref, hf_ref, hb_ref, *, n_chunk, n_ctx_chunk):
    bh = pl.program_id(1)

    @pl.when(bh == 0)
    def _expand_weights():
        wb_ref[...] = _s5_expand(mc_ref[0], SS_GROUP, SS_STATE, 4 * S5_HALF)
        wi_ref[...] = _s5_expand(mi_ref[0], SS_GROUP, SS_GROUP, S5_K)
        wcf_ref[...] = _s5_expand(mrf_ref[0], SS_STATE, SS_GROUP, S5_K)
        wcb_ref[...] = _s5_expand(mrb_ref[0], SS_STATE, SS_GROUP, S5_K)

    nb = x_ref.shape[0]
    n_lb = hf_ref.shape[0]
    half_lb = n_lb // 2
    is_ctx = lax.broadcasted_iota(jnp.int32, (n_chunk, 1), 0) < n_ctx_chunk
    sh_c = mod_ref[8, 0:1, :]
    sc_c = mod_ref[8, 1:2, :]

    def carry_in(bl, _):
        mb = mod_ref[bh * nb + bl]
        sh = jnp.where(is_ctx, sh_c, mb[0:1, :])
        sc1 = 1.0 + jnp.where(is_ctx, sc_c, mb[1:2, :])
        rows = pl.ds(pl.multiple_of(bl * n_chunk, n_chunk), n_chunk)
        for tau in range(S5_T):
            xt = x_ref[bl, pl.ds(tau, n_chunk, stride=S5_T), :]
            xcat_ref[rows, tau * LANE:(tau + 1) * LANE] = (xt * sc1 + sh).astype(BF16)
        hbv = jnp.dot(xcat_ref[rows, :], wb_ref[...], preferred_element_type=F32)
        for l in range(n_lb):
            hf_ref[l, rows, :] = hbv[:, l * LANE:(l + 1) * LANE]
            hb_ref[l, rows, :] = hbv[:, (n_lb + l) * LANE:(n_lb + l + 1) * LANE]
        return 0

    lax.fori_loop(0, nb, carry_in, 0)

    lt = lt_ref[0]

    def lam_blocks(i):
        return [jnp.broadcast_to(lt[i:i + 1, l * LANE:(l + 1) * LANE], (nb, LANE)) for l in range(half_lb)]

    a_f = (lam_blocks(0), lam_blocks(1))
    a_b = (lam_blocks(2), lam_blocks(3))

    def advance(h_ref, c, a, state):
        a_re, a_im = a
        new = []
        for l in range(half_lb):
            sel = pl.ds(c, nb, stride=n_chunk)
            in_re = h_ref[l, sel, :]
            in_im = h_ref[half_lb + l, sel, :]
            s_re, s_im = state[l], state[half_lb + l]
            h_ref[l, sel, :] = s_re
            h_ref[half_lb + l, sel, :] = s_im
            new.append((a_re[l] * s_re - a_im[l] * s_im + in_re, a_re[l] * s_im + a_im[l] * s_re + in_im))
        return tuple(n[0] for n in new) + tuple(n[1] for n in new)

    def step(k, carry):
        st_f, st_b = carry
        cb = jnp.where(k < n_ctx_chunk, n_ctx_chunk - 1 - k, n_chunk + n_ctx_chunk - 1 - k)
        return advance(hf_ref, k, a_f, st_f), advance(hb_ref, cb, a_b, st_b)

    zero = tuple(jnp.zeros((nb, LANE), F32) for _ in range(n_lb))
    lax.fori_loop(0, n_chunk, step, (zero, zero))

    def read_out(bl, _):
        rows = pl.ds(pl.multiple_of(bl * n_chunk, n_chunk), n_chunk)
        h_f = jnp.concatenate([hf_ref[l, rows, :] for l in range(n_lb)], -1).astype(BF16)
        h_b = jnp.concatenate([hb_ref[l, rows, :] for l in range(n_lb)], -1).astype(BF16)
        y = (jnp.dot(xcat_ref[rows, :], wi_ref[...], preferred_element_type=F32)
             + jnp.dot(h_f, wcf_ref[...], preferred_element_type=F32)
             + jnp.dot(h_b, wcb_ref[...], preferred_element_type=F32))
        for t in range(S5_T):
            y_ref[bl, pl.ds(t, n_chunk, stride=S5_T), :] = y[:, t * LANE:(t + 1) * LANE]
        return 0

    lax.fori_loop(0, nb, read_out, 0)


def _s5_core(xs, mods, weights, layer):
    bsz, lt, _ = xs.shape
    m_carry, m_intra, m_read_f, m_read_b, lam_t = weights
    nj = D_MODEL // LANE
    nb = min(S5_BB, bsz)
    n_chunk = lt // S5_T
    kernel = functools.partial(_s5_kernel, n_chunk=n_chunk, n_ctx_chunk=CTX_LEN // S5_T)
    wspec = lambda shape: pl.BlockSpec((1,) + shape, lambda j, bh: (j, 0, 0))
    return pl.pallas_call(
        kernel,
        grid=(nj, bsz // nb),
        in_specs=[pl.BlockSpec((nb, lt, LANE), lambda j, bh: (bh, 0, j)),
                  pl.BlockSpec((MOD_ROWS, 6, LANE), lambda j, bh: (layer, 0, j)),
                  wspec(m_carry.shape[1:]), wspec(m_intra.shape[1:]),
                  wspec(m_read_f.shape[1:]), wspec(m_read_b.shape[1:]),
                  wspec((4, S5_HALF))],
        out_specs=pl.BlockSpec((nb, lt, LANE), lambda j, bh: (bh, 0, j)),
        out_shape=jax.ShapeDtypeStruct((bsz, lt, D_MODEL), F32),
        scratch_shapes=[pltpu.VMEM((S5_K, 4 * S5_HALF), BF16), pltpu.VMEM((S5_K, S5_K), BF16),
                        pltpu.VMEM((2 * S5_HALF, S5_K), BF16), pltpu.VMEM((2 * S5_HALF, S5_K), BF16),
                        pltpu.VMEM((nb * n_chunk, S5_K), BF16),
                        pltpu.VMEM((2 * S5_HALF // LANE, nb * n_chunk, LANE), F32),
                        pltpu.VMEM((2 * S5_HALF // LANE, nb * n_chunk, LANE), F32)],
        compiler_params=_params(2),
        name="s5_core",
    )(xs, mods, m_carry, m_intra, m_read_f, m_read_b, lam_t)


def kernel(x, c, ctx, c_ctx, ada_w, ada_b, ln_g, ln_b, mlp_w1, mlp_w2, dn_w_in, dn_conv, dn_a_log, dn_dt_bias, dn_norm_g, dn_w_out, da_w_qkv, da_lambda, da_norm_g, da_w_out, ga_w_qkv, ga_q_norm, ga_k_norm, ga_w_out, ss_a_re, ss_a_im, ss_log_dt, ss_b_re, ss_b_im, ss_c_re, ss_c_im, ss_d, ss_w_glu):
    bsz, n_latent, _ = x.shape
    xs = jnp.concatenate([ctx, x], 1)
    c_rows = jnp.concatenate([c, c_ctx[None, :], jnp.zeros((MOD_ROWS - bsz - 1, D_MODEL), F32)], 0)
    mods = _ada_all(c_rows, ada_w, ada_b).reshape(DEPTH * MOD_ROWS, 6, D_MODEL)
    ones = jnp.ones((LANE,), F32)
    zeros4 = jnp.zeros((4, DA_HEAD_DIM), F32)

    for i in range(DEPTH):
        m, j = i % 4, i // 4
        last = i == DEPTH - 1
        if m == 0:
            w_in = jnp.pad(dn_w_in[j], ((0, 0), (0, LANE - 4 * DN_HEADS))).astype(BF16)
            p = _proj(xs, mods, w_in, i)
            o = _deltanet_core(p, dn_conv[j], dn_a_log[j], dn_dt_bias[j], dn_norm_g[j])
            w_out = dn_w_out[j]
        elif m == 1:
            lambda_init = 0.8 - 0.6 * math.exp(-0.3 * i)
            p = _proj(xs, mods, da_w_qkv[j].astype(BF16), i)
            cos, sin = _rope_tables(n_latent, DA_HEAD_DIM)
            o = _attention(p, cos, sin, ones, ones, da_lambda[j], da_norm_g[j],
                           n_kv=DA_HEADS, group=1, q_col=0, k_col=DA_HEADS, v_col=2 * DA_HEADS,
                           n_maps=2, qk_norm=False, scale=DA_HEAD_DIM ** -0.5, lambda_init=lambda_init)
            w_out = da_w_out[j]
        elif m == 2:
            p = _proj(xs, mods, ga_w_qkv[j].astype(BF16), i)
            cos, sin = _rope_tables(n_latent, GA_HEAD_DIM)
            o = _attention(p, cos, sin, ga_q_norm[j], ga_k_norm[j], zeros4, ones,
                           n_kv=GA_KV_HEADS, group=GA_HEADS // GA_KV_HEADS, q_col=0, k_col=GA_HEADS,
                           v_col=GA_HEADS + GA_KV_HEADS, n_maps=1, qk_norm=True,
                           scale=GA_HEAD_DIM ** -0.5, lambda_init=0.0)
            w_out = ga_w_out[j]
        w1, w2 = mlp_w1[i].astype(BF16), mlp_w2[i].astype(BF16)
        if m == 3:
            weights = _s5_weights(ss_a_re[j], ss_a_im[j], ss_log_dt[j], ss_b_re[j], ss_b_im[j],
                                  ss_c_re[j], ss_c_im[j])
            y = _s5_core(xs, mods, weights, i)
            xs = _s5_out(y, xs, mods, ss_d[j], ss_w_glu[j].astype(BF16), ln_g[i, 0], ln_b[i, 0], i, last)
            xs = _mlp(xs, mods, w1, w2, ln_g[i, 1], ln_b[i, 1], i, last)
        else:
            xs = _out_mlp(o, xs, mods, w_out.astype(BF16), w1, w2, ln_g[i], ln_b[i], i, last)
    return xs
```

```python
import functools
import math

import jax
import jax.numpy as jnp
from jax import lax
from jax.experimental import pallas as pl
from jax.experimental.pallas import tpu as pltpu

F32 = jnp.float32
BF16 = jnp.bfloat16

D_MODEL = 1024
D_FF = 4 * D_MODEL
DEPTH = 4
GRID_W = 64
CTX_LEN = 256
ROPE_THETA = 10000.0
DEEPNORM_ALPHA = (2 * DEPTH) ** 0.25
TM = 256
LANE = 128
MOD_ROWS = 16
VMEM_LIMIT = 56 * 1024 * 1024

DN_HEADS = 8
DN_HEAD_DIM = 128
DN_WIDTH = DN_HEADS * DN_HEAD_DIM
DN_CONV = 5
DN_CHUNK = 64
DA_HEADS = 8
DA_HEAD_DIM = 64
GA_HEADS = 8
GA_KV_HEADS = 2
GA_HEAD_DIM = 128
SS_GROUP = 16
SS_GROUPS = D_MODEL // SS_GROUP
SS_STATE = 64


def _params(n_axes):
    return pltpu.CompilerParams(dimension_semantics=("arbitrary",) * n_axes,
                                vmem_limit_bytes=VMEM_LIMIT)


def _layer_norm(v, g, b):
    mu = jnp.mean(v, -1, keepdims=True)
    d = v - mu
    var = jnp.mean(d * d, -1, keepdims=True)
    return d * lax.rsqrt(var + 1e-5) * g + b


def _mod_spec(layer, skip=0):
    return pl.BlockSpec((1, 6, D_MODEL),
                        lambda b, r: (layer * MOD_ROWS + jnp.where(r + skip == 0, 8, b), 0, 0))


def _row_spec(width, skip=0):
    return pl.BlockSpec((1, TM, width), lambda b, r: (b, r + skip, 0))


def _ada_kernel(c_ref, w_ref, b_ref, o_ref):
    c = c_ref[...]
    act = (c * jax.nn.sigmoid(c)).astype(BF16)
    o_ref[0] = jnp.dot(act, w_ref[0].astype(BF16), preferred_element_type=F32) + b_ref[0]


def _ada_all(c_rows, ada_w, ada_b):
    tn = 1536
    n = 6 * D_MODEL
    return pl.pallas_call(
        _ada_kernel,
        grid=(DEPTH, n // tn),
        in_specs=[pl.BlockSpec((MOD_ROWS, D_MODEL), lambda i, j: (0, 0)),
                  pl.BlockSpec((1, D_MODEL, tn), lambda i, j: (i, 0, j)),
                  pl.BlockSpec((1, 1, tn), lambda i, j: (i, 0, j))],
        out_specs=pl.BlockSpec((1, MOD_ROWS, tn), lambda i, j: (i, 0, j)),
        out_shape=jax.ShapeDtypeStruct((DEPTH, MOD_ROWS, n), F32),
        compiler_params=_params(2),
        name="ada_mod",
    )(c_rows, ada_w, ada_b.reshape(DEPTH, 1, n))


def _proj_kernel(x_ref, mod_ref, w_ref, o_ref):
    sh = mod_ref[0, 0:1, :]
    sc = mod_ref[0, 1:2, :]
    h = (x_ref[0] * (1.0 + sc) + sh).astype(BF16)
    o_ref[0] = jnp.dot(h, w_ref[...], preferred_element_type=F32).astype(o_ref.dtype)


def _proj(xs, mods, w, layer, out_dtype=F32):
    bsz, lt, _ = xs.shape
    n = w.shape[1]
    return pl.pallas_call(
        _proj_kernel,
        grid=(bsz, lt // TM),
        in_specs=[pl.BlockSpec((1, TM, D_MODEL), lambda b, r: (b, r, 0)),
                  _mod_spec(layer),
                  pl.BlockSpec((D_MODEL, n), lambda b, r: (0, 0))],
        out_specs=pl.BlockSpec((1, TM, n), lambda b, r: (b, r, 0)),
        out_shape=jax.ShapeDtypeStruct((bsz, lt, n), out_dtype),
        compiler_params=_params(2),
        name="mod_proj",
    )(xs, mods, w)


FF_CHUNK = 1024


def _mlp_body(x, mod_ref, w1_ref, w2_ref, g_ref, b_ref, y_ref):
    sh = mod_ref[0, 3:4, :]
    sc = mod_ref[0, 4:5, :]
    gate = mod_ref[0, 5:6, :]
    h = (x * (1.0 + sc) + sh).astype(BF16)
    acc = jnp.zeros((TM, D_MODEL), F32)
    for j in range(D_FF // FF_CHUNK):
        a = jnp.dot(h, w1_ref[:, j * FF_CHUNK:(j + 1) * FF_CHUNK], preferred_element_type=F32)
        a = jnp.square(jnp.maximum(a, 0.0)).astype(BF16)
        acc = acc + jnp.dot(a, w2_ref[j * FF_CHUNK:(j + 1) * FF_CHUNK, :], preferred_element_type=F32)
    y_ref[0] = _layer_norm(DEEPNORM_ALPHA * x + gate * acc, g_ref[...], b_ref[...])


def _mlp_kernel(x_ref, mod_ref, w1_ref, w2_ref, g_ref, b_ref, y_ref):
    _mlp_body(x_ref[0], mod_ref, w1_ref, w2_ref, g_ref, b_ref, y_ref)


def _out_mlp_kernel(o_ref, x_ref, mod_ref, wo_ref, g1_ref, b1_ref, w1_ref, w2_ref, g2_ref, b2_ref, y_ref):
    y = jnp.dot(o_ref[0], wo_ref[...], preferred_element_type=F32)
    x1 = _layer_norm(DEEPNORM_ALPHA * x_ref[0] + mod_ref[0, 2:3, :] * y, g1_ref[...], b1_ref[...])
    _mlp_body(x1, mod_ref, w1_ref, w2_ref, g2_ref, b2_ref, y_ref)


def _out_mlp(o, xs, mods, w_out, w1, w2, ln_g, ln_b, layer, latent_only=False):
    bsz, lt, k = o.shape
    skip = 1 if latent_only else 0
    const = lambda shape: pl.BlockSpec(shape, lambda b, r: (0, 0))
    vec = lambda t: t.reshape(1, D_MODEL)
    return pl.pallas_call(
        _out_mlp_kernel,
        grid=(bsz, lt // TM - skip),
        in_specs=[_row_spec(k, skip), _row_spec(D_MODEL, skip), _mod_spec(layer, skip),
                  const((k, D_MODEL)), const((1, D_MODEL)), const((1, D_MODEL)),
                  const((D_MODEL, D_FF)), const((D_FF, D_MODEL)), const((1, D_MODEL)), const((1, D_MODEL))],
        out_specs=_row_spec(D_MODEL),
        out_shape=jax.ShapeDtypeStruct((bsz, lt - skip * TM, D_MODEL), F32),
        compiler_params=_params(2),
        name="out_mlp_ln",
    )(o, xs, mods, w_out, vec(ln_g[0]), vec(ln_b[0]), w1, w2, vec(ln_g[1]), vec(ln_b[1]))


def _mlp(xs, mods, w1, w2, ln_g, ln_b, layer, latent_only=False):
    bsz, rows, _ = xs.shape
    skip = 1 if latent_only else 0
    return pl.pallas_call(
        _mlp_kernel,
        grid=(bsz, rows // TM),
        in_specs=[_row_spec(D_MODEL),
                  _mod_spec(layer, skip),
                  pl.BlockSpec((D_MODEL, D_FF), lambda b, r: (0, 0)),
                  pl.BlockSpec((D_FF, D_MODEL), lambda b, r: (0, 0)),
                  pl.BlockSpec((1, D_MODEL), lambda b, r: (0, 0)),
                  pl.BlockSpec((1, D_MODEL), lambda b, r: (0, 0))],
        out_specs=_row_spec(D_MODEL),
        out_shape=jax.ShapeDtypeStruct((bsz, rows, D_MODEL), F32),
        compiler_params=_params(2),
        name="mlp_ln",
    )(xs, mods, w1, w2, ln_g.reshape(1, D_MODEL), ln_b.reshape(1, D_MODEL))


def _rope_tables(n_latent, head_dim):
    rows = n_latent // GRID_W
    row = jnp.repeat(jnp.arange(rows), GRID_W).astype(F32)
    col = jnp.tile(jnp.arange(GRID_W), rows).astype(F32)
    n_freq = head_dim // 4
    inv_freq = ROPE_THETA ** (-jnp.arange(n_freq, dtype=F32) / n_freq)
    ang = jnp.concatenate([row[:, None] * inv_freq, col[:, None] * inv_freq], -1)
    cos = jnp.repeat(jnp.cos(ang), 2, axis=-1)
    sin = jnp.repeat(jnp.sin(ang), 2, axis=-1)
    sign = jnp.tile(jnp.array([-1.0, 1.0], F32), head_dim // 2)
    sin = sin * sign
    reps = LANE // head_dim
    cos = jnp.tile(cos, (1, reps))
    sin = jnp.tile(sin, (1, reps))
    cos = jnp.concatenate([jnp.ones((CTX_LEN, LANE), F32), cos], 0)
    sin = jnp.concatenate([jnp.zeros((CTX_LEN, LANE), F32), sin], 0)
    return cos, sin


ATTN_KEY_CHUNKS = 3


def _rope(x, cos, sin_signed):
    lane = lax.broadcasted_iota(jnp.int32, x.shape, 1)
    nxt = pltpu.roll(x, LANE - 1, 1)
    prv = pltpu.roll(x, 1, 1)
    swapped = jnp.where(lane % 2 == 0, nxt, prv)
    return x * cos + swapped * sin_signed


def _rms(x, g):
    return x * lax.rsqrt(jnp.mean(x * x, -1, keepdims=True) + 1e-6) * g


def _attn_kernel(q_ref, qn_ref, qnn_ref, k_ref, v_ref, cos_ref, sin_ref, qg_ref, kg_ref, lam_ref, ng_ref, o_ref,
                 kb_ref, vt_ref, s_even_ref, s_odd_ref, qs_ref, *, n_maps, qk_norm, scale, lambda_init):
    g = pl.program_id(2)
    qt = pl.program_id(3)
    lt = kb_ref.shape[0]
    n_qt = lt // TM

    @pl.when((g == 0) & (qt == 0))
    def _prep_kv():
        k = k_ref[0]
        if qk_norm:
            k = _rms(k, kg_ref[...])
        kb_ref[...] = _rope(k, cos_ref[...], sin_ref[...]).astype(BF16)
        vt_ref[...] = v_ref[0].T.astype(BF16)

    def queries(ref, tile):
        q = ref[0]
        if qk_norm:
            q = _rms(q, qg_ref[...])
        row0 = pl.multiple_of(tile * TM, TM)
        q = _rope(q, cos_ref[pl.ds(row0, TM), :], sin_ref[pl.ds(row0, TM), :]) * (scale * math.log2(math.e))
        return q.astype(BF16)

    def query_maps(qb):
        if n_maps == 1:
            return (qb,)
        lane = lax.broadcasted_iota(jnp.int32, qb.shape, 1)
        zero = jnp.zeros_like(qb)
        return (jnp.where(lane < LANE // 2, qb, zero), jnp.where(lane >= LANE // 2, qb, zero))

    def scores_t(qm, rows):
        return lax.dot_general(kb_ref[rows, :], qm, (((1,), (1,)), ((), ())), preferred_element_type=F32)

    def finish(acc, l):
        if n_maps == 1:
            return (acc[0] * (1.0 / l[0])).T
        lp = lam_ref[...]
        lam = (jnp.exp(jnp.sum(lp[0:1, :] * lp[1:2, :])) - jnp.exp(jnp.sum(lp[2:3, :] * lp[3:4, :]))
               + lambda_init)
        o = (acc[0] * (1.0 / l[0]) - acc[1] * (lam / l[1])).T
        return _rms(o, ng_ref[...]) * (1.0 - lambda_init)

    @pl.when(qt == 0)
    def _first():
        ctx_keys = slice(0, CTX_LEN)
        acc, l = [], []
        for qm in query_maps(queries(q_ref, 0)):
            s = scores_t(qm, ctx_keys)
            e = jnp.exp2(s - jnp.max(s, 0, keepdims=True))
            l.append(jnp.sum(e, 0, keepdims=True))
            acc.append(jnp.dot(vt_ref[:, ctx_keys], e.astype(BF16), preferred_element_type=F32))
        o_ref[0] = finish(acc, l).astype(o_ref.dtype)
        for i, qm in enumerate(query_maps(queries(qn_ref, 1))):
            s_odd_ref[i] = scores_t(qm, slice(0, lt))
        qs_ref[...] = queries(qnn_ref, jnp.minimum(2, n_qt - 1))

    def steady(cur_ref, nxt_ref):
        ch = lt // ATTN_KEY_CHUNKS
        m = [jnp.max(cur_ref[i], 0, keepdims=True) for i in range(n_maps)]
        qn_maps = query_maps(qs_ref[...])
        l = [jnp.zeros((1, TM), F32) for _ in range(n_maps)]
        acc = [jnp.zeros((LANE, TM), F32) for _ in range(n_maps)]
        for j in range(ATTN_KEY_CHUNKS):
            rows = slice(j * ch, (j + 1) * ch)
            for i in range(n_maps):
                nxt_ref[i, rows, :] = scores_t(qn_maps[i], rows)
            for i in range(n_maps):
                e = jnp.exp2(cur_ref[i, rows, :] - m[i])
                l[i] = l[i] + jnp.sum(e, 0, keepdims=True)
                acc[i] = acc[i] + jnp.dot(vt_ref[:, rows], e.astype(BF16), preferred_element_type=F32)
            if j == 0:
                qs_ref[...] = queries(qnn_ref, jnp.minimum(qt + 2, n_qt - 1))
        o_ref[0] = finish(acc, l).astype(o_ref.dtype)

    for parity, (cur_ref, nxt_ref) in enumerate(((s_even_ref, s_odd_ref), (s_odd_ref, s_even_ref))):
        @pl.when((qt > 0) & (qt % 2 == parity))
        def _steady(cur_ref=cur_ref, nxt_ref=nxt_ref):
            steady(cur_ref, nxt_ref)


def _attention(p, cos, sin, q_gain, k_gain, lam_p, norm_g, *, n_kv, group, q_col, k_col, v_col,
               n_maps, qk_norm, scale, lambda_init):
    bsz, lt, _ = p.shape
    n_qt = lt // TM
    kernel = functools.partial(_attn_kernel, n_maps=n_maps, qk_norm=qk_norm, scale=scale,
                               lambda_init=lambda_init)
    const = lambda b, kv, g, qt: (0, 0)
    return pl.pallas_call(
        kernel,
        grid=(bsz, n_kv, group, n_qt),
        in_specs=[pl.BlockSpec((1, TM, LANE), lambda b, kv, g, qt: (b, qt, q_col + kv * group + g)),
                  pl.BlockSpec((1, TM, LANE),
                               lambda b, kv, g, qt: (b, jnp.minimum(qt + 1, n_qt - 1), q_col + kv * group + g)),
                  pl.BlockSpec((1, TM, LANE),
                               lambda b, kv, g, qt: (b, jnp.minimum(qt + 2, n_qt - 1), q_col + kv * group + g)),
                  pl.BlockSpec((1, lt, LANE), lambda b, kv, g, qt: (b, 0, k_col + kv)),
                  pl.BlockSpec((1, lt, LANE), lambda b, kv, g, qt: (b, 0, v_col + kv)),
                  pl.BlockSpec((lt, LANE), const),
                  pl.BlockSpec((lt, LANE), const),
                  pl.BlockSpec((1, LANE), const),
                  pl.BlockSpec((1, LANE), const),
                  pl.BlockSpec(lam_p.shape, const),
                  pl.BlockSpec((1, LANE), const)],
        out_specs=pl.BlockSpec((1, TM, LANE), lambda b, kv, g, qt: (b, qt, kv * group + g)),
        out_shape=jax.ShapeDtypeStruct((bsz, lt, n_kv * group * LANE), BF16),
        scratch_shapes=[pltpu.VMEM((lt, LANE), BF16), pltpu.VMEM((LANE, lt), BF16),
                        pltpu.VMEM((n_maps, lt, TM), F32), pltpu.VMEM((n_maps, lt, TM), F32),
                        pltpu.VMEM((TM, LANE), BF16)],
        compiler_params=_params(4),
        name="attention",
    )(p, p, p, p, p, cos, sin, q_gain.reshape(1, LANE), k_gain.reshape(1, LANE), lam_p, norm_g.reshape(1, LANE))


def _s5_out_kernel(y_ref, x_ref, mod_ref, d_ref, w_ref, g_ref, b_ref, o_ref):
    sh = mod_ref[0, 0:1, :]
    sc = mod_ref[0, 1:2, :]
    gate = mod_ref[0, 2:3, :]
    x = x_ref[0]
    u = x * (1.0 + sc) + sh
    y = jax.nn.gelu(y_ref[0] + d_ref[...] * u).astype(BF16)
    z = jnp.dot(y, w_ref[...], preferred_element_type=F32)
    out = z[:, :D_MODEL] * jax.nn.sigmoid(z[:, D_MODEL:])
    o_ref[0] = _layer_norm(DEEPNORM_ALPHA * x + gate * out, g_ref[...], b_ref[...])


def _s5_out(y, xs, mods, d_skip, w_glu, ln_g, ln_b, layer, latent_only=False):
    bsz, lt, _ = xs.shape
    skip = 1 if latent_only else 0
    return pl.pallas_call(
        _s5_out_kernel,
        grid=(bsz, lt // TM - skip),
        in_specs=[_row_spec(D_MODEL, skip),
                  _row_spec(D_MODEL, skip),
                  _mod_spec(layer, skip),
                  pl.BlockSpec((1, D_MODEL), lambda b, r: (0, 0)),
                  pl.BlockSpec((D_MODEL, 2 * D_MODEL), lambda b, r: (0, 0)),
                  pl.BlockSpec((1, D_MODEL), lambda b, r: (0, 0)),
                  pl.BlockSpec((1, D_MODEL), lambda b, r: (0, 0))],
        out_specs=_row_spec(D_MODEL),
        out_shape=jax.ShapeDtypeStruct((bsz, lt - skip * TM, D_MODEL), F32),
        compiler_params=_params(2),
        name="s5_out_ln",
    )(y, xs, mods, d_skip.reshape(1, D_MODEL), w_glu, ln_g.reshape(1, D_MODEL), ln_b.reshape(1, D_MODEL))


DN_BLK = 128
DN_PREP_BLOCKS = 3
DN_HALO = 8
DN_GATE_BETA_F, DN_GATE_BETA_B, DN_GATE_A_F, DN_GATE_A_B = 0, DN_HEADS, 2 * DN_HEADS, 3 * DN_HEADS
_NT = (((1,), (1,)), ((), ()))
_TN = (((0,), (0,)), ((), ()))


def _dn_kernel(q_ref, k_ref, v_ref, z_ref, g_ref, cq_ref, ck_ref, cv_ref, alog_ref, dtb_ref, ng_ref, o_ref,
               beta_ref, gc_ref, tot_ref, gct_ref, qn_ref, kn_ref, vn_ref, pad_ref,
               wq_f, u_f, qk_f, kd_f, gl_f, o_f, wq_b, u_b, qk_b, kd_b, gl_b, o_b, *, n_blk, n_ctx_blk):
    h = pl.program_id(1)
    lt = n_blk * DN_BLK
    ctx_rows = n_ctx_blk * DN_BLK
    row = lax.broadcasted_iota(jnp.int32, (lt, 1), 0)
    lane = lax.broadcasted_iota(jnp.int32, (1, LANE), 1)

    @pl.when(h == 0)
    def _gates():
        gts = g_ref[0]
        beta_ref[...] = jax.nn.sigmoid(gts)
        g = -jnp.exp(alog_ref[...]) * jax.nn.softplus(gts + dtb_ref[...])
        pos = row % DN_BLK
        pre = g
        suf = g
        s = 1
        while s < DN_BLK:
            pre = pre + jnp.where(pos >= s, pltpu.roll(pre, s, 0), 0.0)
            suf = suf + jnp.where(pos < DN_BLK - s, pltpu.roll(suf, lt - s, 0), 0.0)
            s *= 2
        gc = jnp.where(lane >= DN_GATE_A_B, suf, pre)
        gc_ref[...] = gc
        tot_ref[...] = pre + suf - g
        for c in range(n_blk):
            gct_ref[c] = gc[c * DN_BLK:(c + 1) * DN_BLK, :].T

    def conv_silu(x_ref, w_ref):
        pad_ref[0:DN_HALO, :] = jnp.zeros((DN_HALO, LANE), F32)
        pad_ref[DN_HALO + ctx_rows:2 * DN_HALO + ctx_rows, :] = jnp.zeros((DN_HALO, LANE), F32)
        pad_ref[2 * DN_HALO + lt:3 * DN_HALO + lt, :] = jnp.zeros((DN_HALO, LANE), F32)
        pad_ref[DN_HALO:DN_HALO + ctx_rows, :] = x_ref[0, 0:ctx_rows, :]
        pad_ref[2 * DN_HALO + ctx_rows:2 * DN_HALO + lt, :] = x_ref[0, ctx_rows:lt, :]
        w = w_ref[...]
        parts = []
        for base, n in ((DN_HALO, ctx_rows), (2 * DN_HALO + ctx_rows, lt - ctx_rows)):
            acc = None
            for k in range(DN_CONV):
                tap = pad_ref[base + k - DN_CONV // 2:base + k - DN_CONV // 2 + n, :] * w[k:k + 1, :]
                acc = tap if acc is None else acc + tap
            parts.append(acc * jax.nn.sigmoid(acc))
        return jnp.concatenate(parts, 0)

    def l2n(x):
        return x * lax.rsqrt(jnp.sum(x * x, -1, keepdims=True) + 1e-6)

    qn_ref[...] = l2n(conv_silu(q_ref, cq_ref)) * DN_HEAD_DIM ** -0.5
    kn_ref[...] = l2n(conv_silu(k_ref, ck_ref))
    vn_ref[...] = conv_silu(v_ref, cv_ref)

    ii = lax.broadcasted_iota(jnp.int32, (DN_BLK, DN_BLK), 0)
    jj = lax.broadcasted_iota(jnp.int32, (DN_BLK, DN_BLK), 1)
    fwd =(DN_GATE_BETA_F, DN_GATE_A_F, ii >= jj, ii > jj, (wq_f, u_f, qk_f, kd_f, gl_f, o_f))
    bwd = (DN_GATE_BETA_B, DN_GATE_A_B, ii <= jj, ii < jj, (wq_b, u_b, qk_b, kd_b, gl_b, o_b))

    ii2 = lax.broadcasted_iota(jnp.int32, (2 * DN_BLK, 2 * DN_BLK), 0)
    jj2 = lax.broadcasted_iota(jnp.int32, (2 * DN_BLK, 2 * DN_BLK), 1)
    eye2 = (ii2 == jj2).astype(F32)
    zero_blk = jnp.zeros((DN_BLK, DN_BLK), F32)

    def pair_off(s):
        return ((ii2 // (2 * s)) == (jj2 // (2 * s))) & ((ii2 // s) != (jj2 // s))

    def column(ref, rows, lane_idx):
        return jnp.sum(jnp.where(lane == lane_idx, ref[rows, :], 0.0), -1, keepdims=True)

    def block_inputs(c):
        rows = pl.ds(pl.multiple_of(c * DN_BLK, DN_BLK), DN_BLK)
        qc, kc, vc = qn_ref[rows, :], kn_ref[rows, :], vn_ref[rows, :]
        kb = kc.astype(BF16)
        kk = lax.dot_general(kb, kb, _NT, preferred_element_type=F32)
        qk = lax.dot_general(qc.astype(BF16), kb, _NT, preferred_element_type=F32)
        a_dir, rhs_dir = [], []
        for lane_beta, lane_g, incl, strict, (wq_ref, _, qk_ref, kd_ref, gl_ref, _) in (fwd, bwd):
            bcol = column(beta_ref, rows, lane_beta + h)
            gcol = column(gc_ref, rows, lane_g + h)
            tcol = column(tot_ref, rows, lane_g + h)
            grow = gct_ref[c, pl.ds(lane_g + h, 1), :]
            dec = jnp.exp(jnp.where(incl, gcol - grow, -jnp.inf))
            a_dir.append(jnp.where(strict, bcol * kk * dec, 0.0))
            egc = jnp.exp(gcol)
            rhs_dir.append(jnp.concatenate([bcol * vc, (bcol * egc) * kc], -1).astype(BF16))
            wq_ref[c, DN_BLK:2 * DN_BLK, :] = (qc * egc).astype(BF16)
            qk_ref[c] = (qk * dec).astype(BF16)
            kd_ref[c] = (kc * jnp.exp(tcol - gcol)).T.astype(BF16)
            gl_ref[c] = jnp.broadcast_to(jnp.exp(tcol[0:1, :]), (8, LANE))
        a = jnp.concatenate([jnp.concatenate([a_dir[0], zero_blk], 1),
                             jnp.concatenate([zero_blk, a_dir[1]], 1)], 0)
        return a, jnp.concatenate(rhs_dir, 0)

    def prepare(it, _):
        blocks = [it * DN_PREP_BLOCKS + i for i in range(DN_PREP_BLOCKS)]
        a_rhs = [block_inputs(c) for c in blocks]
        t_inv, m = [], []
        for a, _ in a_rhs:
            a_off = jnp.where(pair_off(1), a, 0.0)
            t_inv.append(eye2 - a_off)
            neighbour = jnp.where(ii2 < DN_BLK, pltpu.roll(a, 1, 0), pltpu.roll(a, 2 * DN_BLK - 1, 0))
            m.append(a - jnp.sum(a_off, -1, keepdims=True) * neighbour)
        s = 2
        while s < DN_BLK:
            for i in range(DN_PREP_BLOCKS):
                m_off = jnp.where(pair_off(s), m[i], 0.0).astype(BF16)
                if 2 * s < DN_BLK:
                    upd = jnp.dot(m_off, jnp.concatenate([t_inv[i], m[i]], 1).astype(BF16),
                                  preferred_element_type=F32)
                    t_inv[i] = t_inv[i] - upd[:, :2 * DN_BLK]
                    m[i] = m[i] - upd[:, 2 * DN_BLK:]
                else:
                    t_inv[i] = t_inv[i] - jnp.dot(m_off, t_inv[i].astype(BF16), preferred_element_type=F32)
            s *= 2
        for i, c in enumerate(blocks):
            uw = jnp.dot(t_inv[i].astype(BF16), a_rhs[i][1], preferred_element_type=F32)
            for d, (wq_ref, u_ref) in enumerate(((wq_f, u_f), (wq_b, u_b))):
                u_ref[c] = uw[d * DN_BLK:(d + 1) * DN_BLK, :DN_HEAD_DIM]
                wq_ref[c, 0:DN_BLK, :] = uw[d * DN_BLK:(d + 1) * DN_BLK, DN_HEAD_DIM:].astype(BF16)
        return 0

    lax.fori_loop(0, n_blk // DN_PREP_BLOCKS, prepare, 0)

    zero_bf = jnp.zeros((DN_BLK, DN_BLK), BF16)

    def block_diag(top, bottom):
        return jnp.concatenate([jnp.concatenate([top, zero_bf], 1), jnp.concatenate([zero_bf, bottom], 1)], 0)

    def step(k, carry):
        s_f, s_b = carry
        cb = jnp.where(k < n_ctx_blk, n_ctx_blk - 1 - k, n_blk + n_ctx_blk - 1 - k)
        hd = DN_HEAD_DIM
        r = jnp.dot(jnp.concatenate([wq_f[k], wq_b[cb]], 1), block_diag(s_f.astype(BF16), s_b.astype(BF16)),
                    preferred_element_type=F32)
        v_new = block_diag((u_f[k] - r[:DN_BLK, :hd]).astype(BF16), (u_b[cb] - r[:DN_BLK, hd:]).astype(BF16))
        lhs = jnp.concatenate([jnp.concatenate([qk_f[k], qk_b[cb]], 1),
                               jnp.concatenate([kd_f[k], kd_b[cb]], 1)], 0)
        r2 = jnp.dot(lhs, v_new, preferred_element_type=F32)
        o_f[k] = r[DN_BLK:, :hd] + r2[:DN_BLK, :hd]
        o_b[cb] = r[DN_BLK:, hd:] + r2[:DN_BLK, hd:]
        return (gl_f[k][0:1, :] * s_f + r2[DN_BLK:, :hd], gl_b[cb][0:1, :] * s_b + r2[DN_BLK:, hd:])

    zero = jnp.zeros((DN_HEAD_DIM, DN_HEAD_DIM), F32)
    lax.fori_loop(0, n_blk, step, (zero, zero))

    o = (o_f[...] + o_b[...]).reshape(lt, DN_HEAD_DIM)
    o = o * lax.rsqrt(jnp.mean(o * o, -1, keepdims=True) + 1e-6) * ng_ref[...]
    z = z_ref[0]
    o_ref[0] = (o * (z * jax.nn.sigmoid(z))).astype(o_ref.dtype)


def _deltanet_core(p, conv_w, a_log, dt_bias, norm_g):
    bsz, lt, _ = p.shape
    n_blk = lt // DN_BLK
    gate_pad = lambda t: jnp.pad(t.reshape(1, 2 * DN_HEADS), ((0, 0), (DN_GATE_A_F, LANE - DN_GATE_A_F - 2 * DN_HEADS)))
    kernel = functools.partial(_dn_kernel, n_blk=n_blk, n_ctx_blk=CTX_LEN // DN_BLK)
    col = lambda off: pl.BlockSpec((1, lt, LANE), lambda b, h: (b, 0, off + h))
    cw = lambda off: pl.BlockSpec((DN_CONV, LANE), lambda b, h: (0, off + h))
    const = pl.BlockSpec((1, LANE), lambda b, h: (0, 0))
    seq = lambda dt: pltpu.VMEM((lt, LANE), dt)
    blk = lambda rows, dt: pltpu.VMEM((n_blk, rows, LANE), dt)
    per_dir = [blk(2 * DN_BLK, BF16), blk(DN_BLK, F32), blk(DN_BLK, BF16), blk(DN_BLK, BF16), blk(8, F32),
               blk(DN_BLK, F32)]
    return pl.pallas_call(
        kernel,
        grid=(bsz, DN_HEADS),
        in_specs=[col(0), col(DN_HEADS), col(2 * DN_HEADS), col(3 * DN_HEADS),
                  pl.BlockSpec((1, lt, LANE), lambda b, h: (b, 0, 4 * DN_HEADS)),
                  cw(0), cw(DN_HEADS), cw(2 * DN_HEADS), const, const, const],
        out_specs=pl.BlockSpec((1, lt, LANE), lambda b, h: (b, 0, h)),
        out_shape=jax.ShapeDtypeStruct((bsz, lt, DN_WIDTH), BF16),
        scratch_shapes=[seq(F32), seq(F32), seq(F32), blk(LANE, F32), seq(F32), seq(F32), seq(F32),
                        pltpu.VMEM((lt + 3 * DN_HALO, LANE), F32)] + per_dir + per_dir,
        compiler_params=_params(2),
        name="deltanet_core",
    )(p, p, p, p, p, conv_w, conv_w, conv_w, gate_pad(a_log), gate_pad(dt_bias), norm_g.reshape(1, LANE))


S5_T = 8
S5_BB = 4
S5_BLK_GROUPS = LANE // SS_GROUP
S5_HALF = S5_BLK_GROUPS * SS_STATE
S5_K = S5_T * LANE


def _s5_weights(a_re, a_im, log_dt, b_re, b_im, c_re, c_im):
    hp = lax.Precision.HIGHEST
    t = S5_T
    nj = SS_GROUPS // S5_BLK_GROUPS
    ar = jnp.minimum(a_re, -1e-4)
    ai = a_im
    dt = jnp.exp(log_dt)[..., None]
    zr, zi = ar * dt, ai * dt
    d = jnp.arange(t + 1, dtype=F32)[:, None, None, None]
    mag = jnp.exp(d * zr)
    er, ei = mag * jnp.cos(d * zi), mag * jnp.sin(d * zi)
    nr, ni = er[1] - 1.0, ei[1]
    den = ar * ar + ai * ai
    fr, fi = (nr * ar + ni * ai) / den, (ni * ar - nr * ai) / den
    bt_re, bt_im = b_re.transpose(0, 1, 3, 2), b_im.transpose(0, 1, 3, 2)
    bbr = fr[:, :, None, :] * bt_re - fi[:, :, None, :] * bt_im
    bbi = fr[:, :, None, :] * bt_im + fi[:, :, None, :] * bt_re
    cer = c_re[None] * er[:, :, :, None, :] - c_im[None] * ei[:, :, :, None, :]
    cei = c_re[None] * ei[:, :, :, None, :] + c_im[None] * er[:, :, :, None, :]
    kk = jnp.einsum('tdgiq,dgjq->tdgij', jnp.concatenate([cer, -cei], -1), jnp.concatenate([bbr, bbi], -1),
                    precision=hp)
    s_idx = jnp.arange(t)[:, None]
    t_idx = jnp.arange(t)[None, :]
    lag_f = jnp.clip(t_idx - s_idx, 0, t)
    lag_b = jnp.clip(s_idx - t_idx, 0, t)
    m_f = (t_idx >= s_idx)[:, :, None, None, None]
    m_b = (s_idx >= t_idx)[:, :, None, None, None]
    kst = jnp.where(m_f, kk[lag_f, 0], 0.0) + jnp.where(m_b, kk[lag_b, 1], 0.0)
    kst = kst.reshape(t, t, nj, S5_BLK_GROUPS, SS_GROUP, SS_GROUP).transpose(2, 0, 3, 5, 1, 4)
    m_intra = kst.reshape(nj, S5_K, t * SS_GROUP)

    pow_f = (t - 1 - jnp.arange(t))
    pow_b = jnp.arange(t)

    def carry_w(pw, dd):
        e_r, e_i = er[pw, dd][:, :, None, :], ei[pw, dd][:, :, None, :]
        re = e_r * bbr[dd][None] - e_i * bbi[dd][None]
        im = e_r * bbi[dd][None] + e_i * bbr[dd][None]
        return jnp.stack([re, im], 0)

    wb = jnp.stack([carry_w(pow_f, 0), carry_w(pow_b, 1)], 0)
    wb = wb.reshape(2, 2, t, nj, S5_BLK_GROUPS, SS_GROUP, SS_STATE)
    m_carry = wb.transpose(3, 2, 4, 5, 0, 1, 6).reshape(nj, S5_K, 4 * SS_STATE)

    def read_w(pw, dd):
        wc = jnp.stack([cer[pw, dd], -cei[pw, dd]], 0)
        wc = wc.reshape(2, t, nj, S5_BLK_GROUPS, SS_GROUP, SS_STATE)
        return wc.transpose(2, 1, 4, 0, 3, 5).reshape(nj, t * SS_GROUP, 2 * S5_HALF)

    m_read_f = read_w(jnp.arange(t) + 1, 0)
    m_read_b = read_w(t - jnp.arange(t), 1)
    lam_t = jnp.stack([er[t, 0], ei[t, 0], er[t, 1], ei[t, 1]], 0)
    lam_t = lam_t.reshape(4, nj, S5_HALF).transpose(1, 0, 2)
    return m_carry.astype(BF16), m_intra.astype(BF16), m_read_f.astype(BF16), m_read_b.astype(BF16), lam_t


def _s5_expand(m, row_group_div, col_seg, n_col, transposed=False):
    n_in, rows = m.shape if transposed else m.shape[::-1]
    q = lax.broadcasted_iota(jnp.int32, (n_in, n_col), 0)
    c = lax.broadcasted_iota(jnp.int32, (n_in, n_col), 1)
    out_seg = col_seg * S5_BLK_GROUPS
    rep = ((q // col_seg == c // out_seg) & (q % col_seg == c % col_seg)).astype(BF16)
    wide = lax.dot_general(m, rep, (((0 if transposed else 1,), (0,)), ((), ())), preferred_element_type=F32)
    g_row = (lax.broadcasted_iota(jnp.int32, (rows, n_col), 0) // row_group_div) % S5_BLK_GROUPS
    g_col = (lax.broadcasted_iota(jnp.int32, (rows, n_col), 1) // col_seg) % S5_BLK_GROUPS
    return jnp.where(g_row == g_col, wide, 0.0).astype(BF16)


def _s5_kernel(x_ref, mod_ref, mc_ref, mi_ref, mrf_ref, mrb_ref, lt_ref, y_ref,
               wb_ref, wi_ref, wcf_ref, wcb_ref, xcat_ref, hf_ref, hb_ref, *, n_chunk, n_ctx_chunk):
    bh = pl.program_id(1)

    @pl.when(bh == 0)
    def _expand_weights():
        wb_ref[...] = _s5_expand(mc_ref[0], SS_GROUP, SS_STATE, 4 * S5_HALF)
        wi_ref[...] = _s5_expand(mi_ref[0], SS_GROUP, SS_GROUP, S5_K)
        wcf_ref[...] = _s5_expand(mrf_ref[0], SS_STATE, SS_GROUP, S5_K, transposed=True)
        wcb_ref[...] = _s5_expand(mrb_ref[0], SS_STATE, SS_GROUP, S5_K, transposed=True)

    nb = x_ref.shape[0]
    n_lb = hf_ref.shape[0]
    half_lb = n_lb // 2
    is_ctx = lax.broadcasted_iota(jnp.int32, (n_chunk, 1), 0) < n_ctx_chunk
    sh_c = mod_ref[8, 0:1, :]
    sc_c = mod_ref[8, 1:2, :]

    def carry_in(bl, _):
        mb = mod_ref[bh * nb + bl]
        sh = jnp.where(is_ctx, sh_c, mb[0:1, :])
        sc1 = 1.0 + jnp.where(is_ctx, sc_c, mb[1:2, :])
        rows = pl.ds(pl.multiple_of(bl * n_chunk, n_chunk), n_chunk)
        for tau in range(S5_T):
            xt = x_ref[bl, pl.ds(tau, n_chunk, stride=S5_T), :]
            xcat_ref[rows, tau * LANE:(tau + 1) * LANE] = (xt * sc1 + sh).astype(BF16)
        hbv = jnp.dot(xcat_ref[rows, :], wb_ref[...], preferred_element_type=F32)
        for l in range(n_lb):
            hf_ref[l, rows, :] = hbv[:, l * LANE:(l + 1) * LANE]
            hb_ref[l, rows, :] = hbv[:, (n_lb + l) * LANE:(n_lb + l + 1) * LANE]
        return 0

    lax.fori_loop(0, nb, carry_in, 0)

    lt = lt_ref[0]

    def lam_blocks(i):
        return [jnp.broadcast_to(lt[i:i + 1, l * LANE:(l + 1) * LANE], (nb, LANE)) for l in range(half_lb)]

    a_f = (lam_blocks(0), lam_blocks(1))
    a_b = (lam_blocks(2), lam_blocks(3))

    def advance(h_ref, c, a, state):
        a_re, a_im = a
        new = []
        for l in range(half_lb):
            sel = pl.ds(c, nb, stride=n_chunk)
            in_re = h_ref[l, sel, :]
            in_im = h_ref[half_lb + l, sel, :]
            s_re, s_im = state[l], state[half_lb + l]
            h_ref[l, sel, :] = s_re
            h_ref[half_lb + l, sel, :] = s_im
            new.append((a_re[l] * s_re - a_im[l] * s_im + in_re, a_re[l] * s_im + a_im[l] * s_re + in_im))
        return tuple(n[0] for n in new) + tuple(n[1] for n in new)

    def step(k, carry):
        st_f, st_b = carry
        cb = jnp.where(k < n_ctx_chunk, n_ctx_chunk - 1 - k, n_chunk + n_ctx_chunk - 1 - k)
        return advance(hf_ref, k, a_f, st_f), advance(hb_ref, cb, a_b, st_b)

    zero = tuple(jnp.zeros((nb, LANE), F32) for _ in range(n_lb))
    lax.fori_loop(0, n_chunk, step, (zero, zero))

    def read_out(bl, _):
        rows = pl.ds(pl.multiple_of(bl * n_chunk, n_chunk), n_chunk)
        h_f = jnp.concatenate([hf_ref[l, rows, :] for l in range(n_lb)], -1).astype(BF16)
        h_b = jnp.concatenate([hb_ref[l, rows, :] for l in range(n_lb)], -1).astype(BF16)
        y = (jnp.dot(xcat_ref[rows, :], wi_ref[...], preferred_element_type=F32)
             + jnp.dot(h_f, wcf_ref[...], preferred_element_type=F32)
             + jnp.dot(h_b, wcb_ref[...], preferred_element_type=F32))
        for t in range(S5_T):
            y_ref[bl, pl.ds(t, n_chunk, stride=S5_T), :] = y[:, t * LANE:(t + 1) * LANE]
        return 0

    lax.fori_loop(0, nb, read_out, 0)


def _s5_core(xs, mods, weights, layer):
    bsz, lt, _ = xs.shape
    m_carry, m_intra, m_read_f, m_read_b, lam_t = weights
    nj = D_MODEL // LANE
    nb = min(S5_BB, bsz)
    n_chunk = lt // S5_T
    kernel = functools.partial(_s5_kernel, n_chunk=n_chunk, n_ctx_chunk=CTX_LEN // S5_T)
    wspec = lambda shape: pl.BlockSpec((1,) + shape, lambda j, bh: (j, 0, 0))
    return pl.pallas_call(
        kernel,
        grid=(nj, bsz // nb),
        in_specs=[pl.BlockSpec((nb, lt, LANE), lambda j, bh: (bh, 0, j)),
                  pl.BlockSpec((MOD_ROWS, 6, LANE), lambda j, bh: (layer, 0, j)),
                  wspec(m_carry.shape[1:]), wspec(m_intra.shape[1:]),
                  wspec(m_read_f.shape[1:]), wspec(m_read_b.shape[1:]),
                  wspec((4, S5_HALF))],
        out_specs=pl.BlockSpec((nb, lt, LANE), lambda j, bh: (bh, 0, j)),
        out_shape=jax.ShapeDtypeStruct((bsz, lt, D_MODEL), F32),
        scratch_shapes=[pltpu.VMEM((S5_K, 4 * S5_HALF), BF16), pltpu.VMEM((S5_K, S5_K), BF16),
                        pltpu.VMEM((2 * S5_HALF, S5_K), BF16), pltpu.VMEM((2 * S5_HALF, S5_K), BF16),
                        pltpu.VMEM((nb * n_chunk, S5_K), BF16),
                        pltpu.VMEM((2 * S5_HALF // LANE, nb * n_chunk, LANE), F32),
                        pltpu.VMEM((2 * S5_HALF // LANE, nb * n_chunk, LANE), F32)],
        compiler_params=_params(2),
        name="s5_core",
    )(xs, mods, m_carry, m_intra, m_read_f, m_read_b, lam_t)


def kernel(x, c, ctx, c_ctx, ada_w, ada_b, ln_g, ln_b, mlp_w1, mlp_w2, dn_w_in, dn_conv, dn_a_log, dn_dt_bias, dn_norm_g, dn_w_out, da_w_qkv, da_lambda, da_norm_g, da_w_out, ga_w_qkv, ga_q_norm, ga_k_norm, ga_w_out, ss_a_re, ss_a_im, ss_log_dt, ss_b_re, ss_b_im, ss_c_re, ss_c_im, ss_d, ss_w_glu):
    bsz, n_latent, _ = x.shape
    xs = jnp.concatenate([ctx, x], 1)
    c_rows = jnp.concatenate([c, c_ctx[None, :], jnp.zeros((MOD_ROWS - bsz - 1, D_MODEL), F32)], 0)
    mods = _ada_all(c_rows, ada_w, ada_b).reshape(DEPTH * MOD_ROWS, 6, D_MODEL)
    ones = jnp.ones((LANE,), F32)
    zeros4 = jnp.zeros((4, DA_HEAD_DIM), F32)

    for i in range(DEPTH):
        m, j = i % 4, i // 4
        last = i == DEPTH - 1
        if m == 0:
            w_in = jnp.pad(dn_w_in[j], ((0, 0), (0, LANE - 4 * DN_HEADS))).astype(BF16)
            p = _proj(xs, mods, w_in, i)
            o = _deltanet_core(p, dn_conv[j], dn_a_log[j], dn_dt_bias[j], dn_norm_g[j])
            w_out = dn_w_out[j]
        elif m == 1:
            lambda_init = 0.8 - 0.6 * math.exp(-0.3 * i)
            p = _proj(xs, mods, da_w_qkv[j].astype(BF16), i)
            cos, sin = _rope_tables(n_latent, DA_HEAD_DIM)
            o = _attention(p, cos, sin, ones, ones, da_lambda[j], da_norm_g[j],
                           n_kv=DA_HEADS, group=1, q_col=0, k_col=DA_HEADS, v_col=2 * DA_HEADS,
                           n_maps=2, qk_norm=False, scale=DA_HEAD_DIM ** -0.5, lambda_init=lambda_init)
            w_out = da_w_out[j]
        elif m == 2:
            p = _proj(xs, mods, ga_w_qkv[j].astype(BF16), i)
            cos, sin = _rope_tables(n_latent, GA_HEAD_DIM)
            o = _attention(p, cos, sin, ga_q_norm[j], ga_k_norm[j], zeros4, ones,
                           n_kv=GA_KV_HEADS, group=GA_HEADS // GA_KV_HEADS, q_col=0, k_col=GA_HEADS,
                           v_col=GA_HEADS + GA_KV_HEADS, n_maps=1, qk_norm=True,
                           scale=GA_HEAD_DIM ** -0.5, lambda_init=0.0)
            w_out = ga_w_out[j]
        w1, w2 = mlp_w1[i].astype(BF16), mlp_w2[i].astype(BF16)
        if m == 3:
            weights = _s5_weights(ss_a_re[j], ss_a_im[j], ss_log_dt[j], ss_b_re[j], ss_b_im[j],
                                  ss_c_re[j], ss_c_im[j])
            y = _s5_core(xs, mods, weights, i)
            xs = _s5_out(y, xs, mods, ss_d[j], ss_w_glu[j].astype(BF16), ln_g[i, 0], ln_b[i, 0], i, last)
            xs = _mlp(xs, mods, w1, w2, ln_g[i, 1], ln_b[i, 1], i, last)
        else:
            xs = _out_mlp(o, xs, mods, w_out.astype(BF16), w1, w2, ln_g[i], ln_b[i], i, last)
    return xs
```

```python
import functools
import math

import jax
import jax.numpy as jnp
import numpy as np
from jax import lax
from jax.experimental import pallas as pl
from jax.experimental.pallas import tpu as pltpu

F32 = jnp.float32
BF16 = jnp.bfloat16

D_MODEL = 1024
D_FF = 4 * D_MODEL
DEPTH = 4
GRID_W = 64
CTX_LEN = 256
ROPE_THETA = 10000.0
DEEPNORM_ALPHA = (2 * DEPTH) ** 0.25
TM = 256
LANE = 128
MOD_ROWS = 16
VMEM_LIMIT = 56 * 1024 * 1024

DN_HEADS = 8
DN_HEAD_DIM = 128
DN_WIDTH = DN_HEADS * DN_HEAD_DIM
DN_CONV = 5
DN_CHUNK = 64
DA_HEADS = 8
DA_HEAD_DIM = 64
GA_HEADS = 8
GA_KV_HEADS = 2
GA_HEAD_DIM = 128
SS_GROUP = 16
SS_GROUPS = D_MODEL // SS_GROUP
SS_STATE = 64


def _params(n_axes):
    return pltpu.CompilerParams(dimension_semantics=("arbitrary",) * n_axes,
                                vmem_limit_bytes=VMEM_LIMIT)


def _layer_norm(v, g, b):
    mu = jnp.mean(v, -1, keepdims=True)
    d = v - mu
    var = jnp.mean(d * d, -1, keepdims=True)
    return d * lax.rsqrt(var + 1e-5) * g + b


def _mod_spec(layer, skip=0):
    return pl.BlockSpec((1, 6, D_MODEL),
                        lambda b, r: (layer * MOD_ROWS + jnp.where(r + skip == 0, 8, b), 0, 0))


def _row_spec(width, skip=0):
    return pl.BlockSpec((1, TM, width), lambda b, r: (b, r + skip, 0))


def _ada_kernel(c_ref, w_ref, b_ref, o_ref):
    c = c_ref[...]
    act = (c * jax.nn.sigmoid(c)).astype(BF16)
    o_ref[0] = jnp.dot(act, w_ref[0].astype(BF16), preferred_element_type=F32) + b_ref[0]


def _ada_all(c_rows, ada_w, ada_b):
    tn = 1536
    n = 6 * D_MODEL
    return pl.pallas_call(
        _ada_kernel,
        grid=(DEPTH, n // tn),
        in_specs=[pl.BlockSpec((MOD_ROWS, D_MODEL), lambda i, j: (0, 0)),
                  pl.BlockSpec((1, D_MODEL, tn), lambda i, j: (i, 0, j)),
                  pl.BlockSpec((1, 1, tn), lambda i, j: (i, 0, j))],
        out_specs=pl.BlockSpec((1, MOD_ROWS, tn), lambda i, j: (i, 0, j)),
        out_shape=jax.ShapeDtypeStruct((DEPTH, MOD_ROWS, n), F32),
        compiler_params=_params(2),
        name="ada_mod",
    )(c_rows, ada_w, ada_b.reshape(DEPTH, 1, n))


def _proj_kernel(x_ref, mod_ref, w_ref, o_ref):
    sh = mod_ref[0, 0:1, :]
    sc = mod_ref[0, 1:2, :]
    h = (x_ref[0] * (1.0 + sc) + sh).astype(BF16)
    o_ref[0] = jnp.dot(h, w_ref[...], preferred_element_type=F32).astype(o_ref.dtype)


def _proj(xs, mods, w, layer, out_dtype=F32):
    bsz, lt, _ = xs.shape
    n = w.shape[1]
    return pl.pallas_call(
        _proj_kernel,
        grid=(bsz, lt // TM),
        in_specs=[pl.BlockSpec((1, TM, D_MODEL), lambda b, r: (b, r, 0)),
                  _mod_spec(layer),
                  pl.BlockSpec((D_MODEL, n), lambda b, r: (0, 0))],
        out_specs=pl.BlockSpec((1, TM, n), lambda b, r: (b, r, 0)),
        out_shape=jax.ShapeDtypeStruct((bsz, lt, n), out_dtype),
        compiler_params=_params(2),
        name="mod_proj",
    )(xs, mods, w)


FF_CHUNK = 1024


def _mlp_body(x, mod_ref, w1_ref, w2_ref, g_ref, b_ref, y_ref):
    sh = mod_ref[0, 3:4, :]
    sc = mod_ref[0, 4:5, :]
    gate = mod_ref[0, 5:6, :]
    h = (x * (1.0 + sc) + sh).astype(BF16)
    acc = jnp.zeros((TM, D_MODEL), F32)
    for j in range(D_FF // FF_CHUNK):
        a = jnp.dot(h, w1_ref[:, j * FF_CHUNK:(j + 1) * FF_CHUNK], preferred_element_type=F32)
        a = jnp.square(jnp.maximum(a, 0.0)).astype(BF16)
        acc = acc + jnp.dot(a, w2_ref[j * FF_CHUNK:(j + 1) * FF_CHUNK, :], preferred_element_type=F32)
    y_ref[0] = _layer_norm(DEEPNORM_ALPHA * x + gate * acc, g_ref[...], b_ref[...])


def _mlp_kernel(x_ref, mod_ref, w1_ref, w2_ref, g_ref, b_ref, y_ref):
    _mlp_body(x_ref[0], mod_ref, w1_ref, w2_ref, g_ref, b_ref, y_ref)


def _out_mlp_kernel(o_ref, x_ref, mod_ref, wo_ref, g1_ref, b1_ref, w1_ref, w2_ref, g2_ref, b2_ref, y_ref):
    y = jnp.dot(o_ref[0], wo_ref[...], preferred_element_type=F32)
    x1 = _layer_norm(DEEPNORM_ALPHA * x_ref[0] + mod_ref[0, 2:3, :] * y, g1_ref[...], b1_ref[...])
    _mlp_body(x1, mod_ref, w1_ref, w2_ref, g2_ref, b2_ref, y_ref)


def _out_mlp(o, xs, mods, w_out, w1, w2, ln_g, ln_b, layer, latent_only=False):
    bsz, lt, k = o.shape
    skip = 1 if latent_only else 0
    const = lambda shape: pl.BlockSpec(shape, lambda b, r: (0, 0))
    vec = lambda t: t.reshape(1, D_MODEL)
    return pl.pallas_call(
        _out_mlp_kernel,
        grid=(bsz, lt // TM - skip),
        in_specs=[_row_spec(k, skip), _row_spec(D_MODEL, skip), _mod_spec(layer, skip),
                  const((k, D_MODEL)), const((1, D_MODEL)), const((1, D_MODEL)),
                  const((D_MODEL, D_FF)), const((D_FF, D_MODEL)), const((1, D_MODEL)), const((1, D_MODEL))],
        out_specs=_row_spec(D_MODEL),
        out_shape=jax.ShapeDtypeStruct((bsz, lt - skip * TM, D_MODEL), F32),
        compiler_params=_params(2),
        name="out_mlp_ln",
    )(o, xs, mods, w_out, vec(ln_g[0]), vec(ln_b[0]), w1, w2, vec(ln_g[1]), vec(ln_b[1]))


def _mlp(xs, mods, w1, w2, ln_g, ln_b, layer, latent_only=False):
    bsz, rows, _ = xs.shape
    skip = 1 if latent_only else 0
    return pl.pallas_call(
        _mlp_kernel,
        grid=(bsz, rows // TM),
        in_specs=[_row_spec(D_MODEL),
                  _mod_spec(layer, skip),
                  pl.BlockSpec((D_MODEL, D_FF), lambda b, r: (0, 0)),
                  pl.BlockSpec((D_FF, D_MODEL), lambda b, r: (0, 0)),
                  pl.BlockSpec((1, D_MODEL), lambda b, r: (0, 0)),
                  pl.BlockSpec((1, D_MODEL), lambda b, r: (0, 0))],
        out_specs=_row_spec(D_MODEL),
        out_shape=jax.ShapeDtypeStruct((bsz, rows, D_MODEL), F32),
        compiler_params=_params(2),
        name="mlp_ln",
    )(xs, mods, w1, w2, ln_g.reshape(1, D_MODEL), ln_b.reshape(1, D_MODEL))


def _rope_tables(n_latent, head_dim):
    rows = n_latent // GRID_W
    row = np.repeat(np.arange(rows), GRID_W).astype(np.float32)
    col = np.tile(np.arange(GRID_W), rows).astype(np.float32)
    n_freq = head_dim // 4
    inv_freq = (ROPE_THETA ** (-np.arange(n_freq, dtype=np.float32) / n_freq)).astype(np.float32)
    ang = np.concatenate([row[:, None] * inv_freq, col[:, None] * inv_freq], -1)
    cos = np.repeat(np.cos(ang.astype(np.float64)), 2, axis=-1)
    sin = np.repeat(np.sin(ang.astype(np.float64)), 2, axis=-1) * np.tile([-1.0, 1.0], head_dim // 2)
    reps = LANE // head_dim
    cos = np.concatenate([np.ones((CTX_LEN, LANE)), np.tile(cos, (1, reps))], 0)
    sin = np.concatenate([np.zeros((CTX_LEN, LANE)), np.tile(sin, (1, reps))], 0)
    return jnp.asarray(cos, F32), jnp.asarray(sin, F32)


ATTN_KEY_CHUNKS = 3


def _rope(x, cos, sin_signed):
    lane = lax.broadcasted_iota(jnp.int32, x.shape, 1)
    nxt = pltpu.roll(x, LANE - 1, 1)
    prv = pltpu.roll(x, 1, 1)
    swapped = jnp.where(lane % 2 == 0, nxt, prv)
    return x * cos + swapped * sin_signed


def _rms(x, g):
    return x * lax.rsqrt(jnp.mean(x * x, -1, keepdims=True) + 1e-6) * g


def _attn_kernel(q_ref, qn_ref, qnn_ref, k_ref, v_ref, cos_ref, sin_ref, qg_ref, kg_ref, lam_ref, ng_ref, o_ref,
                 kb_ref, vt_ref, s_even_ref, s_odd_ref, qs_ref, *, n_maps, qk_norm, scale, lambda_init):
    g = pl.program_id(2)
    qt = pl.program_id(3)
    lt = kb_ref.shape[0]
    n_qt = lt // TM

    @pl.when((g == 0) & (qt == 0))
    def _prep_kv():
        k = k_ref[0]
        if qk_norm:
            k = _rms(k, kg_ref[...])
        kb_ref[...] = _rope(k, cos_ref[...], sin_ref[...]).astype(BF16)
        vt_ref[...] = v_ref[0].T.astype(BF16)

    def queries(ref, tile):
        q = ref[0]
        if qk_norm:
            q = _rms(q, qg_ref[...])
        row0 = pl.multiple_of(tile * TM, TM)
        q = _rope(q, cos_ref[pl.ds(row0, TM), :], sin_ref[pl.ds(row0, TM), :]) * (scale * math.log2(math.e))
        return q.astype(BF16)

    def query_maps(qb):
        if n_maps == 1:
            return (qb,)
        lane = lax.broadcasted_iota(jnp.int32, qb.shape, 1)
        zero = jnp.zeros_like(qb)
        return (jnp.where(lane < LANE // 2, qb, zero), jnp.where(lane >= LANE // 2, qb, zero))

    def scores_t(qm, rows):
        return lax.dot_general(kb_ref[rows, :], qm, (((1,), (1,)), ((), ())), preferred_element_type=F32)

    def finish(acc, l):
        if n_maps == 1:
            return (acc[0] * (1.0 / l[0])).T
        lp = lam_ref[...]
        lam = (jnp.exp(jnp.sum(lp[0:1, :] * lp[1:2, :])) - jnp.exp(jnp.sum(lp[2:3, :] * lp[3:4, :]))
               + lambda_init)
        o = (acc[0] * (1.0 / l[0]) - acc[1] * (lam / l[1])).T
        return _rms(o, ng_ref[...]) * (1.0 - lambda_init)

    @pl.when(qt == 0)
    def _first():
        ctx_keys = slice(0, CTX_LEN)
        acc, l = [], []
        for qm in query_maps(queries(q_ref, 0)):
            s = scores_t(qm, ctx_keys)
            e = jnp.exp2(s - jnp.max(s, 0, keepdims=True))
            l.append(jnp.sum(e, 0, keepdims=True))
            acc.append(jnp.dot(vt_ref[:, ctx_keys], e.astype(BF16), preferred_element_type=F32))
        o_ref[0] = finish(acc, l).astype(o_ref.dtype)
        for i, qm in enumerate(query_maps(queries(qn_ref, 1))):
            s_odd_ref[i] = scores_t(qm, slice(0, lt))
        qs_ref[...] = queries(qnn_ref, jnp.minimum(2, n_qt - 1))

    def steady(cur_ref, nxt_ref):
        ch = lt // ATTN_KEY_CHUNKS
        m = [jnp.max(cur_ref[i], 0, keepdims=True) for i in range(n_maps)]
        qn_maps = query_maps(qs_ref[...])
        l = [jnp.zeros((1, TM), F32) for _ in range(n_maps)]
        acc = [jnp.zeros((LANE, TM), F32) for _ in range(n_maps)]
        for j in range(ATTN_KEY_CHUNKS):
            rows = slice(j * ch, (j + 1) * ch)
            for i in range(n_maps):
                nxt_ref[i, rows, :] = scores_t(qn_maps[i], rows)
            for i in range(n_maps):
                e = jnp.exp2(cur_ref[i, rows, :] - m[i])
                l[i] = l[i] + jnp.sum(e, 0, keepdims=True)
                acc[i] = acc[i] + jnp.dot(vt_ref[:, rows], e.astype(BF16), preferred_element_type=F32)
            if j == 0:
                qs_ref[...] = queries(qnn_ref, jnp.minimum(qt + 2, n_qt - 1))
        o_ref[0] = finish(acc, l).astype(o_ref.dtype)

    for parity, (cur_ref, nxt_ref) in enumerate(((s_even_ref, s_odd_ref), (s_odd_ref, s_even_ref))):
        @pl.when((qt > 0) & (qt % 2 == parity))
        def _steady(cur_ref=cur_ref, nxt_ref=nxt_ref):
            steady(cur_ref, nxt_ref)


def _attention(p, cos, sin, q_gain, k_gain, lam_p, norm_g, *, n_kv, group, q_col, k_col, v_col,
               n_maps, qk_norm, scale, lambda_init):
    bsz, lt, _ = p.shape
    n_qt = lt // TM
    kernel = functools.partial(_attn_kernel, n_maps=n_maps, qk_norm=qk_norm, scale=scale,
                               lambda_init=lambda_init)
    const = lambda b, kv, g, qt: (0, 0)
    return pl.pallas_call(
        kernel,
        grid=(bsz, n_kv, group, n_qt),
        in_specs=[pl.BlockSpec((1, TM, LANE), lambda b, kv, g, qt: (b, qt, q_col + kv * group + g)),
                  pl.BlockSpec((1, TM, LANE),
                               lambda b, kv, g, qt: (b, jnp.minimum(qt + 1, n_qt - 1), q_col + kv * group + g)),
                  pl.BlockSpec((1, TM, LANE),
                               lambda b, kv, g, qt: (b, jnp.minimum(qt + 2, n_qt - 1), q_col + kv * group + g)),
                  pl.BlockSpec((1, lt, LANE), lambda b, kv, g, qt: (b, 0, k_col + kv)),
                  pl.BlockSpec((1, lt, LANE), lambda b, kv, g, qt: (b, 0, v_col + kv)),
                  pl.BlockSpec((lt, LANE), const),
                  pl.BlockSpec((lt, LANE), const),
                  pl.BlockSpec((1, LANE), const),
                  pl.BlockSpec((1, LANE), const),
                  pl.BlockSpec(lam_p.shape, const),
                  pl.BlockSpec((1, LANE), const)],
        out_specs=pl.BlockSpec((1, TM, LANE), lambda b, kv, g, qt: (b, qt, kv * group + g)),
        out_shape=jax.ShapeDtypeStruct((bsz, lt, n_kv * group * LANE), BF16),
        scratch_shapes=[pltpu.VMEM((lt, LANE), BF16), pltpu.VMEM((LANE, lt), BF16),
                        pltpu.VMEM((n_maps, lt, TM), F32), pltpu.VMEM((n_maps, lt, TM), F32),
                        pltpu.VMEM((TM, LANE), BF16)],
        compiler_params=_params(4),
        name="attention",
    )(p, p, p, p, p, cos, sin, q_gain.reshape(1, LANE), k_gain.reshape(1, LANE), lam_p, norm_g.reshape(1, LANE))


def _s5_out_kernel(y_ref, x_ref, mod_ref, d_ref, w_ref, g_ref, b_ref, o_ref):
    sh = mod_ref[0, 0:1, :]
    sc = mod_ref[0, 1:2, :]
    gate = mod_ref[0, 2:3, :]
    x = x_ref[0]
    u = x * (1.0 + sc) + sh
    y = jax.nn.gelu(y_ref[0] + d_ref[...] * u).astype(BF16)
    z = jnp.dot(y, w_ref[...], preferred_element_type=F32)
    out = z[:, :D_MODEL] * jax.nn.sigmoid(z[:, D_MODEL:])
    o_ref[0] = _layer_norm(DEEPNORM_ALPHA * x + gate * out, g_ref[...], b_ref[...])


def _s5_out(y, xs, mods, d_skip, w_glu, ln_g, ln_b, layer, latent_only=False):
    bsz, lt, _ = xs.shape
    skip = 1 if latent_only else 0
    return pl.pallas_call(
        _s5_out_kernel,
        grid=(bsz, lt // TM - skip),
        in_specs=[_row_spec(D_MODEL, skip),
                  _row_spec(D_MODEL, skip),
                  _mod_spec(layer, skip),
                  pl.BlockSpec((1, D_MODEL), lambda b, r: (0, 0)),
                  pl.BlockSpec((D_MODEL, 2 * D_MODEL), lambda b, r: (0, 0)),
                  pl.BlockSpec((1, D_MODEL), lambda b, r: (0, 0)),
                  pl.BlockSpec((1, D_MODEL), lambda b, r: (0, 0))],
        out_specs=_row_spec(D_MODEL),
        out_shape=jax.ShapeDtypeStruct((bsz, lt - skip * TM, D_MODEL), F32),
        compiler_params=_params(2),
        name="s5_out_ln",
    )(y, xs, mods, d_skip.reshape(1, D_MODEL), w_glu, ln_g.reshape(1, D_MODEL), ln_b.reshape(1, D_MODEL))


DN_BLK = 128
DN_PREP_BLOCKS = 3
DN_HALO = 8
DN_GATE_BETA_F, DN_GATE_BETA_B, DN_GATE_A_F, DN_GATE_A_B = 0, DN_HEADS, 2 * DN_HEADS, 3 * DN_HEADS
_NT = (((1,), (1,)), ((), ()))
_TN = (((0,), (0,)), ((), ()))


def _dn_kernel(q_ref, k_ref, v_ref, z_ref, g_ref, cq_ref, ck_ref, cv_ref, alog_ref, dtb_ref, ng_ref, o_ref,
               beta_ref, gc_ref, tot_ref, gct_ref, qn_ref, kn_ref, vn_ref, pad_ref,
               wq_f, u_f, qk_f, kd_f, gl_f, o_f, wq_b, u_b, qk_b, kd_b, gl_b, o_b, *, n_blk, n_ctx_blk):
    h = pl.program_id(1)
    lt = n_blk * DN_BLK
    ctx_rows = n_ctx_blk * DN_BLK
    row = lax.broadcasted_iota(jnp.int32, (lt, 1), 0)
    lane = lax.broadcasted_iota(jnp.int32, (1, LANE), 1)

    @pl.when(h == 0)
    def _gates():
        gts = g_ref[0]
        beta_ref[...] = jax.nn.sigmoid(gts)
        g = -jnp.exp(alog_ref[...]) * jax.nn.softplus(gts + dtb_ref[...])
        pos = row % DN_BLK
        pre = g
        suf = g
        s = 1
        while s < DN_BLK:
            pre = pre + jnp.where(pos >= s, pltpu.roll(pre, s, 0), 0.0)
            suf = suf + jnp.where(pos < DN_BLK - s, pltpu.roll(suf, lt - s, 0), 0.0)
            s *= 2
        gc = jnp.where(lane >= DN_GATE_A_B, suf, pre)
        gc_ref[...] = gc
        tot_ref[...] = pre + suf - g
        for c in range(n_blk):
            gct_ref[c] = gc[c * DN_BLK:(c + 1) * DN_BLK, :].T

    def conv_silu(x_ref, w_ref):
        pad_ref[0:DN_HALO, :] = jnp.zeros((DN_HALO, LANE), F32)
        pad_ref[DN_HALO + ctx_rows:2 * DN_HALO + ctx_rows, :] = jnp.zeros((DN_HALO, LANE), F32)
        pad_ref[2 * DN_HALO + lt:3 * DN_HALO + lt, :] = jnp.zeros((DN_HALO, LANE), F32)
        pad_ref[DN_HALO:DN_HALO + ctx_rows, :] = x_ref[0, 0:ctx_rows, :]
        pad_ref[2 * DN_HALO + ctx_rows:2 * DN_HALO + lt, :] = x_ref[0, ctx_rows:lt, :]
        w = w_ref[...]
        parts = []
        for base, n in ((DN_HALO, ctx_rows), (2 * DN_HALO + ctx_rows, lt - ctx_rows)):
            acc = None
            for k in range(DN_CONV):
                tap = pad_ref[base + k - DN_CONV // 2:base + k - DN_CONV // 2 + n, :] * w[k:k + 1, :]
                acc = tap if acc is None else acc + tap
            parts.append(acc * jax.nn.sigmoid(acc))
        return jnp.concatenate(parts, 0)

    def l2n(x):
        return x * lax.rsqrt(jnp.sum(x * x, -1, keepdims=True) + 1e-6)

    qn_ref[...] = l2n(conv_silu(q_ref, cq_ref)) * DN_HEAD_DIM ** -0.5
    kn_ref[...] = l2n(conv_silu(k_ref, ck_ref))
    vn_ref[...] = conv_silu(v_ref, cv_ref)

    ii = lax.broadcasted_iota(jnp.int32, (DN_BLK, DN_BLK), 0)
    jj = lax.broadcasted_iota(jnp.int32, (DN_BLK, DN_BLK), 1)
    fwd =(DN_GATE_BETA_F, DN_GATE_A_F, ii >= jj, ii > jj, (wq_f, u_f, qk_f, kd_f, gl_f, o_f))
    bwd = (DN_GATE_BETA_B, DN_GATE_A_B, ii <= jj, ii < jj, (wq_b, u_b, qk_b, kd_b, gl_b, o_b))

    ii2 = lax.broadcasted_iota(jnp.int32, (2 * DN_BLK, 2 * DN_BLK), 0)
    jj2 = lax.broadcasted_iota(jnp.int32, (2 * DN_BLK, 2 * DN_BLK), 1)
    eye2 = (ii2 == jj2).astype(F32)
    zero_blk = jnp.zeros((DN_BLK, DN_BLK), F32)

    def pair_off(s):
        return ((ii2 // (2 * s)) == (jj2 // (2 * s))) & ((ii2 // s) != (jj2 // s))

    def column(ref, rows, lane_idx):
        return jnp.sum(jnp.where(lane == lane_idx, ref[rows, :], 0.0), -1, keepdims=True)

    def block_inputs(c):
        rows = pl.ds(pl.multiple_of(c * DN_BLK, DN_BLK), DN_BLK)
        qc, kc, vc = qn_ref[rows, :], kn_ref[rows, :], vn_ref[rows, :]
        kb = kc.astype(BF16)
        kk = lax.dot_general(kb, kb, _NT, preferred_element_type=F32)
        qk = lax.dot_general(qc.astype(BF16), kb, _NT, preferred_element_type=F32)
        a_dir, rhs_dir = [], []
        for lane_beta, lane_g, incl, strict, (wq_ref, _, qk_ref, kd_ref, gl_ref, _) in (fwd, bwd):
            bcol = column(beta_ref, rows, lane_beta + h)
            gcol = column(gc_ref, rows, lane_g + h)
            tcol = column(tot_ref, rows, lane_g + h)
            grow = gct_ref[c, pl.ds(lane_g + h, 1), :]
            dec = jnp.exp(jnp.where(incl, gcol - grow, -jnp.inf))
            a_dir.append(jnp.where(strict, bcol * kk * dec, 0.0))
            egc = jnp.exp(gcol)
            rhs_dir.append(jnp.concatenate([bcol * vc, (bcol * egc) * kc], -1).astype(BF16))
            wq_ref[c, DN_BLK:2 * DN_BLK, :] = (qc * egc).astype(BF16)
            qk_ref[c] = (qk * dec).astype(BF16)
            kd_ref[c] = (kc * jnp.exp(tcol - gcol)).T.astype(BF16)
            gl_ref[c] = jnp.broadcast_to(jnp.exp(tcol[0:1, :]), (8, LANE))
        a = jnp.concatenate([jnp.concatenate([a_dir[0], zero_blk], 1),
                             jnp.concatenate([zero_blk, a_dir[1]], 1)], 0)
        return a, jnp.concatenate(rhs_dir, 0)

    def prepare(it, _):
        blocks = [it * DN_PREP_BLOCKS + i for i in range(DN_PREP_BLOCKS)]
        a_rhs = [block_inputs(c) for c in blocks]
        t_inv, m = [], []
        for a, _ in a_rhs:
            a_off = jnp.where(pair_off(1), a, 0.0)
            t_inv.append(eye2 - a_off)
            neighbour = jnp.where(ii2 < DN_BLK, pltpu.roll(a, 1, 0), pltpu.roll(a, 2 * DN_BLK - 1, 0))
            m.append(a - jnp.sum(a_off, -1, keepdims=True) * neighbour)
        s = 2
        while s < DN_BLK:
            for i in range(DN_PREP_BLOCKS):
                m_off = jnp.where(pair_off(s), m[i], 0.0).astype(BF16)
                if 2 * s < DN_BLK:
                    upd = jnp.dot(m_off, jnp.concatenate([t_inv[i], m[i]], 1).astype(BF16),
                                  preferred_element_type=F32)
                    t_inv[i] = t_inv[i] - upd[:, :2 * DN_BLK]
                    m[i] = m[i] - upd[:, 2 * DN_BLK:]
                else:
                    t_inv[i] = t_inv[i] - jnp.dot(m_off, t_inv[i].astype(BF16), preferred_element_type=F32)
            s *= 2
        for i, c in enumerate(blocks):
            uw = jnp.dot(t_inv[i].astype(BF16), a_rhs[i][1], preferred_element_type=F32)
            for d, (wq_ref, u_ref) in enumerate(((wq_f, u_f), (wq_b, u_b))):
                u_ref[c] = uw[d * DN_BLK:(d + 1) * DN_BLK, :DN_HEAD_DIM]
                wq_ref[c, 0:DN_BLK, :] = uw[d * DN_BLK:(d + 1) * DN_BLK, DN_HEAD_DIM:].astype(BF16)
        return 0

    lax.fori_loop(0, n_blk // DN_PREP_BLOCKS, prepare, 0)

    zero_bf = jnp.zeros((DN_BLK, DN_BLK), BF16)

    def block_diag(top, bottom):
        return jnp.concatenate([jnp.concatenate([top, zero_bf], 1), jnp.concatenate([zero_bf, bottom], 1)], 0)

    def step(k, carry):
        s_f, s_b = carry
        cb = jnp.where(k < n_ctx_blk, n_ctx_blk - 1 - k, n_blk + n_ctx_blk - 1 - k)
        hd = DN_HEAD_DIM
        r = jnp.dot(jnp.concatenate([wq_f[k], wq_b[cb]], 1), block_diag(s_f.astype(BF16), s_b.astype(BF16)),
                    preferred_element_type=F32)
        v_new = block_diag((u_f[k] - r[:DN_BLK, :hd]).astype(BF16), (u_b[cb] - r[:DN_BLK, hd:]).astype(BF16))
        lhs = jnp.concatenate([jnp.concatenate([qk_f[k], qk_b[cb]], 1),
                               jnp.concatenate([kd_f[k], kd_b[cb]], 1)], 0)
        r2 = jnp.dot(lhs, v_new, preferred_element_type=F32)
        o_f[k] = r[DN_BLK:, :hd] + r2[:DN_BLK, :hd]
        o_b[cb] = r[DN_BLK:, hd:] + r2[:DN_BLK, hd:]
        return (gl_f[k][0:1, :] * s_f + r2[DN_BLK:, :hd], gl_b[cb][0:1, :] * s_b + r2[DN_BLK:, hd:])

    zero = jnp.zeros((DN_HEAD_DIM, DN_HEAD_DIM), F32)
    lax.fori_loop(0, n_blk, step, (zero, zero))

    o = (o_f[...] + o_b[...]).reshape(lt, DN_HEAD_DIM)
    o = o * lax.rsqrt(jnp.mean(o * o, -1, keepdims=True) + 1e-6) * ng_ref[...]
    z = z_ref[0]
    o_ref[0] = (o * (z * jax.nn.sigmoid(z))).astype(o_ref.dtype)


def _deltanet_core(p, conv_w, a_log, dt_bias, norm_g):
    bsz, lt, _ = p.shape
    n_blk = lt // DN_BLK
    gate_pad = lambda t: jnp.pad(t.reshape(1, 2 * DN_HEADS), ((0, 0), (DN_GATE_A_F, LANE - DN_GATE_A_F - 2 * DN_HEADS)))
    kernel = functools.partial(_dn_kernel, n_blk=n_blk, n_ctx_blk=CTX_LEN // DN_BLK)
    col = lambda off: pl.BlockSpec((1, lt, LANE), lambda b, h: (b, 0, off + h))
    cw = lambda off: pl.BlockSpec((DN_CONV, LANE), lambda b, h: (0, off + h))
    const = pl.BlockSpec((1, LANE), lambda b, h: (0, 0))
    seq = lambda dt: pltpu.VMEM((lt, LANE), dt)
    blk = lambda rows, dt: pltpu.VMEM((n_blk, rows, LANE), dt)
    per_dir = [blk(2 * DN_BLK, BF16), blk(DN_BLK, F32), blk(DN_BLK, BF16), blk(DN_BLK, BF16), blk(8, F32),
               blk(DN_BLK, F32)]
    return pl.pallas_call(
        kernel,
        grid=(bsz, DN_HEADS),
        in_specs=[col(0), col(DN_HEADS), col(2 * DN_HEADS), col(3 * DN_HEADS),
                  pl.BlockSpec((1, lt, LANE), lambda b, h: (b, 0, 4 * DN_HEADS)),
                  cw(0), cw(DN_HEADS), cw(2 * DN_HEADS), const, const, const],
        out_specs=pl.BlockSpec((1, lt, LANE), lambda b, h: (b, 0, h)),
        out_shape=jax.ShapeDtypeStruct((bsz, lt, DN_WIDTH), BF16),
        scratch_shapes=[seq(F32), seq(F32), seq(F32), blk(LANE, F32), seq(F32), seq(F32), seq(F32),
                        pltpu.VMEM((lt + 3 * DN_HALO, LANE), F32)] + per_dir + per_dir,
        compiler_params=_params(2),
        name="deltanet_core",
    )(p, p, p, p, p, conv_w, conv_w, conv_w, gate_pad(a_log), gate_pad(dt_bias), norm_g.reshape(1, LANE))


S5_T = 8
S5_BB = 4
S5_BLK_GROUPS = LANE // SS_GROUP
S5_HALF = S5_BLK_GROUPS * SS_STATE
S5_K = S5_T * LANE


def _s5_weights(a_re, a_im, log_dt, b_re, b_im, c_re, c_im):
    hp = lax.Precision.HIGHEST
    t = S5_T
    nj = SS_GROUPS // S5_BLK_GROUPS
    ar = jnp.minimum(a_re, -1e-4)
    ai = a_im
    dt = jnp.exp(log_dt)[..., None]
    zr, zi = ar * dt, ai * dt
    d = jnp.arange(t + 1, dtype=F32)[:, None, None, None]
    mag = jnp.exp(d * zr)
    er, ei = mag * jnp.cos(d * zi), mag * jnp.sin(d * zi)
    nr, ni = er[1] - 1.0, ei[1]
    den = ar * ar + ai * ai
    fr, fi = (nr * ar + ni * ai) / den, (ni * ar - nr * ai) / den
    bt_re, bt_im = b_re.transpose(0, 1, 3, 2), b_im.transpose(0, 1, 3, 2)
    bbr = fr[:, :, None, :] * bt_re - fi[:, :, None, :] * bt_im
    bbi = fr[:, :, None, :] * bt_im + fi[:, :, None, :] * bt_re
    cer = c_re[None] * er[:, :, :, None, :] - c_im[None] * ei[:, :, :, None, :]
    cei = c_re[None] * ei[:, :, :, None, :] + c_im[None] * er[:, :, :, None, :]
    kk = jnp.einsum('tdgiq,dgjq->tdgij', jnp.concatenate([cer, -cei], -1), jnp.concatenate([bbr, bbi], -1),
                    precision=hp)
    s_idx = jnp.arange(t)[:, None]
    t_idx = jnp.arange(t)[None, :]
    lag_f = jnp.clip(t_idx - s_idx, 0, t)
    lag_b = jnp.clip(s_idx - t_idx, 0, t)
    m_f = (t_idx >= s_idx)[:, :, None, None, None]
    m_b = (s_idx >= t_idx)[:, :, None, None, None]
    kst = jnp.where(m_f, kk[lag_f, 0], 0.0) + jnp.where(m_b, kk[lag_b, 1], 0.0)
    kst = kst.reshape(t, t, nj, S5_BLK_GROUPS, SS_GROUP, SS_GROUP).transpose(2, 0, 3, 5, 1, 4)
    m_intra = kst.reshape(nj, S5_K, t * SS_GROUP)

    pow_f = (t - 1 - jnp.arange(t))
    pow_b = jnp.arange(t)

    def carry_w(pw, dd):
        e_r, e_i = er[pw, dd][:, :, None, :], ei[pw, dd][:, :, None, :]
        re = e_r * bbr[dd][None] - e_i * bbi[dd][None]
        im = e_r * bbi[dd][None] + e_i * bbr[dd][None]
        return jnp.stack([re, im], 0)

    wb = jnp.stack([carry_w(pow_f, 0), carry_w(pow_b, 1)], 0)
    wb = wb.reshape(2, 2, t, nj, S5_BLK_GROUPS, SS_GROUP, SS_STATE)
    m_carry = wb.transpose(3, 2, 4, 5, 0, 1, 6).reshape(nj, S5_K, 4 * SS_STATE)

    def read_w(pw, dd):
        wc = jnp.stack([cer[pw, dd], -cei[pw, dd]], 0)
        wc = wc.reshape(2, t, nj, S5_BLK_GROUPS, SS_GROUP, SS_STATE)
        return wc.transpose(2, 1, 4, 0, 3, 5).reshape(nj, t * SS_GROUP, 2 * S5_HALF)

    m_read_f = read_w(jnp.arange(t) + 1, 0)
    m_read_b = read_w(t - jnp.arange(t), 1)
    lam_t = jnp.stack([er[t, 0], ei[t, 0], er[t, 1], ei[t, 1]], 0)
    lam_t = lam_t.reshape(4, nj, S5_HALF).transpose(1, 0, 2)
    return m_carry.astype(BF16), m_intra.astype(BF16), m_read_f.astype(BF16), m_read_b.astype(BF16), lam_t


def _s5_expand(m, row_group_div, col_seg, n_col, transposed=False):
    n_in, rows = m.shape if transposed else m.shape[::-1]
    q = lax.broadcasted_iota(jnp.int32, (n_in, n_col), 0)
    c = lax.broadcasted_iota(jnp.int32, (n_in, n_col), 1)
    out_seg = col_seg * S5_BLK_GROUPS
    rep = ((q // col_seg == c // out_seg) & (q % col_seg == c % col_seg)).astype(BF16)
    wide = lax.dot_general(m, rep, (((0 if transposed else 1,), (0,)), ((), ())), preferred_element_type=F32)
    g_row = (lax.broadcasted_iota(jnp.int32, (rows, n_col), 0) // row_group_div) % S5_BLK_GROUPS
    g_col = (lax.broadcasted_iota(jnp.int32, (rows, n_col), 1) // col_seg) % S5_BLK_GROUPS
    return jnp.where(g_row == g_col, wide, 0.0).astype(BF16)


def _s5_kernel(x_ref, mod_ref, mc_ref, mi_ref, mrf_ref, mrb_ref, lt_ref, y_ref,
               wb_ref, wi_ref, wcf_ref, wcb_ref, xcat_ref, hf_ref, hb_ref, *, n_chunk, n_ctx_chunk):
    bh = pl.program_id(1)

    @pl.when(bh == 0)
    def _expand_weights():
        wb_ref[...] = _s5_expand(mc_ref[0], SS_GROUP, SS_STATE, 4 * S5_HALF)
        wi_ref[...] = _s5_expand(mi_ref[0], SS_GROUP, SS_GROUP, S5_K)
        wcf_ref[...] = _s5_expand(mrf_ref[0], SS_STATE, SS_GROUP, S5_K, transposed=True)
        wcb_ref[...] = _s5_expand(mrb_ref[0], SS_STATE, SS_GROUP, S5_K, transposed=True)

    nb = x_ref.shape[0]
    n_lb = hf_ref.shape[0]
    half_lb = n_lb // 2
    is_ctx = lax.broadcasted_iota(jnp.int32, (n_chunk, 1), 0) < n_ctx_chunk
    sh_c = mod_ref[8, 0:1, :]
    sc_c = mod_ref[8, 1:2, :]

    def carry_in(bl, _):
        mb = mod_ref[bh * nb + bl]
        sh = jnp.where(is_ctx, sh_c, mb[0:1, :])
        sc1 = 1.0 + jnp.where(is_ctx, sc_c, mb[1:2, :])
        rows = pl.ds(pl.multiple_of(bl * n_chunk, n_chunk), n_chunk)
        for tau in range(S5_T):
            xt = x_ref[bl, pl.ds(tau, n_chunk, stride=S5_T), :]
            xcat_ref[rows, tau * LANE:(tau + 1) * LANE] = (xt * sc1 + sh).astype(BF16)
        hbv = jnp.dot(xcat_ref[rows, :], wb_ref[...], preferred_element_type=F32)
        for l in range(n_lb):
            hf_ref[l, rows, :] = hbv[:, l * LANE:(l + 1) * LANE]
            hb_ref[l, rows, :] = hbv[:, (n_lb + l) * LANE:(n_lb + l + 1) * LANE]
        return 0

    lax.fori_loop(0, nb, carry_in, 0)

    lt = lt_ref[0]

    def lam_blocks(i):
        return [jnp.broadcast_to(lt[i:i + 1, l * LANE:(l + 1) * LANE], (nb, LANE)) for l in range(half_lb)]

    a_f = (lam_blocks(0), lam_blocks(1))
    a_b = (lam_blocks(2), lam_blocks(3))

    def advance(h_ref, c, a, state):
        a_re, a_im = a
        new = []
        for l in range(half_lb):
            sel = pl.ds(c, nb, stride=n_chunk)
            in_re = h_ref[l, sel, :]
            in_im = h_ref[half_lb + l, sel, :]
            s_re, s_im = state[l], state[half_lb + l]
            h_ref[l, sel, :] = s_re
            h_ref[half_lb + l, sel, :] = s_im
            new.append((a_re[l] * s_re - a_im[l] * s_im + in_re, a_re[l] * s_im + a_im[l] * s_re + in_im))
        return tuple(n[0] for n in new) + tuple(n[1] for n in new)

    def step(k, carry):
        st_f, st_b = carry
        cb = jnp.where(k < n_ctx_chunk, n_ctx_chunk - 1 - k, n_chunk + n_ctx_chunk - 1 - k)
        return advance(hf_ref, k, a_f, st_f), advance(hb_ref, cb, a_b, st_b)

    zero = tuple(jnp.zeros((nb, LANE), F32) for _ in range(n_lb))
    lax.fori_loop(0, n_chunk, step, (zero, zero))

    def read_out(bl, _):
        rows = pl.ds(pl.multiple_of(bl * n_chunk, n_chunk), n_chunk)
        h_f = jnp.concatenate([hf_ref[l, rows, :] for l in range(n_lb)], -1).astype(BF16)
        h_b = jnp.concatenate([hb_ref[l, rows, :] for l in range(n_lb)], -1).astype(BF16)
        y = (jnp.dot(xcat_ref[rows, :], wi_ref[...], preferred_element_type=F32)
             + jnp.dot(h_f, wcf_ref[...], preferred_element_type=F32)
             + jnp.dot(h_b, wcb_ref[...], preferred_element_type=F32))
        for t in range(S5_T):
            y_ref[bl, pl.ds(t, n_chunk, stride=S5_T), :] = y[:, t * LANE:(t + 1) * LANE]
        return 0

    lax.fori_loop(0, nb, read_out, 0)


def _s5_core(xs, mods, weights, layer):
    bsz, lt, _ = xs.shape
    m_carry, m_intra, m_read_f, m_read_b, lam_t = weights
    nj = D_MODEL // LANE
    nb = min(S5_BB, bsz)
    n_chunk = lt // S5_T
    kernel = functools.partial(_s5_kernel, n_chunk=n_chunk, n_ctx_chunk=CTX_LEN // S5_T)
    wspec = lambda shape: pl.BlockSpec((1,) + shape, lambda j, bh: (j, 0, 0))
    return pl.pallas_call(
        kernel,
        grid=(nj, bsz // nb),
        in_specs=[pl.BlockSpec((nb, lt, LANE), lambda j, bh: (bh, 0, j)),
                  pl.BlockSpec((MOD_ROWS, 6, LANE), lambda j, bh: (layer, 0, j)),
                  wspec(m_carry.shape[1:]), wspec(m_intra.shape[1:]),
                  wspec(m_read_f.shape[1:]), wspec(m_read_b.shape[1:]),
                  wspec((4, S5_HALF))],
        out_specs=pl.BlockSpec((nb, lt, LANE), lambda j, bh: (bh, 0, j)),
        out_shape=jax.ShapeDtypeStruct((bsz, lt, D_MODEL), F32),
        scratch_shapes=[pltpu.VMEM((S5_K, 4 * S5_HALF), BF16), pltpu.VMEM((S5_K, S5_K), BF16),
                        pltpu.VMEM((2 * S5_HALF, S5_K), BF16), pltpu.VMEM((2 * S5_HALF, S5_K), BF16),
                        pltpu.VMEM((nb * n_chunk, S5_K), BF16),
                        pltpu.VMEM((2 * S5_HALF // LANE, nb * n_chunk, LANE), F32),
                        pltpu.VMEM((2 * S5_HALF // LANE, nb * n_chunk, LANE), F32)],
        compiler_params=_params(2),
        name="s5_core",
    )(xs, mods, m_carry, m_intra, m_read_f, m_read_b, lam_t)


def kernel(x, c, ctx, c_ctx, ada_w, ada_b, ln_g, ln_b, mlp_w1, mlp_w2, dn_w_in, dn_conv, dn_a_log, dn_dt_bias, dn_norm_g, dn_w_out, da_w_qkv, da_lambda, da_norm_g, da_w_out, ga_w_qkv, ga_q_norm, ga_k_norm, ga_w_out, ss_a_re, ss_a_im, ss_log_dt, ss_b_re, ss_b_im, ss_c_re, ss_c_im, ss_d, ss_w_glu):
    bsz, n_latent, _ = x.shape
    xs = jnp.concatenate([ctx, x], 1)
    c_rows = jnp.concatenate([c, c_ctx[None, :], jnp.zeros((MOD_ROWS - bsz - 1, D_MODEL), F32)], 0)
    mods = _ada_all(c_rows, ada_w, ada_b).reshape(DEPTH * MOD_ROWS, 6, D_MODEL)
    ones = jnp.ones((LANE,), F32)
    zeros4 = jnp.zeros((4, DA_HEAD_DIM), F32)

    for i in range(DEPTH):
        m, j = i % 4, i // 4
        last = i == DEPTH - 1
        if m == 0:
            w_in = jnp.pad(dn_w_in[j], ((0, 0), (0, LANE - 4 * DN_HEADS))).astype(BF16)
            p = _proj(xs, mods, w_in, i)
            o = _deltanet_core(p, dn_conv[j], dn_a_log[j], dn_dt_bias[j], dn_norm_g[j])
            w_out = dn_w_out[j]
        elif m == 1:
            lambda_init = 0.8 - 0.6 * math.exp(-0.3 * i)
            p = _proj(xs, mods, da_w_qkv[j].astype(BF16), i)
            cos, sin = _rope_tables(n_latent, DA_HEAD_DIM)
            o = _attention(p, cos, sin, ones, ones, da_lambda[j], da_norm_g[j],
                           n_kv=DA_HEADS, group=1, q_col=0, k_col=DA_HEADS, v_col=2 * DA_HEADS,
                           n_maps=2, qk_norm=False, scale=DA_HEAD_DIM ** -0.5, lambda_init=lambda_init)
            w_out = da_w_out[j]
        elif m == 2:
            p = _proj(xs, mods, ga_w_qkv[j].astype(BF16), i)
            cos, sin = _rope_tables(n_latent, GA_HEAD_DIM)
            o = _attention(p, cos, sin, ga_q_norm[j], ga_k_norm[j], zeros4, ones,
                           n_kv=GA_KV_HEADS, group=GA_HEADS // GA_KV_HEADS, q_col=0, k_col=GA_HEADS,
                           v_col=GA_HEADS + GA_KV_HEADS, n_maps=1, qk_norm=True,
                           scale=GA_HEAD_DIM ** -0.5, lambda_init=0.0)
            w_out = ga_w_out[j]
        w1, w2 = mlp_w1[i].astype(BF16), mlp_w2[i].astype(BF16)
        if m == 3:
            weights = _s5_weights(ss_a_re[j], ss_a_im[j], ss_log_dt[j], ss_b_re[j], ss_b_im[j],
                                  ss_c_re[j], ss_c_im[j])
            y = _s5_core(xs, mods, weights, i)
            xs = _s5_out(y, xs, mods, ss_d[j], ss_w_glu[j].astype(BF16), ln_g[i, 0], ln_b[i, 0], i, last)
            xs = _mlp(xs, mods, w1, w2, ln_g[i, 1], ln_b[i, 1], i, last)
        else:
            xs = _out_mlp(o, xs, mods, w_out.astype(BF16), w1, w2, ln_g[i], ln_b[i], i, last)
    return xs
```

```python
import functools
import math

import jax
import jax.numpy as jnp
import numpy as np
from jax import lax
from jax.experimental import pallas as pl
from jax.experimental.pallas import tpu as pltpu

F32 = jnp.float32
BF16 = jnp.bfloat16

D_MODEL = 1024
D_FF = 4 * D_MODEL
DEPTH = 4
GRID_W = 64
CTX_LEN = 256
ROPE_THETA = 10000.0
DEEPNORM_ALPHA = (2 * DEPTH) ** 0.25
TM = 256
LANE = 128
MOD_ROWS = 16
VMEM_LIMIT = 56 * 1024 * 1024

DN_HEADS = 8
DN_HEAD_DIM = 128
DN_WIDTH = DN_HEADS * DN_HEAD_DIM
DN_CONV = 5
DN_CHUNK = 64
DA_HEADS = 8
DA_HEAD_DIM = 64
GA_HEADS = 8
GA_KV_HEADS = 2
GA_HEAD_DIM = 128
SS_GROUP = 16
SS_GROUPS = D_MODEL // SS_GROUP
SS_STATE = 64


def _params(n_axes):
    return pltpu.CompilerParams(dimension_semantics=("arbitrary",) * n_axes,
                                vmem_limit_bytes=VMEM_LIMIT)


def _layer_norm(v, g, b):
    mu = jnp.mean(v, -1, keepdims=True)
    d = v - mu
    var = jnp.mean(d * d, -1, keepdims=True)
    return d * lax.rsqrt(var + 1e-5) * g + b


def _mod_spec(layer, skip=0):
    return pl.BlockSpec((1, 6, D_MODEL),
                        lambda b, r: (layer * MOD_ROWS + jnp.where(r + skip == 0, 8, b), 0, 0))


def _row_spec(width, skip=0):
    return pl.BlockSpec((1, TM, width), lambda b, r: (b, r + skip, 0))


def _ada_kernel(c_ref, w_ref, b_ref, o_ref):
    c = c_ref[...]
    act = (c * jax.nn.sigmoid(c)).astype(BF16)
    o_ref[0] = jnp.dot(act, w_ref[0].astype(BF16), preferred_element_type=F32) + b_ref[0]


def _ada_all(c_rows, ada_w, ada_b):
    tn = 1536
    n = 6 * D_MODEL
    return pl.pallas_call(
        _ada_kernel,
        grid=(DEPTH, n // tn),
        in_specs=[pl.BlockSpec((MOD_ROWS, D_MODEL), lambda i, j: (0, 0)),
                  pl.BlockSpec((1, D_MODEL, tn), lambda i, j: (i, 0, j)),
                  pl.BlockSpec((1, 1, tn), lambda i, j: (i, 0, j))],
        out_specs=pl.BlockSpec((1, MOD_ROWS, tn), lambda i, j: (i, 0, j)),
        out_shape=jax.ShapeDtypeStruct((DEPTH, MOD_ROWS, n), F32),
        compiler_params=_params(2),
        name="ada_mod",
    )(c_rows, ada_w, ada_b.reshape(DEPTH, 1, n))


def _proj_kernel(x_ref, mod_ref, w_ref, o_ref):
    sh = mod_ref[0, 0:1, :]
    sc = mod_ref[0, 1:2, :]
    h = (x_ref[0] * (1.0 + sc) + sh).astype(BF16)
    o_ref[0] = jnp.dot(h, w_ref[...], preferred_element_type=F32).astype(o_ref.dtype)


def _proj(xs, mods, w, layer, out_dtype=F32):
    bsz, lt, _ = xs.shape
    n = w.shape[1]
    return pl.pallas_call(
        _proj_kernel,
        grid=(bsz, lt // TM),
        in_specs=[pl.BlockSpec((1, TM, D_MODEL), lambda b, r: (b, r, 0)),
                  _mod_spec(layer),
                  pl.BlockSpec((D_MODEL, n), lambda b, r: (0, 0))],
        out_specs=pl.BlockSpec((1, TM, n), lambda b, r: (b, r, 0)),
        out_shape=jax.ShapeDtypeStruct((bsz, lt, n), out_dtype),
        compiler_params=_params(2),
        name="mod_proj",
    )(xs, mods, w)


FF_CHUNK = 1024


def _mlp_body(x, mod_ref, w1_ref, w2_ref, g_ref, b_ref, y_ref):
    sh = mod_ref[0, 3:4, :]
    sc = mod_ref[0, 4:5, :]
    gate = mod_ref[0, 5:6, :]
    h = (x * (1.0 + sc) + sh).astype(BF16)
    acc = jnp.zeros((TM, D_MODEL), F32)
    for j in range(D_FF // FF_CHUNK):
        a = jnp.dot(h, w1_ref[:, j * FF_CHUNK:(j + 1) * FF_CHUNK], preferred_element_type=F32)
        a = jnp.square(jnp.maximum(a, 0.0)).astype(BF16)
        acc = acc + jnp.dot(a, w2_ref[j * FF_CHUNK:(j + 1) * FF_CHUNK, :], preferred_element_type=F32)
    y_ref[0] = _layer_norm(DEEPNORM_ALPHA * x + gate * acc, g_ref[...], b_ref[...])


def _mlp_kernel(x_ref, mod_ref, w1_ref, w2_ref, g_ref, b_ref, y_ref):
    _mlp_body(x_ref[0], mod_ref, w1_ref, w2_ref, g_ref, b_ref, y_ref)


def _out_mlp_kernel(o_ref, x_ref, mod_ref, wo_ref, g1_ref, b1_ref, w1_ref, w2_ref, g2_ref, b2_ref, y_ref):
    y = jnp.dot(o_ref[0], wo_ref[...], preferred_element_type=F32)
    x1 = _layer_norm(DEEPNORM_ALPHA * x_ref[0] + mod_ref[0, 2:3, :] * y, g1_ref[...], b1_ref[...])
    _mlp_body(x1, mod_ref, w1_ref, w2_ref, g2_ref, b2_ref, y_ref)


def _out_mlp(o, xs, mods, w_out, w1, w2, ln_g, ln_b, layer, latent_only=False):
    bsz, lt, k = o.shape
    skip = 1 if latent_only else 0
    const = lambda shape: pl.BlockSpec(shape, lambda b, r: (0, 0))
    vec = lambda t: t.reshape(1, D_MODEL)
    return pl.pallas_call(
        _out_mlp_kernel,
        grid=(bsz, lt // TM - skip),
        in_specs=[_row_spec(k, skip), _row_spec(D_MODEL, skip), _mod_spec(layer, skip),
                  const((k, D_MODEL)), const((1, D_MODEL)), const((1, D_MODEL)),
                  const((D_MODEL, D_FF)), const((D_FF, D_MODEL)), const((1, D_MODEL)), const((1, D_MODEL))],
        out_specs=_row_spec(D_MODEL),
        out_shape=jax.ShapeDtypeStruct((bsz, lt - skip * TM, D_MODEL), F32),
        compiler_params=_params(2),
        name="out_mlp_ln",
    )(o, xs, mods, w_out, vec(ln_g[0]), vec(ln_b[0]), w1, w2, vec(ln_g[1]), vec(ln_b[1]))


def _mlp(xs, mods, w1, w2, ln_g, ln_b, layer, latent_only=False):
    bsz, rows, _ = xs.shape
    skip = 1 if latent_only else 0
    return pl.pallas_call(
        _mlp_kernel,
        grid=(bsz, rows // TM),
        in_specs=[_row_spec(D_MODEL),
                  _mod_spec(layer, skip),
                  pl.BlockSpec((D_MODEL, D_FF), lambda b, r: (0, 0)),
                  pl.BlockSpec((D_FF, D_MODEL), lambda b, r: (0, 0)),
                  pl.BlockSpec((1, D_MODEL), lambda b, r: (0, 0)),
                  pl.BlockSpec((1, D_MODEL), lambda b, r: (0, 0))],
        out_specs=_row_spec(D_MODEL),
        out_shape=jax.ShapeDtypeStruct((bsz, rows, D_MODEL), F32),
        compiler_params=_params(2),
        name="mlp_ln",
    )(xs, mods, w1, w2, ln_g.reshape(1, D_MODEL), ln_b.reshape(1, D_MODEL))


def _rope_tables(n_latent, head_dim):
    rows = n_latent // GRID_W
    row = np.repeat(np.arange(rows), GRID_W).astype(np.float32)
    col = np.tile(np.arange(GRID_W), rows).astype(np.float32)
    n_freq = head_dim // 4
    inv_freq = (ROPE_THETA ** (-np.arange(n_freq, dtype=np.float32) / n_freq)).astype(np.float32)
    ang = np.concatenate([row[:, None] * inv_freq, col[:, None] * inv_freq], -1)
    cos = np.repeat(np.cos(ang.astype(np.float64)), 2, axis=-1)
    sin = np.repeat(np.sin(ang.astype(np.float64)), 2, axis=-1) * np.tile([-1.0, 1.0], head_dim // 2)
    reps = LANE // head_dim
    cos = np.concatenate([np.ones((CTX_LEN, LANE)), np.tile(cos, (1, reps))], 0)
    sin = np.concatenate([np.zeros((CTX_LEN, LANE)), np.tile(sin, (1, reps))], 0)
    return jnp.asarray(cos, F32), jnp.asarray(sin, F32)


ATTN_KEY_CHUNKS = 3


def _rope(x, cos, sin_signed):
    lane = lax.broadcasted_iota(jnp.int32, x.shape, 1)
    nxt = pltpu.roll(x, LANE - 1, 1)
    prv = pltpu.roll(x, 1, 1)
    swapped = jnp.where(lane % 2 == 0, nxt, prv)
    return x * cos + swapped * sin_signed


def _rms(x, g):
    return x * lax.rsqrt(jnp.mean(x * x, -1, keepdims=True) + 1e-6) * g


def _attn_kernel(q_ref, qn_ref, qnn_ref, k_ref, v_ref, cos_ref, sin_ref, qg_ref, kg_ref, lam_ref, ng_ref, o_ref,
                 kb_ref, vt_ref, s_even_ref, s_odd_ref, qs_ref, *, n_maps, qk_norm, scale, lambda_init):
    g = pl.program_id(2)
    qt = pl.program_id(3)
    lt = kb_ref.shape[0]
    n_qt = lt // TM

    @pl.when((g == 0) & (qt == 0))
    def _prep_kv():
        k = k_ref[0]
        if qk_norm:
            k = _rms(k, kg_ref[...])
        kb_ref[...] = _rope(k, cos_ref[...], sin_ref[...]).astype(BF16)
        vt_ref[...] = v_ref[0].T.astype(BF16)

    def queries(ref, tile):
        q = ref[0]
        if qk_norm:
            q = _rms(q, qg_ref[...])
        row0 = pl.multiple_of(tile * TM, TM)
        q = _rope(q, cos_ref[pl.ds(row0, TM), :], sin_ref[pl.ds(row0, TM), :]) * (scale * math.log2(math.e))
        return q.astype(BF16)

    def query_maps(qb):
        if n_maps == 1:
            return (qb,)
        lane = lax.broadcasted_iota(jnp.int32, qb.shape, 1)
        zero = jnp.zeros_like(qb)
        return (jnp.where(lane < LANE // 2, qb, zero), jnp.where(lane >= LANE // 2, qb, zero))

    def scores_t(qm, rows):
        return lax.dot_general(kb_ref[rows, :], qm, (((1,), (1,)), ((), ())), preferred_element_type=F32)

    def finish(acc, l):
        if n_maps == 1:
            return (acc[0] * (1.0 / l[0])).T
        lp = lam_ref[...]
        lam = (jnp.exp(jnp.sum(lp[0:1, :] * lp[1:2, :])) - jnp.exp(jnp.sum(lp[2:3, :] * lp[3:4, :]))
               + lambda_init)
        o = (acc[0] * (1.0 / l[0]) - acc[1] * (lam / l[1])).T
        return _rms(o, ng_ref[...]) * (1.0 - lambda_init)

    @pl.when(qt == 0)
    def _first():
        ctx_keys = slice(0, CTX_LEN)
        acc, l = [], []
        for qm in query_maps(queries(q_ref, 0)):
            s = scores_t(qm, ctx_keys)
            e = jnp.exp2(s - jnp.max(s, 0, keepdims=True))
            l.append(jnp.sum(e, 0, keepdims=True))
            acc.append(jnp.dot(vt_ref[:, ctx_keys], e.astype(BF16), preferred_element_type=F32))
        o_ref[0] = finish(acc, l).astype(o_ref.dtype)
        for i, qm in enumerate(query_maps(queries(qn_ref, 1))):
            s_odd_ref[i] = scores_t(qm, slice(0, lt))
        qs_ref[...] = queries(qnn_ref, jnp.minimum(2, n_qt - 1))

    def steady(cur_ref, nxt_ref):
        ch = lt // ATTN_KEY_CHUNKS
        m = [jnp.max(cur_ref[i], 0, keepdims=True) for i in range(n_maps)]
        qn_maps = query_maps(qs_ref[...])
        l = [jnp.zeros((1, TM), F32) for _ in range(n_maps)]
        acc = [jnp.zeros((LANE, TM), F32) for _ in range(n_maps)]
        for j in range(ATTN_KEY_CHUNKS):
            rows = slice(j * ch, (j + 1) * ch)
            for i in range(n_maps):
                nxt_ref[i, rows, :] = scores_t(qn_maps[i], rows)
            for i in range(n_maps):
                e = jnp.exp2(cur_ref[i, rows, :] - m[i])
                l[i] = l[i] + jnp.sum(e, 0, keepdims=True)
                acc[i] = acc[i] + jnp.dot(vt_ref[:, rows], e.astype(BF16), preferred_element_type=F32)
            if j == 0:
                qs_ref[...] = queries(qnn_ref, jnp.minimum(qt + 2, n_qt - 1))
        o_ref[0] = finish(acc, l).astype(o_ref.dtype)

    for parity, (cur_ref, nxt_ref) in enumerate(((s_even_ref, s_odd_ref), (s_odd_ref, s_even_ref))):
        @pl.when((qt > 0) & (qt % 2 == parity))
        def _steady(cur_ref=cur_ref, nxt_ref=nxt_ref):
            steady(cur_ref, nxt_ref)


def _attention(p, cos, sin, q_gain, k_gain, lam_p, norm_g, *, n_kv, group, q_col, k_col, v_col,
               n_maps, qk_norm, scale, lambda_init):
    bsz, lt, _ = p.shape
    n_qt = lt // TM
    kernel = functools.partial(_attn_kernel, n_maps=n_maps, qk_norm=qk_norm, scale=scale,
                               lambda_init=lambda_init)
    const = lambda b, kv, g, qt: (0, 0)
    return pl.pallas_call(
        kernel,
        grid=(bsz, n_kv, group, n_qt),
        in_specs=[pl.BlockSpec((1, TM, LANE), lambda b, kv, g, qt: (b, qt, q_col + kv * group + g)),
                  pl.BlockSpec((1, TM, LANE),
                               lambda b, kv, g, qt: (b, jnp.minimum(qt + 1, n_qt - 1), q_col + kv * group + g)),
                  pl.BlockSpec((1, TM, LANE),
                               lambda b, kv, g, qt: (b, jnp.minimum(qt + 2, n_qt - 1), q_col + kv * group + g)),
                  pl.BlockSpec((1, lt, LANE), lambda b, kv, g, qt: (b, 0, k_col + kv)),
                  pl.BlockSpec((1, lt, LANE), lambda b, kv, g, qt: (b, 0, v_col + kv)),
                  pl.BlockSpec((lt, LANE), const),
                  pl.BlockSpec((lt, LANE), const),
                  pl.BlockSpec((1, LANE), const),
                  pl.BlockSpec((1, LANE), const),
                  pl.BlockSpec(lam_p.shape, const),
                  pl.BlockSpec((1, LANE), const)],
        out_specs=pl.BlockSpec((1, TM, LANE), lambda b, kv, g, qt: (b, qt, kv * group + g)),
        out_shape=jax.ShapeDtypeStruct((bsz, lt, n_kv * group * LANE), BF16),
        scratch_shapes=[pltpu.VMEM((lt, LANE), BF16), pltpu.VMEM((LANE, lt), BF16),
                        pltpu.VMEM((n_maps, lt, TM), F32), pltpu.VMEM((n_maps, lt, TM), F32),
                        pltpu.VMEM((TM, LANE), BF16)],
        compiler_params=_params(4),
        name="attention",
    )(p, p, p, p, p, cos, sin, q_gain.reshape(1, LANE), k_gain.reshape(1, LANE), lam_p, norm_g.reshape(1, LANE))


def _s5_out_kernel(y_ref, x_ref, mod_ref, d_ref, w_ref, g_ref, b_ref, o_ref):
    sh = mod_ref[0, 0:1, :]
    sc = mod_ref[0, 1:2, :]
    gate = mod_ref[0, 2:3, :]
    x = x_ref[0]
    u = x * (1.0 + sc) + sh
    y = jax.nn.gelu(y_ref[0] + d_ref[...] * u).astype(BF16)
    z = jnp.dot(y, w_ref[...], preferred_element_type=F32)
    out = z[:, :D_MODEL] * jax.nn.sigmoid(z[:, D_MODEL:])
    o_ref[0] = _layer_norm(DEEPNORM_ALPHA * x + gate * out, g_ref[...], b_ref[...])


def _s5_out(y, xs, mods, d_skip, w_glu, ln_g, ln_b, layer, latent_only=False):
    bsz, lt, _ = xs.shape
    skip = 1 if latent_only else 0
    return pl.pallas_call(
        _s5_out_kernel,
        grid=(bsz, lt // TM - skip),
        in_specs=[_row_spec(D_MODEL, skip),
                  _row_spec(D_MODEL, skip),
                  _mod_spec(layer, skip),
                  pl.BlockSpec((1, D_MODEL), lambda b, r: (0, 0)),
                  pl.BlockSpec((D_MODEL, 2 * D_MODEL), lambda b, r: (0, 0)),
                  pl.BlockSpec((1, D_MODEL), lambda b, r: (0, 0)),
                  pl.BlockSpec((1, D_MODEL), lambda b, r: (0, 0))],
        out_specs=_row_spec(D_MODEL),
        out_shape=jax.ShapeDtypeStruct((bsz, lt - skip * TM, D_MODEL), F32),
        compiler_params=_params(2),
        name="s5_out_ln",
    )(y, xs, mods, d_skip.reshape(1, D_MODEL), w_glu, ln_g.reshape(1, D_MODEL), ln_b.reshape(1, D_MODEL))


DN_BLK = 128
DN_PREP_BLOCKS = 3
DN_HALO = 8
DN_GATE_BETA_F, DN_GATE_BETA_B, DN_GATE_A_F, DN_GATE_A_B = 0, DN_HEADS, 2 * DN_HEADS, 3 * DN_HEADS
_NT = (((1,), (1,)), ((), ()))
_TN = (((0,), (0,)), ((), ()))


def _dn_kernel(q_ref, k_ref, v_ref, z_ref, g_ref, cq_ref, ck_ref, cv_ref, alog_ref, dtb_ref, ng_ref, o_ref,
               beta_ref, gc_ref, tot_ref, gct_ref, qn_ref, kn_ref, vn_ref, pad_ref,
               wq_f, u_f, qk_f, kd_f, gl_f, o_f, wq_b, u_b, qk_b, kd_b, gl_b, o_b, *, n_blk, n_ctx_blk):
    h = pl.program_id(1)
    lt = n_blk * DN_BLK
    ctx_rows = n_ctx_blk * DN_BLK
    row = lax.broadcasted_iota(jnp.int32, (lt, 1), 0)
    lane = lax.broadcasted_iota(jnp.int32, (1, LANE), 1)

    @pl.when(h == 0)
    def _gates():
        gts = g_ref[0]
        beta_ref[...] = jax.nn.sigmoid(gts)
        g = -jnp.exp(alog_ref[...]) * jax.nn.softplus(gts + dtb_ref[...])
        pos = row % DN_BLK
        pre = g
        suf = g
        s = 1
        while s < DN_BLK:
            pre = pre + jnp.where(pos >= s, pltpu.roll(pre, s, 0), 0.0)
            suf = suf + jnp.where(pos < DN_BLK - s, pltpu.roll(suf, lt - s, 0), 0.0)
            s *= 2
        gc = jnp.where(lane >= DN_GATE_A_B, suf, pre)
        gc_ref[...] = gc
        tot_ref[...] = pre + suf - g
        for c in range(n_blk):
            gct_ref[c] = gc[c * DN_BLK:(c + 1) * DN_BLK, :].T

    def conv_silu(x_ref, w_ref):
        pad_ref[0:DN_HALO, :] = jnp.zeros((DN_HALO, LANE), F32)
        pad_ref[DN_HALO + ctx_rows:2 * DN_HALO + ctx_rows, :] = jnp.zeros((DN_HALO, LANE), F32)
        pad_ref[2 * DN_HALO + lt:3 * DN_HALO + lt, :] = jnp.zeros((DN_HALO, LANE), F32)
        pad_ref[DN_HALO:DN_HALO + ctx_rows, :] = x_ref[0, 0:ctx_rows, :]
        pad_ref[2 * DN_HALO + ctx_rows:2 * DN_HALO + lt, :] = x_ref[0, ctx_rows:lt, :]
        w = w_ref[...]
        parts = []
        for base, n in ((DN_HALO, ctx_rows), (2 * DN_HALO + ctx_rows, lt - ctx_rows)):
            acc = None
            for k in range(DN_CONV):
                tap = pad_ref[base + k - DN_CONV // 2:base + k - DN_CONV // 2 + n, :] * w[k:k + 1, :]
                acc = tap if acc is None else acc + tap
            parts.append(acc * jax.nn.sigmoid(acc))
        return jnp.concatenate(parts, 0)

    def l2n(x):
        return x * lax.rsqrt(jnp.sum(x * x, -1, keepdims=True) + 1e-6)

    qn_ref[...] = l2n(conv_silu(q_ref, cq_ref)) * DN_HEAD_DIM ** -0.5
    kn_ref[...] = l2n(conv_silu(k_ref, ck_ref))
    vn_ref[...] = conv_silu(v_ref, cv_ref)

    ii = lax.broadcasted_iota(jnp.int32, (DN_BLK, DN_BLK), 0)
    jj = lax.broadcasted_iota(jnp.int32, (DN_BLK, DN_BLK), 1)
    fwd =(DN_GATE_BETA_F, DN_GATE_A_F, ii >= jj, ii > jj, (wq_f, u_f, qk_f, kd_f, gl_f, o_f))
    bwd = (DN_GATE_BETA_B, DN_GATE_A_B, ii <= jj, ii < jj, (wq_b, u_b, qk_b, kd_b, gl_b, o_b))

    ii2 = lax.broadcasted_iota(jnp.int32, (2 * DN_BLK, 2 * DN_BLK), 0)
    jj2 = lax.broadcasted_iota(jnp.int32, (2 * DN_BLK, 2 * DN_BLK), 1)
    eye2 = (ii2 == jj2).astype(F32)
    zero_blk = jnp.zeros((DN_BLK, DN_BLK), F32)

    def pair_off(s):
        return ((ii2 // (2 * s)) == (jj2 // (2 * s))) & ((ii2 // s) != (jj2 // s))

    def column(ref, rows, lane_idx):
        return jnp.sum(jnp.where(lane == lane_idx, ref[rows, :], 0.0), -1, keepdims=True)

    def block_inputs(c):
        rows = pl.ds(pl.multiple_of(c * DN_BLK, DN_BLK), DN_BLK)
        qc, kc, vc = qn_ref[rows, :], kn_ref[rows, :], vn_ref[rows, :]
        kb = kc.astype(BF16)
        kk = lax.dot_general(kb, kb, _NT, preferred_element_type=F32)
        qk = lax.dot_general(qc.astype(BF16), kb, _NT, preferred_element_type=F32)
        a_dir, rhs_dir = [], []
        for lane_beta, lane_g, incl, strict, (wq_ref, _, qk_ref, kd_ref, gl_ref, _) in (fwd, bwd):
            bcol = column(beta_ref, rows, lane_beta + h)
            gcol = column(gc_ref, rows, lane_g + h)
            tcol = column(tot_ref, rows, lane_g + h)
            grow = gct_ref[c, pl.ds(lane_g + h, 1), :]
            dec = jnp.exp(jnp.where(incl, gcol - grow, -jnp.inf))
            a_dir.append(jnp.where(strict, bcol * kk * dec, 0.0))
            egc = jnp.exp(gcol)
            rhs_dir.append(jnp.concatenate([bcol * vc, (bcol * egc) * kc], -1).astype(BF16))
            wq_ref[c, DN_BLK:2 * DN_BLK, :] = (qc * egc).astype(BF16)
            qk_ref[c] = (qk * dec).astype(BF16)
            kd_ref[c] = (kc * jnp.exp(tcol - gcol)).T.astype(BF16)
            gl_ref[c] = jnp.broadcast_to(jnp.exp(tcol[0:1, :]), (8, LANE))
        a = jnp.concatenate([jnp.concatenate([a_dir[0], zero_blk], 1),
                             jnp.concatenate([zero_blk, a_dir[1]], 1)], 0)
        return a, jnp.concatenate(rhs_dir, 0)

    def prepare(it, _):
        blocks = [it * DN_PREP_BLOCKS + i for i in range(DN_PREP_BLOCKS)]
        a_rhs = [block_inputs(c) for c in blocks]
        t_inv, m = [], []
        for a, _ in a_rhs:
            a_off = jnp.where(pair_off(1), a, 0.0)
            t_inv.append(eye2 - a_off)
            neighbour = jnp.where(ii2 < DN_BLK, pltpu.roll(a, 1, 0), pltpu.roll(a, 2 * DN_BLK - 1, 0))
            m.append(a - jnp.sum(a_off, -1, keepdims=True) * neighbour)
        s = 2
        while s < DN_BLK:
            for i in range(DN_PREP_BLOCKS):
                m_off = jnp.where(pair_off(s), m[i], 0.0).astype(BF16)
                if 2 * s < DN_BLK:
                    upd = jnp.dot(m_off, jnp.concatenate([t_inv[i], m[i]], 1).astype(BF16),
                                  preferred_element_type=F32)
                    t_inv[i] = t_inv[i] - upd[:, :2 * DN_BLK]
                    m[i] = m[i] - upd[:, 2 * DN_BLK:]
                else:
                    t_inv[i] = t_inv[i] - jnp.dot(m_off, t_inv[i].astype(BF16), preferred_element_type=F32)
            s *= 2
        for i, c in enumerate(blocks):
            uw = jnp.dot(t_inv[i].astype(BF16), a_rhs[i][1], preferred_element_type=F32)
            for d, (wq_ref, u_ref) in enumerate(((wq_f, u_f), (wq_b, u_b))):
                u_ref[c] = uw[d * DN_BLK:(d + 1) * DN_BLK, :DN_HEAD_DIM]
                wq_ref[c, 0:DN_BLK, :] = uw[d * DN_BLK:(d + 1) * DN_BLK, DN_HEAD_DIM:].astype(BF16)
        return 0

    lax.fori_loop(0, n_blk // DN_PREP_BLOCKS, prepare, 0)

    zero_bf = jnp.zeros((DN_BLK, DN_BLK), BF16)

    def block_diag(top, bottom):
        return jnp.concatenate([jnp.concatenate([top, zero_bf], 1), jnp.concatenate([zero_bf, bottom], 1)], 0)

    def step(k, carry):
        s_f, s_b = carry
        cb = jnp.where(k < n_ctx_blk, n_ctx_blk - 1 - k, n_blk + n_ctx_blk - 1 - k)
        hd = DN_HEAD_DIM
        r = jnp.dot(jnp.concatenate([wq_f[k], wq_b[cb]], 1), block_diag(s_f.astype(BF16), s_b.astype(BF16)),
                    preferred_element_type=F32)
        v_new = block_diag((u_f[k] - r[:DN_BLK, :hd]).astype(BF16), (u_b[cb] - r[:DN_BLK, hd:]).astype(BF16))
        lhs = jnp.concatenate([jnp.concatenate([qk_f[k], qk_b[cb]], 1),
                               jnp.concatenate([kd_f[k], kd_b[cb]], 1)], 0)
        r2 = jnp.dot(lhs, v_new, preferred_element_type=F32)
        o_f[k] = r[DN_BLK:, :hd] + r2[:DN_BLK, :hd]
        o_b[cb] = r[DN_BLK:, hd:] + r2[:DN_BLK, hd:]
        return (gl_f[k][0:1, :] * s_f + r2[DN_BLK:, :hd], gl_b[cb][0:1, :] * s_b + r2[DN_BLK:, hd:])

    zero = jnp.zeros((DN_HEAD_DIM, DN_HEAD_DIM), F32)
    lax.fori_loop(0, n_blk, step, (zero, zero))

    o = (o_f[...] + o_b[...]).reshape(lt, DN_HEAD_DIM)
    o = o * lax.rsqrt(jnp.mean(o * o, -1, keepdims=True) + 1e-6) * ng_ref[...]
    z = z_ref[0]
    o_ref[0] = (o * (z * jax.nn.sigmoid(z))).astype(o_ref.dtype)


def _deltanet_core(p, conv_w, a_log, dt_bias, norm_g):
    bsz, lt, _ = p.shape
    n_blk = lt // DN_BLK
    gate_pad = lambda t: jnp.pad(t.reshape(1, 2 * DN_HEADS), ((0, 0), (DN_GATE_A_F, LANE - DN_GATE_A_F - 2 * DN_HEADS)))
    kernel = functools.partial(_dn_kernel, n_blk=n_blk, n_ctx_blk=CTX_LEN // DN_BLK)
    col = lambda off: pl.BlockSpec((1, lt, LANE), lambda b, h: (b, 0, off + h))
    cw = lambda off: pl.BlockSpec((DN_CONV, LANE), lambda b, h: (0, off + h))
    const = pl.BlockSpec((1, LANE), lambda b, h: (0, 0))
    seq = lambda dt: pltpu.VMEM((lt, LANE), dt)
    blk = lambda rows, dt: pltpu.VMEM((n_blk, rows, LANE), dt)
    per_dir = [blk(2 * DN_BLK, BF16), blk(DN_BLK, F32), blk(DN_BLK, BF16), blk(DN_BLK, BF16), blk(8, F32),
               blk(DN_BLK, F32)]
    return pl.pallas_call(
        kernel,
        grid=(bsz, DN_HEADS),
        in_specs=[col(0), col(DN_HEADS), col(2 * DN_HEADS), col(3 * DN_HEADS),
                  pl.BlockSpec((1, lt, LANE), lambda b, h: (b, 0, 4 * DN_HEADS)),
                  cw(0), cw(DN_HEADS), cw(2 * DN_HEADS), const, const, const],
        out_specs=pl.BlockSpec((1, lt, LANE), lambda b, h: (b, 0, h)),
        out_shape=jax.ShapeDtypeStruct((bsz, lt, DN_WIDTH), BF16),
        scratch_shapes=[seq(F32), seq(F32), seq(F32), blk(LANE, F32), seq(F32), seq(F32), seq(F32),
                        pltpu.VMEM((lt + 3 * DN_HALO, LANE), F32)] + per_dir + per_dir,
        compiler_params=_params(2),
        name="deltanet_core",
    )(p, p, p, p, p, conv_w, conv_w, conv_w, gate_pad(a_log), gate_pad(dt_bias), norm_g.reshape(1, LANE))


S5_T = 8
S5_BB = 4
S5_BLK_GROUPS = LANE // SS_GROUP
S5_HALF = S5_BLK_GROUPS * SS_STATE
S5_K = S5_T * LANE


def _s5_weights(a_re, a_im, log_dt, b_re, b_im, c_re, c_im):
    t = S5_T
    nj = SS_GROUPS // S5_BLK_GROUPS
    ar = jnp.minimum(a_re, -1e-4)
    ai = a_im
    dt = jnp.exp(log_dt)[..., None]
    zr, zi = ar * dt, ai * dt
    d = jnp.arange(t + 1, dtype=F32)[:, None, None, None]
    mag = jnp.exp(d * zr)
    er, ei = mag * jnp.cos(d * zi), mag * jnp.sin(d * zi)
    nr, ni = er[1] - 1.0, ei[1]
    den = ar * ar + ai * ai
    fr, fi = (nr * ar + ni * ai) / den, (ni * ar - nr * ai) / den
    bt_re, bt_im = b_re.transpose(0, 1, 3, 2), b_im.transpose(0, 1, 3, 2)
    bbr = fr[:, :, None, :] * bt_re - fi[:, :, None, :] * bt_im
    bbi = fr[:, :, None, :] * bt_im + fi[:, :, None, :] * bt_re
    cer = c_re[None] * er[:, :, :, None, :] - c_im[None] * ei[:, :, :, None, :]
    cei = c_re[None] * ei[:, :, :, None, :] + c_im[None] * er[:, :, :, None, :]
    kk = jnp.einsum('tdgiq,dgjq->tdgij', jnp.concatenate([cer, -cei], -1), jnp.concatenate([bbr, bbi], -1),
                    precision=lax.Precision.HIGH)
    s_idx = jnp.arange(t)[:, None]
    t_idx = jnp.arange(t)[None, :]
    lag_f = jnp.clip(t_idx - s_idx, 0, t)
    lag_b = jnp.clip(s_idx - t_idx, 0, t)
    m_f = (t_idx >= s_idx)[:, :, None, None, None]
    m_b = (s_idx >= t_idx)[:, :, None, None, None]
    kst = jnp.where(m_f, kk[lag_f, 0], 0.0) + jnp.where(m_b, kk[lag_b, 1], 0.0)
    kst = kst.reshape(t, t, nj, S5_BLK_GROUPS, SS_GROUP, SS_GROUP).transpose(2, 0, 3, 5, 1, 4)
    m_intra = kst.reshape(nj, S5_K, t * SS_GROUP)

    pow_f = (t - 1 - jnp.arange(t))
    pow_b = jnp.arange(t)

    def carry_w(pw, dd):
        e_r, e_i = er[pw, dd][:, :, None, :], ei[pw, dd][:, :, None, :]
        re = e_r * bbr[dd][None] - e_i * bbi[dd][None]
        im = e_r * bbi[dd][None] + e_i * bbr[dd][None]
        return jnp.stack([re, im], 0)

    wb = jnp.stack([carry_w(pow_f, 0), carry_w(pow_b, 1)], 0)
    wb = wb.reshape(2, 2, t, nj, S5_BLK_GROUPS, SS_GROUP, SS_STATE)
    m_carry = wb.transpose(3, 2, 4, 5, 0, 1, 6).reshape(nj, S5_K, 4 * SS_STATE)

    def read_w(pw, dd):
        wc = jnp.stack([cer[pw, dd], -cei[pw, dd]], 0)
        wc = wc.reshape(2, t, nj, S5_BLK_GROUPS, SS_GROUP, SS_STATE)
        return wc.transpose(2, 1, 4, 0, 3, 5).reshape(nj, t * SS_GROUP, 2 * S5_HALF)

    m_read_f = read_w(jnp.arange(t) + 1, 0)
    m_read_b = read_w(t - jnp.arange(t), 1)
    lam_t = jnp.stack([er[t, 0], ei[t, 0], er[t, 1], ei[t, 1]], 0)
    lam_t = lam_t.reshape(4, nj, S5_HALF).transpose(1, 0, 2)
    return m_carry.astype(BF16), m_intra.astype(BF16), m_read_f.astype(BF16), m_read_b.astype(BF16), lam_t


def _s5_expand(m, row_group_div, col_seg, n_col, transposed=False):
    n_in, rows = m.shape if transposed else m.shape[::-1]
    q = lax.broadcasted_iota(jnp.int32, (n_in, n_col), 0)
    c = lax.broadcasted_iota(jnp.int32, (n_in, n_col), 1)
    out_seg = col_seg * S5_BLK_GROUPS
    rep = ((q // col_seg == c // out_seg) & (q % col_seg == c % col_seg)).astype(BF16)
    wide = lax.dot_general(m, rep, (((0 if transposed else 1,), (0,)), ((), ())), preferred_element_type=F32)
    g_row = (lax.broadcasted_iota(jnp.int32, (rows, n_col), 0) // row_group_div) % S5_BLK_GROUPS
    g_col = (lax.broadcasted_iota(jnp.int32, (rows, n_col), 1) // col_seg) % S5_BLK_GROUPS
    return jnp.where(g_row == g_col, wide, 0.0).astype(BF16)


def _s5_kernel(x_ref, mod_ref, mc_ref, mi_ref, mrf_ref, mrb_ref, lt_ref, y_ref,
               wb_ref, wi_ref, wcf_ref, wcb_ref, xcat_ref, hf_ref, hb_ref, *, n_chunk, n_ctx_chunk):
    bh = pl.program_id(1)

    @pl.when(bh == 0)
    def _expand_weights():
        wb_ref[...] = _s5_expand(mc_ref[0], SS_GROUP, SS_STATE, 4 * S5_HALF)
        wi_ref[...] = _s5_expand(mi_ref[0], SS_GROUP, SS_GROUP, S5_K)
        wcf_ref[...] = _s5_expand(mrf_ref[0], SS_STATE, SS_GROUP, S5_K, transposed=True)
        wcb_ref[...] = _s5_expand(mrb_ref[0], SS_STATE, SS_GROUP, S5_K, transposed=True)

    nb = x_ref.shape[0]
    n_lb = hf_ref.shape[0]
    half_lb = n_lb // 2
    is_ctx = lax.broadcasted_iota(jnp.int32, (n_chunk, 1), 0) < n_ctx_chunk
    sh_c = mod_ref[8, 0:1, :]
    sc_c = mod_ref[8, 1:2, :]

    def carry_in(bl, _):
        mb = mod_ref[bh * nb + bl]
        sh = jnp.where(is_ctx, sh_c, mb[0:1, :])
        sc1 = 1.0 + jnp.where(is_ctx, sc_c, mb[1:2, :])
        rows = pl.ds(pl.multiple_of(bl * n_chunk, n_chunk), n_chunk)
        for tau in range(S5_T):
            xt = x_ref[bl, pl.ds(tau, n_chunk, stride=S5_T), :]
            xcat_ref[rows, tau * LANE:(tau + 1) * LANE] = (xt * sc1 + sh).astype(BF16)
        hbv = jnp.dot(xcat_ref[rows, :], wb_ref[...], preferred_element_type=F32)
        for l in range(n_lb):
            hf_ref[l, rows, :] = hbv[:, l * LANE:(l + 1) * LANE]
            hb_ref[l, rows, :] = hbv[:, (n_lb + l) * LANE:(n_lb + l + 1) * LANE]
        return 0

    lax.fori_loop(0, nb, carry_in, 0)

    lt = lt_ref[0]

    def lam_blocks(i):
        return [jnp.broadcast_to(lt[i:i + 1, l * LANE:(l + 1) * LANE], (nb, LANE)) for l in range(half_lb)]

    a_f = (lam_blocks(0), lam_blocks(1))
    a_b = (lam_blocks(2), lam_blocks(3))

    def advance(h_ref, c, a, state):
        a_re, a_im = a
        new = []
        for l in range(half_lb):
            sel = pl.ds(c, nb, stride=n_chunk)
            in_re = h_ref[l, sel, :]
            in_im = h_ref[half_lb + l, sel, :]
            s_re, s_im = state[l], state[half_lb + l]
            h_ref[l, sel, :] = s_re
            h_ref[half_lb + l, sel, :] = s_im
            new.append((a_re[l] * s_re - a_im[l] * s_im + in_re, a_re[l] * s_im + a_im[l] * s_re + in_im))
        return tuple(n[0] for n in new) + tuple(n[1] for n in new)

    def step(k, carry):
        st_f, st_b = carry
        cb = jnp.where(k < n_ctx_chunk, n_ctx_chunk - 1 - k, n_chunk + n_ctx_chunk - 1 - k)
        return advance(hf_ref, k, a_f, st_f), advance(hb_ref, cb, a_b, st_b)

    zero = tuple(jnp.zeros((nb, LANE), F32) for _ in range(n_lb))
    lax.fori_loop(0, n_chunk, step, (zero, zero))

    def read_out(bl, _):
        rows = pl.ds(pl.multiple_of(bl * n_chunk, n_chunk), n_chunk)
        h_f = jnp.concatenate([hf_ref[l, rows, :] for l in range(n_lb)], -1).astype(BF16)
        h_b = jnp.concatenate([hb_ref[l, rows, :] for l in range(n_lb)], -1).astype(BF16)
        y = (jnp.dot(xcat_ref[rows, :], wi_ref[...], preferred_element_type=F32)
             + jnp.dot(h_f, wcf_ref[...], preferred_element_type=F32)
             + jnp.dot(h_b, wcb_ref[...], preferred_element_type=F32))
        for t in range(S5_T):
            y_ref[bl, pl.ds(t, n_chunk, stride=S5_T), :] = y[:, t * LANE:(t + 1) * LANE]
        return 0

    lax.fori_loop(0, nb, read_out, 0)


def _s5_core(xs, mods, weights, layer):
    bsz, lt, _ = xs.shape
    m_carry, m_intra, m_read_f, m_read_b, lam_t = weights
    nj = D_MODEL // LANE
    nb = min(S5_BB, bsz)
    n_chunk = lt // S5_T
    kernel = functools.partial(_s5_kernel, n_chunk=n_chunk, n_ctx_chunk=CTX_LEN // S5_T)
    wspec = lambda shape: pl.BlockSpec((1,) + shape, lambda j, bh: (j, 0, 0))
    return pl.pallas_call(
        kernel,
        grid=(nj, bsz // nb),
        in_specs=[pl.BlockSpec((nb, lt, LANE), lambda j, bh: (bh, 0, j)),
                  pl.BlockSpec((MOD_ROWS, 6, LANE), lambda j, bh: (layer, 0, j)),
                  wspec(m_carry.shape[1:]), wspec(m_intra.shape[1:]),
                  wspec(m_read_f.shape[1:]), wspec(m_read_b.shape[1:]),
                  wspec((4, S5_HALF))],
        out_specs=pl.BlockSpec((nb, lt, LANE), lambda j, bh: (bh, 0, j)),
        out_shape=jax.ShapeDtypeStruct((bsz, lt, D_MODEL), F32),
        scratch_shapes=[pltpu.VMEM((S5_K, 4 * S5_HALF), BF16), pltpu.VMEM((S5_K, S5_K), BF16),
                        pltpu.VMEM((2 * S5_HALF, S5_K), BF16), pltpu.VMEM((2 * S5_HALF, S5_K), BF16),
                        pltpu.VMEM((nb * n_chunk, S5_K), BF16),
                        pltpu.VMEM((2 * S5_HALF // LANE, nb * n_chunk, LANE), F32),
                        pltpu.VMEM((2 * S5_HALF // LANE, nb * n_chunk, LANE), F32)],
        compiler_params=_params(2),
        name="s5_core",
    )(xs, mods, m_carry, m_intra, m_read_f, m_read_b, lam_t)


def kernel(x, c, ctx, c_ctx, ada_w, ada_b, ln_g, ln_b, mlp_w1, mlp_w2, dn_w_in, dn_conv, dn_a_log, dn_dt_bias, dn_norm_g, dn_w_out, da_w_qkv, da_lambda, da_norm_g, da_w_out, ga_w_qkv, ga_q_norm, ga_k_norm, ga_w_out, ss_a_re, ss_a_im, ss_log_dt, ss_b_re, ss_b_im, ss_c_re, ss_c_im, ss_d, ss_w_glu):
    bsz, n_latent, _ = x.shape
    xs = jnp.concatenate([ctx, x], 1)
    c_rows = jnp.concatenate([c, c_ctx[None, :], jnp.zeros((MOD_ROWS - bsz - 1, D_MODEL), F32)], 0)
    mods = _ada_all(c_rows, ada_w, ada_b).reshape(DEPTH * MOD_ROWS, 6, D_MODEL)
    ones = jnp.ones((LANE,), F32)
    zeros4 = jnp.zeros((4, DA_HEAD_DIM), F32)

    for i in range(DEPTH):
        m, j = i % 4, i // 4
        last = i == DEPTH - 1
        if m == 0:
            w_in = jnp.pad(dn_w_in[j], ((0, 0), (0, LANE - 4 * DN_HEADS))).astype(BF16)
            p = _proj(xs, mods, w_in, i)
            o = _deltanet_core(p, dn_conv[j], dn_a_log[j], dn_dt_bias[j], dn_norm_g[j])
            w_out = dn_w_out[j]
        elif m == 1:
            lambda_init = 0.8 - 0.6 * math.exp(-0.3 * i)
            p = _proj(xs, mods, da_w_qkv[j].astype(BF16), i)
            cos, sin = _rope_tables(n_latent, DA_HEAD_DIM)
            o = _attention(p, cos, sin, ones, ones, da_lambda[j], da_norm_g[j],
                           n_kv=DA_HEADS, group=1, q_col=0, k_col=DA_HEADS, v_col=2 * DA_HEADS,
                           n_maps=2, qk_norm=False, scale=DA_HEAD_DIM ** -0.5, lambda_init=lambda_init)
            w_out = da_w_out[j]
        elif m == 2:
            p = _proj(xs, mods, ga_w_qkv[j].astype(BF16), i)
            cos, sin = _rope_tables(n_latent, GA_HEAD_DIM)
            o = _attention(p, cos, sin, ga_q_norm[j], ga_k_norm[j], zeros4, ones,
                           n_kv=GA_KV_HEADS, group=GA_HEADS // GA_KV_HEADS, q_col=0, k_col=GA_HEADS,
                           v_col=GA_HEADS + GA_KV_HEADS, n_maps=1, qk_norm=True,
                           scale=GA_HEAD_DIM ** -0.5, lambda_init=0.0)
            w_out = ga_w_out[j]
        w1, w2 = mlp_w1[i].astype(BF16), mlp_w2[i].astype(BF16)
        if m == 3:
            weights = _s5_weights(ss_a_re[j], ss_a_im[j], ss_log_dt[j], ss_b_re[j], ss_b_im[j],
                                  ss_c_re[j], ss_c_im[j])
            y = _s5_core(xs, mods, weights, i)
            xs = _s5_out(y, xs, mods, ss_d[j], ss_w_glu[j].astype(BF16), ln_g[i, 0], ln_b[i, 0], i, last)
            xs = _mlp(xs, mods, w1, w2, ln_g[i, 1], ln_b[i, 1], i, last)
        else:
            xs = _out_mlp(o, xs, mods, w_out.astype(BF16), w1, w2, ln_g[i], ln_b[i], i, last)
    return xs
```
